```python
import jax
import jax.numpy as jnp
from jax import lax
import numpy as np

D_MODEL = 1024
BATCH = 2
SEQ = 8192
DEPTH = 1
DEC_BATCH = 16
DEC_SEQ = 32
PAST_LEN = 4096

CHUNK = 64
NORM_EPS = 1e-6
ML_HEADS = 4
ML_HEAD_DIM = 128
ML_WIDTH = ML_HEADS * ML_HEAD_DIM
SWA_HEADS = 8
SWA_KV_HEADS = 2
SWA_GROUP = SWA_HEADS // SWA_KV_HEADS
SWA_HEAD_DIM = 64
SWA_WIDTH = SWA_HEADS * SWA_HEAD_DIM
SWA_KV_WIDTH = SWA_KV_HEADS * SWA_HEAD_DIM
WINDOW = 128
WINDOW_CHUNKS = WINDOW // CHUNK
ROPE_THETA = 500000.0
ROPE_DIM = SWA_HEAD_DIM // 4
MIX_WIDTH = ML_WIDTH + SWA_WIDTH
IN_SPLITS = (ML_WIDTH, 2 * ML_WIDTH, 3 * ML_WIDTH, 4 * ML_WIDTH, 4 * ML_WIDTH + ML_HEADS, 4 * ML_WIDTH + 2 * ML_HEADS, 4 * ML_WIDTH + 2 * ML_HEADS + SWA_WIDTH, 4 * ML_WIDTH + 2 * ML_HEADS + SWA_WIDTH + SWA_KV_WIDTH)
IN_COLS = 4 * ML_WIDTH + 2 * ML_HEADS + SWA_WIDTH + 2 * SWA_KV_WIDTH
PEER_HEADS = 8
N_KEYS = 128
N_EXPERTS = N_KEYS * N_KEYS
PEER_TOPK = 16
PEER_KEY_DIM = 256
PEER_HALF = PEER_KEY_DIM // 2
PEER_BLOCK = 128

kernel_name = "hymba_mlstm_swa_peer_stream_step"


def rmsnorm(x, w):
    xf = x.astype(jnp.float32)
    y = xf * lax.rsqrt(jnp.mean(xf * xf, axis=-1, keepdims=True) + NORM_EPS)
    return (y * w.astype(jnp.float32)).astype(x.dtype)


def head_rmsnorm(h, w):
    B, T, H, Dh = h.shape
    hf = h.astype(jnp.float32)
    y = hf * lax.rsqrt(jnp.mean(hf * hf, axis=-1, keepdims=True) + NORM_EPS)
    return y.reshape(B, T, H * Dh) * w.astype(jnp.float32)


def rope(x, pos):
    half = ROPE_DIM // 2
    inv = ROPE_THETA ** (-jnp.arange(half, dtype=jnp.float32) * 2.0 / ROPE_DIM)
    ang = pos.astype(jnp.float32)[:, None] * inv[None, :]
    cos = jnp.cos(ang)[:, None, :]
    sin = jnp.sin(ang)[:, None, :]
    xf = x.astype(jnp.float32)
    x1 = xf[..., :half]
    x2 = xf[..., half:ROPE_DIM]
    out = jnp.concatenate([x1 * cos - x2 * sin, x2 * cos + x1 * sin, xf[..., ROPE_DIM:]], axis=-1)
    return out.astype(x.dtype)


def mlstm_chunk(carry, inp):
    c, n, m = carry
    q, k, v, ig, lf = inp
    L = q.shape[2]
    b = jnp.cumsum(lf, axis=-1)
    causal = jnp.tril(jnp.ones((L, L), dtype=bool))
    logd = jnp.where(causal, b[..., :, None] - b[..., None, :] + ig[..., None, :], -jnp.inf)
    m_inter = b + m[..., None]
    m_t = jnp.maximum(m_inter, jnp.max(logd, axis=-1))
    dmat = jnp.exp(logd - m_t[..., None])
    w_inter = jnp.exp(m_inter - m_t)
    qk = jnp.einsum('bhtd,bhsd->bhts', q, k) * dmat
    num = w_inter[..., None] * jnp.einsum('bhed,bhtd->bhte', c, q) + jnp.einsum('bhts,bhse->bhte', qk, v)
    den = w_inter * jnp.einsum('bhd,bhtd->bht', n, q) + jnp.sum(qk, axis=-1)
    h = num / jnp.maximum(jnp.abs(den), jnp.exp(-m_t))[..., None]
    b_last = b[..., -1]
    logw = b_last[..., None] - b + ig
    m_new = jnp.maximum(b_last + m, jnp.max(logw, axis=-1))
    w_s = jnp.exp(logw - m_new[..., None])
    decay = jnp.exp(b_last + m - m_new)
    c_new = decay[..., None, None] * c + jnp.einsum('bhs,bhse,bhsd->bhed', w_s, v, k)
    n_new = decay[..., None] * n + jnp.einsum('bhs,bhsd->bhd', w_s, k)
    return (c_new, n_new, m_new), h


def mlstm_scan(q, k, v, ig, lf, c0, n0, m0, chunk_len):
    f32 = jnp.float32
    B, T, H, Dh = q.shape
    nc = T // chunk_len

    def to_chunks(a):
        a = a.astype(f32).reshape((B, nc, chunk_len) + a.shape[2:])
        return jnp.moveaxis(jnp.moveaxis(a, 3, 2), 1, 0)

    carry0 = (c0.astype(f32), n0.astype(f32), m0.astype(f32))
    (c, n, m), h = lax.scan(mlstm_chunk, carry0, (to_chunks(q), to_chunks(k), to_chunks(v), to_chunks(ig), to_chunks(lf)))
    h = jnp.swapaxes(jnp.moveaxis(h, 0, 1), 2, 3).reshape(B, T, H, Dh)
    return h, c, n, m


def sink_softmax(s, sinks):
    sk = sinks.astype(jnp.float32).reshape(SWA_KV_HEADS, SWA_GROUP, 1, 1)
    mx = jnp.maximum(jnp.max(s, axis=-1, keepdims=True), sk)
    p = jnp.exp(s - mx)
    return p / (jnp.sum(p, axis=-1, keepdims=True) + jnp.exp(sk - mx))


def swa_prompt(q, k, v, sinks):
    B, S = q.shape[:2]
    nc = S // CHUNK
    scale = SWA_HEAD_DIM ** -0.5
    qc = q.reshape(B, nc, CHUNK, SWA_KV_HEADS, SWA_GROUP, SWA_HEAD_DIM)
    pad = ((0, 0), (WINDOW, 0), (0, 0), (0, 0))
    kp = jnp.pad(k, pad).reshape(B, nc + WINDOW_CHUNKS, CHUNK, SWA_KV_HEADS, SWA_HEAD_DIM)
    vp = jnp.pad(v, pad).reshape(B, nc + WINDOW_CHUNKS, CHUNK, SWA_KV_HEADS, SWA_HEAD_DIM)
    kw = jnp.concatenate([kp[:, j:j + nc] for j in range(WINDOW_CHUNKS + 1)], axis=2)
    vw = jnp.concatenate([vp[:, j:j + nc] for j in range(WINDOW_CHUNKS + 1)], axis=2)
    key_chunk = jnp.arange(nc)[:, None] - WINDOW_CHUNKS + (jnp.arange(WINDOW + CHUNK) // CHUNK)[None, :]
    valid = (key_chunk >= 0)[None, :, None, None, None, :]
    s = jnp.einsum('bctkgd,bcskd->bckgts', qc, kw).astype(jnp.float32) * scale
    s = jnp.where(valid, s, -jnp.inf)
    p = sink_softmax(s, sinks).astype(v.dtype)
    o = jnp.einsum('bckgts,bcskd->bctkgd', p, vw)
    return o.reshape(B, S, SWA_WIDTH)


def swa_sample(q, k_all, v_all, sinks):
    B, T = q.shape[:2]
    scale = SWA_HEAD_DIM ** -0.5
    qg = q.reshape(B, T, SWA_KV_HEADS, SWA_GROUP, SWA_HEAD_DIM)
    s = jnp.einsum('btkgd,bskd->bkgts', qg, k_all).astype(jnp.float32) * scale
    p = sink_softmax(s, sinks).astype(v_all.dtype)
    o = jnp.einsum('bkgts,bskd->btkgd', p, v_all)
    return o.reshape(B, T, SWA_WIDTH)


def peer(xblocks, w_q, sub_keys_1, sub_keys_2, expert_u, expert_v):
    def block(xb):
        TB = xb.shape[0]
        q = (xb @ w_q).reshape(TB, PEER_HEADS, PEER_KEY_DIM)
        s1 = jnp.einsum('thd,nd->thn', q[..., :PEER_HALF], sub_keys_1)
        s2 = jnp.einsum('thd,nd->thn', q[..., PEER_HALF:], sub_keys_2)
        t1, i1 = lax.top_k(s1, PEER_TOPK)
        t2, i2 = lax.top_k(s2, PEER_TOPK)
        cand = (t1[..., :, None] + t2[..., None, :]).reshape(TB, PEER_HEADS, PEER_TOPK * PEER_TOPK)
        cidx = (i1[..., :, None] * N_KEYS + i2[..., None, :]).reshape(TB, PEER_HEADS, PEER_TOPK * PEER_TOPK)
        top, sel = lax.top_k(cand, PEER_TOPK)
        idx = jnp.take_along_axis(cidx, sel, axis=-1)
        g = jax.nn.softmax(top.astype(jnp.float32), axis=-1).astype(xb.dtype)
        u = jnp.take(expert_u, idx, axis=0)
        a = jax.nn.gelu(jnp.einsum('thkd,td->thk', u, xb), approximate=False)
        ve = jnp.take(expert_v, idx, axis=0)
        return jnp.einsum('thk,thkd->td', g * a, ve)
    return lax.map(block, xblocks)


def layer(x, pos, prompt, k_cache, v_cache, c0, n0, m0,
          norm_mix_w, w_in, mlstm_if_bias, mlstm_norm_w, swa_sinks, w_out,
          norm_ffn_w, peer_w_q, peer_sub_keys_1, peer_sub_keys_2, peer_u, peer_v):
    f32 = jnp.float32
    B, T, _ = x.shape
    h = rmsnorm(x, norm_mix_w)
    proj = h @ w_in
    mq, mk, mv, mo, mi, mf, sq, sk, sv = jnp.split(proj, IN_SPLITS, axis=-1)
    mshape = (B, T, ML_HEADS, ML_HEAD_DIM)
    gate = jnp.concatenate([mi, mf], axis=-1).astype(f32) + mlstm_if_bias.astype(f32)
    ig = gate[..., :ML_HEADS]
    lf = jax.nn.log_sigmoid(gate[..., ML_HEADS:])
    if prompt:
        c0 = jnp.zeros((B, ML_HEADS, ML_HEAD_DIM, ML_HEAD_DIM), f32)
        n0 = jnp.zeros((B, ML_HEADS, ML_HEAD_DIM), f32)
        m0 = jnp.zeros((B, ML_HEADS), f32)
        chunk_len = CHUNK
    else:
        chunk_len = T
    hm, c_new, n_new, m_new = mlstm_scan(mq.reshape(mshape), mk.reshape(mshape) * (ML_HEAD_DIM ** -0.5), mv.reshape(mshape), ig, lf, c0, n0, m0, chunk_len)
    hm = jax.nn.sigmoid(mo) * head_rmsnorm(hm, mlstm_norm_w).astype(x.dtype)
    q = rope(sq.reshape(B, T, SWA_HEADS, SWA_HEAD_DIM), pos)
    k = rope(sk.reshape(B, T, SWA_KV_HEADS, SWA_HEAD_DIM), pos)
    v = sv.reshape(B, T, SWA_KV_HEADS, SWA_HEAD_DIM)
    if prompt:
        hs = swa_prompt(q, k, v, swa_sinks)
        k_win = k[:, -WINDOW:]
        v_win = v[:, -WINDOW:]
    else:
        k_all = jnp.concatenate([k_cache.astype(k.dtype), k], axis=1)
        v_all = jnp.concatenate([v_cache.astype(v.dtype), v], axis=1)
        hs = swa_sample(q, k_all, v_all, swa_sinks)
        k_win = k_all[:, -WINDOW:]
        v_win = v_all[:, -WINDOW:]
    x = x + jnp.concatenate([hm, hs], axis=-1) @ w_out
    h2 = rmsnorm(x, norm_ffn_w)
    blocks = h2.reshape(B * T // PEER_BLOCK, PEER_BLOCK, D_MODEL) if prompt else h2
    x = x + peer(blocks, peer_w_q, peer_sub_keys_1, peer_sub_keys_2, peer_u, peer_v).reshape(B, T, D_MODEL)
    return x, k_win, v_win, c_new, n_new, m_new


def setup_inputs(seed: int = 0) -> dict:
    key = jax.random.key(seed)
    ks = jax.random.split(key, 24)
    f32 = jnp.float32

    def nrm(k, shape, scale):
        return scale * jax.random.normal(k, shape, f32)

    x_prompt = nrm(ks[0], (BATCH, SEQ, D_MODEL), 1.0)
    x_sample = nrm(ks[1], (DEC_BATCH, DEC_SEQ, D_MODEL), 1.0)
    cache_swa_k = nrm(ks[2], (DEPTH, DEC_BATCH, WINDOW, SWA_KV_HEADS, SWA_HEAD_DIM), 1.0)
    cache_swa_v = nrm(ks[3], (DEPTH, DEC_BATCH, WINDOW, SWA_KV_HEADS, SWA_HEAD_DIM), 1.0)
    state_mlstm_c = nrm(ks[4], (DEPTH, DEC_BATCH, ML_HEADS, ML_HEAD_DIM, ML_HEAD_DIM), 0.05)
    state_mlstm_n = nrm(ks[5], (DEPTH, DEC_BATCH, ML_HEADS, ML_HEAD_DIM), 0.1)
    state_mlstm_m = nrm(ks[6], (DEPTH, DEC_BATCH, ML_HEADS), 1.0)
    norm_mix_w = 1.0 + nrm(ks[7], (DEPTH, D_MODEL), 0.02)
    w_in = nrm(ks[8], (DEPTH, D_MODEL, IN_COLS), D_MODEL ** -0.5)
    i_bias = nrm(ks[9], (DEPTH, ML_HEADS), 0.1)
    f_bias = 3.0 + jnp.linspace(0.0, 3.0, ML_HEADS, dtype=f32) + nrm(ks[10], (DEPTH, ML_HEADS), 0.1)
    mlstm_if_bias = jnp.concatenate([i_bias, f_bias], axis=-1)
    mlstm_norm_w = 1.0 + nrm(ks[11], (DEPTH, ML_WIDTH), 0.02)
    swa_sinks = nrm(ks[12], (DEPTH, SWA_HEADS), 0.5)
    w_out = nrm(ks[13], (DEPTH, MIX_WIDTH, D_MODEL), MIX_WIDTH ** -0.5)
    norm_ffn_w = 1.0 + nrm(ks[14], (DEPTH, D_MODEL), 0.02)
    peer_w_q = nrm(ks[15], (DEPTH, D_MODEL, PEER_HEADS * PEER_KEY_DIM), D_MODEL ** -0.5)
    peer_sub_keys_1 = nrm(ks[16], (DEPTH, N_KEYS, PEER_HALF), PEER_HALF ** -0.5)
    peer_sub_keys_2 = nrm(ks[17], (DEPTH, N_KEYS, PEER_HALF), PEER_HALF ** -0.5)
    peer_u = nrm(ks[18], (DEPTH, N_EXPERTS, D_MODEL), D_MODEL ** -0.5)
    peer_v = nrm(ks[19], (DEPTH, N_EXPERTS, D_MODEL), 0.5)
    norm_final_w = 1.0 + nrm(ks[20], (D_MODEL,), 0.02)
    return {"x_prompt": x_prompt, "x_sample": x_sample,
            "cache_swa_k": cache_swa_k, "cache_swa_v": cache_swa_v,
            "state_mlstm_c": state_mlstm_c, "state_mlstm_n": state_mlstm_n, "state_mlstm_m": state_mlstm_m,
            "norm_mix_w": norm_mix_w, "w_in": w_in, "mlstm_if_bias": mlstm_if_bias, "mlstm_norm_w": mlstm_norm_w,
            "swa_sinks": swa_sinks, "w_out": w_out, "norm_ffn_w": norm_ffn_w,
            "peer_w_q": peer_w_q, "peer_sub_keys_1": peer_sub_keys_1, "peer_sub_keys_2": peer_sub_keys_2,
            "peer_u": peer_u, "peer_v": peer_v, "norm_final_w": norm_final_w}


def reference(x_prompt, x_sample, cache_swa_k, cache_swa_v, state_mlstm_c, state_mlstm_n, state_mlstm_m,
              norm_mix_w, w_in, mlstm_if_bias, mlstm_norm_w, swa_sinks, w_out, norm_ffn_w,
              peer_w_q, peer_sub_keys_1, peer_sub_keys_2, peer_u, peer_v, norm_final_w):
    pos_p = jnp.arange(x_prompt.shape[1], dtype=jnp.int32)
    pos_s = PAST_LEN + jnp.arange(x_sample.shape[1], dtype=jnp.int32)
    xp = x_prompt
    xs = x_sample
    kp_l, vp_l, cp_l, np_l, mp_l = [], [], [], [], []
    ks_l, vs_l, cs_l, ns_l, ms_l = [], [], [], [], []
    for l in range(DEPTH):
        w = (norm_mix_w[l], w_in[l], mlstm_if_bias[l], mlstm_norm_w[l], swa_sinks[l], w_out[l],
             norm_ffn_w[l], peer_w_q[l], peer_sub_keys_1[l], peer_sub_keys_2[l], peer_u[l], peer_v[l])
        xp, kp, vp, cp, npv, mp = layer(xp, pos_p, True, None, None, None, None, None, *w)
        xs, ksn, vsn, csn, nsn, msn = layer(xs, pos_s, False, cache_swa_k[l], cache_swa_v[l],
                                            state_mlstm_c[l], state_mlstm_n[l], state_mlstm_m[l], *w)
        kp_l.append(kp); vp_l.append(vp); cp_l.append(cp); np_l.append(npv); mp_l.append(mp)
        ks_l.append(ksn); vs_l.append(vsn); cs_l.append(csn); ns_l.append(nsn); ms_l.append(msn)
    y_prompt = rmsnorm(xp, norm_final_w)
    y_sample = rmsnorm(xs, norm_final_w)
    new_swa_k_prompt = jnp.stack(kp_l, axis=0)
    new_swa_v_prompt = jnp.stack(vp_l, axis=0)
    new_mlstm_c_prompt = jnp.stack(cp_l, axis=0)
    new_mlstm_n_prompt = jnp.stack(np_l, axis=0)
    new_mlstm_m_prompt = jnp.stack(mp_l, axis=0)
    new_swa_k_sample = jnp.stack(ks_l, axis=0)
    new_swa_v_sample = jnp.stack(vs_l, axis=0)
    new_mlstm_c_sample = jnp.stack(cs_l, axis=0)
    new_mlstm_n_sample = jnp.stack(ns_l, axis=0)
    new_mlstm_m_sample = jnp.stack(ms_l, axis=0)
    return (y_prompt, y_sample,
            new_swa_k_prompt, new_swa_v_prompt, new_mlstm_c_prompt, new_mlstm_n_prompt, new_mlstm_m_prompt,
            new_swa_k_sample, new_swa_v_sample, new_mlstm_c_sample, new_mlstm_n_sample, new_mlstm_m_sample)
```

```python
import functools

import jax
import jax.numpy as jnp
import numpy as np
from jax import lax
from jax.experimental import pallas as pl
from jax.experimental.pallas import tpu as pltpu

F32 = jnp.float32
BF16 = jnp.bfloat16

D_MODEL = 1024
SEQ = 8192
DEC_SEQ = 32
PAST_LEN = 4096
NORM_EPS = 1e-6
ML_HEADS = 4
ML_HEAD_DIM = 128
ML_WIDTH = ML_HEADS * ML_HEAD_DIM
SWA_HEADS = 8
SWA_KV_HEADS = 2
SWA_GROUP = SWA_HEADS // SWA_KV_HEADS
SWA_HEAD_DIM = 64
SWA_WIDTH = SWA_HEADS * SWA_HEAD_DIM
SWA_KV_WIDTH = SWA_KV_HEADS * SWA_HEAD_DIM
WINDOW = 128
SWA_CHUNK = 64
ROPE_THETA = 500000.0
ROPE_DIM = SWA_HEAD_DIM // 4
ROPE_HALF = ROPE_DIM // 2
PEER_HEADS = 8
N_KEYS = 128
N_EXPERTS = N_KEYS * N_KEYS
PEER_TOPK = 16
PEER_KEY_DIM = 256
PEER_HALF = PEER_KEY_DIM // 2

LANES = 128
VMEM_LIMIT = 52 * 1024 * 1024

COL_MQ, COL_MK, COL_MV, COL_MO = 0, ML_WIDTH, 2 * ML_WIDTH, 3 * ML_WIDTH
COL_SQ = 4 * ML_WIDTH
COL_SK = COL_SQ + SWA_WIDTH
COL_SV = COL_SK + SWA_KV_WIDTH
COL_G = COL_SV + SWA_KV_WIDTH
IN_COLS_PAD = COL_G + LANES

TM_INPROJ = 256
ML_CHUNK = 128
SWA_TQ = 256
TM_OUTPROJ = 256
TM_ROUTE = 128
TM_PEER = 256
PEER_SUB = 512
PEER_NSUB = 4

NT_DIMS = (((1,), (1,)), ((), ()))
TN_DIMS = (((0,), (0,)), ((), ()))


def _rms(x, w):
    return x * lax.rsqrt(jnp.mean(x * x, axis=-1, keepdims=True) + NORM_EPS) * w


def _inproj_kernel(x_ref, nw_ref, w_ref, bias_ref, cos_ref, sina_ref, sinb_ref,
                   qm_ref, km_ref, vm_ref, og_ref, sq_ref, sk_ref, sv_ref, gc_ref):
    h = _rms(x_ref[...], nw_ref[...])
    proj = jnp.dot(h.astype(BF16), w_ref[...], preferred_element_type=F32)
    qm_ref[...] = proj[:, COL_MQ:COL_MQ + ML_WIDTH]
    km_ref[...] = proj[:, COL_MK:COL_MK + ML_WIDTH] * (ML_HEAD_DIM ** -0.5)
    vm_ref[...] = proj[:, COL_MV:COL_MV + ML_WIDTH]
    og_ref[...] = jax.nn.sigmoid(proj[:, COL_MO:COL_MO + ML_WIDTH])
    cosf, sina, sinb = cos_ref[...], sina_ref[...], sinb_ref[...]

    def rope(xc):
        return (xc * cosf + pltpu.roll(xc, LANES - ROPE_HALF, 1) * sina
                + pltpu.roll(xc, ROPE_HALF, 1) * sinb)

    for j in range(SWA_WIDTH // LANES):
        sq_ref[:, j * LANES:(j + 1) * LANES] = rope(proj[:, COL_SQ + j * LANES:COL_SQ + (j + 1) * LANES])
    sk_ref[...] = rope(proj[:, COL_SK:COL_SK + LANES])
    sv_ref[...] = proj[:, COL_SV:COL_SV + LANES]
    g = proj[:, COL_G:COL_G + LANES] + bias_ref[...]
    lane = lax.broadcasted_iota(jnp.int32, g.shape, 1)
    gc_ref[...] = jnp.where(lane < ML_HEADS, g, jax.nn.log_sigmoid(g))


def _inproj(x2d, nw, w_perm, bias_pad, cos_t, sina_t, sinb_t, tab_map):
    T = x2d.shape[0]
    TM = TM_INPROJ
    row = lambda i: (i, 0)
    const = lambda i: (0, 0)
    f = lambda n: jax.ShapeDtypeStruct((T, n), F32)
    return pl.pallas_call(
        _inproj_kernel,
        grid=(T // TM,),
        in_specs=[pl.BlockSpec((TM, D_MODEL), row),
                  pl.BlockSpec((1, D_MODEL), const),
                  pl.BlockSpec((D_MODEL, IN_COLS_PAD), const),
                  pl.BlockSpec((1, LANES), const),
                  pl.BlockSpec((TM, LANES), tab_map),
                  pl.BlockSpec((TM, LANES), tab_map),
                  pl.BlockSpec((TM, LANES), tab_map)],
        out_specs=[pl.BlockSpec((TM, ML_WIDTH), row)] * 4
                  + [pl.BlockSpec((TM, SWA_WIDTH), row),
                     pl.BlockSpec((TM, LANES), row), pl.BlockSpec((TM, LANES), row),
                     pl.BlockSpec((TM, LANES), row)],
        out_shape=[f(ML_WIDTH)] * 4 + [f(SWA_WIDTH), f(LANES), f(LANES), f(LANES)],
        compiler_params=pltpu.CompilerParams(dimension_semantics=("arbitrary",),
                                             vmem_limit_bytes=VMEM_LIMIT),
        name="inproj",
    )(x2d, nw, w_perm, bias_pad, cos_t, sina_t, sinb_t)


def _mlstm_kernel(q_ref, k_ref, v_ref, og_ref, gc_ref, c0_ref, n0_ref, m0_ref, nw_ref,
                  hm_ref, c_ref, n_ref, m_ref, *, BB, L_in):
    LP = ML_CHUNK

    @pl.when(pl.program_id(1) == 0)
    def _():
        c_ref[...] = c0_ref[...]
        n_ref[...] = n0_ref[...]
        m_ref[...] = m0_ref[...]

    row = lax.broadcasted_iota(jnp.int32, (LP, LP), 0)
    col = lax.broadcasted_iota(jnp.int32, (LP, LP), 1)
    causal = row >= col
    tri = causal.astype(F32)
    tri_t = (row <= col).astype(F32)
    lane = lax.broadcasted_iota(jnp.int32, (LP, LANES), 1)
    lane1 = lax.broadcasted_iota(jnp.int32, (1, LANES), 1)

    def pad_rows(a):
        if L_in == LP:
            return a
        return jnp.concatenate([a, jnp.zeros((LP - L_in, a.shape[1]), a.dtype)], axis=0)

    for b in range(BB):
        gc = gc_ref[b]
        if L_in < LP:
            lane_pad = lax.broadcasted_iota(jnp.int32, (LP - L_in, LANES), 1)
            fill = jnp.where(lane_pad < ML_HEADS, -jnp.inf, 0.0).astype(F32)
            gc = jnp.concatenate([gc, fill], axis=0)
        gr = gc.T
        lf_c = jnp.where(lane >= ML_HEADS, gc, 0.0)
        b_c = jnp.dot(tri, lf_c, precision=lax.Precision.HIGHEST, preferred_element_type=F32)
        rowi = lax.broadcasted_iota(jnp.int32, (8, LP), 0)
        lf_r = jnp.where(rowi >= ML_HEADS, gr[:8], 0.0)
        b_r = jnp.dot(lf_r, tri_t, precision=lax.Precision.HIGHEST, preferred_element_type=F32)
        m_all = m_ref[b]
        m_out = m_all
        q_all = pad_rows(q_ref[b])
        k_all = pad_rows(k_ref[b])
        v_all = pad_rows(v_ref[b])
        og_all = og_ref[b]
        for h in range(ML_HEADS):
            sl = slice(h * ML_HEAD_DIM, (h + 1) * ML_HEAD_DIM)
            q, k, v = q_all[:, sl], k_all[:, sl], v_all[:, sl]
            qb, kb = q.astype(BF16), k.astype(BF16)
            ig_c = gc[:, h:h + 1]
            b_ch = b_c[:, ML_HEADS + h:ML_HEADS + h + 1]
            ig_r = gr[h:h + 1, :]
            b_rh = b_r[ML_HEADS + h:ML_HEADS + h + 1, :]
            m_prev = m_all[:, h:h + 1]
            logd = jnp.where(causal, b_ch - b_rh + ig_r, -jnp.inf)
            m_inter = b_ch + m_prev
            m_t = jnp.maximum(m_inter, jnp.max(logd, axis=1, keepdims=True))
            dmat = jnp.exp(logd - m_t)
            w_int = jnp.exp(m_inter - m_t)
            s = lax.dot_general(qb, kb, NT_DIMS, preferred_element_type=F32)
            qk = s * dmat
            c_old = c_ref[b, h]
            n_old = n_ref[b, h:h + 1, :]
            num = (w_int * lax.dot_general(qb, c_old.astype(BF16), NT_DIMS, preferred_element_type=F32)
                   + jnp.dot(qk.astype(BF16), v.astype(BF16), preferred_element_type=F32))
            den = (w_int * jnp.sum(q * n_old, axis=1, keepdims=True)
                   + jnp.sum(qk, axis=1, keepdims=True))
            hh = num / jnp.maximum(jnp.abs(den), jnp.exp(-m_t))
            b_last = b_ch[LP - 1:LP, :]
            logw = b_last - b_ch + ig_c
            m_new = jnp.maximum(b_last + m_prev, jnp.max(logw, axis=0, keepdims=True))
            w_c = jnp.exp(logw - m_new)
            decay = jnp.exp(b_last + m_prev - m_new)
            c_ref[b, h] = decay * c_old + lax.dot_general(
                (v * w_c).astype(BF16), kb, TN_DIMS, preferred_element_type=F32)
            n_ref[b, h:h + 1, :] = decay * n_old + jnp.sum(w_c * k, axis=0, keepdims=True)
            m_out = jnp.where(lane1 == h, m_new, m_out)
            y = _rms(hh, nw_ref[:, sl])
            hm_ref[b, :, sl] = (og_all[:, sl] * y[:L_in]).astype(BF16)
        m_ref[b] = m_out


def _mlstm(qm, km, vm, og, gc, c0, n0, m0, nw, *, L_in):
    B, T, _ = qm.shape
    BB = 2
    nchunks = T // L_in
    tok = lambda n: pl.BlockSpec((BB, L_in, n), lambda g, c: (g, c, 0))
    st_c = pl.BlockSpec((BB, ML_HEADS, ML_HEAD_DIM, ML_HEAD_DIM), lambda g, c: (g, 0, 0, 0))
    st_n = pl.BlockSpec((BB, ML_HEADS, ML_HEAD_DIM), lambda g, c: (g, 0, 0))
    st_m = pl.BlockSpec((BB, 1, LANES), lambda g, c: (g, 0, 0))
    return pl.pallas_call(
        functools.partial(_mlstm_kernel, BB=BB, L_in=L_in),
        grid=(B // BB, nchunks),
        in_specs=[tok(ML_WIDTH), tok(ML_WIDTH), tok(ML_WIDTH), tok(ML_WIDTH), tok(LANES),
                  st_c, st_n, st_m, pl.BlockSpec((1, ML_WIDTH), lambda g, c: (0, 0))],
        out_specs=[tok(ML_WIDTH), st_c, st_n, st_m],
        out_shape=[jax.ShapeDtypeStruct((B, T, ML_WIDTH), BF16),
                   jax.ShapeDtypeStruct(c0.shape, F32),
                   jax.ShapeDtypeStruct(n0.shape, F32),
                   jax.ShapeDtypeStruct(m0.shape, F32)],
        compiler_params=pltpu.CompilerParams(dimension_semantics=("arbitrary", "arbitrary"),
                                             vmem_limit_bytes=VMEM_LIMIT),
        name="mlstm",
    )(qm, km, vm, og, gc, c0, n0, m0, nw)


def _sink_attention(q4, kw, vw, sink_col, valid):
    s = lax.dot_general(q4, kw, NT_DIMS, preferred_element_type=F32) * (SWA_HEAD_DIM ** -0.5)
    if valid is not None:
        s = jnp.where(valid, s, -jnp.inf)
    mx = jnp.maximum(jnp.max(s, axis=1, keepdims=True), sink_col)
    p = jnp.exp(s - mx)
    den = jnp.sum(p, axis=1, keepdims=True) + jnp.exp(sink_col - mx)
    return jnp.dot(p.astype(BF16), vw, preferred_element_type=F32) / den


def _sink_column(sink_ref, g, rows):
    return jnp.concatenate(
        [jnp.broadcast_to(sink_ref[:, g * SWA_GROUP + j:g * SWA_GROUP + j + 1], (rows, 1))
         for j in range(SWA_GROUP)], axis=0)


def _swa_prompt_kernel(q_ref, kc_ref, kp_ref, vc_ref, vp_ref, sink_ref, o_ref):
    TQ = SWA_TQ
    i = pl.program_id(1)
    q = q_ref[0]
    kcat = jnp.concatenate([kp_ref[0, TQ - WINDOW:, :], kc_ref[0]], axis=0).astype(BF16)
    vcat = jnp.concatenate([vp_ref[0, TQ - WINDOW:, :], vc_ref[0]], axis=0).astype(BF16)
    span = WINDOW + SWA_CHUNK
    key_iota = lax.broadcasted_iota(jnp.int32, (1, span), 1)
    for g in range(SWA_KV_HEADS):
        gs = slice(g * SWA_HEAD_DIM, (g + 1) * SWA_HEAD_DIM)
        kg, vg = kcat[:, gs], vcat[:, gs]
        sink_col = _sink_column(sink_ref, g, SWA_CHUNK)
        for c in range(TQ // SWA_CHUNK):
            rs = slice(c * SWA_CHUNK, (c + 1) * SWA_CHUNK)
            q4 = jnp.concatenate(
                [q[rs, (g * SWA_GROUP + j) * SWA_HEAD_DIM:(g * SWA_GROUP + j + 1) * SWA_HEAD_DIM]
                 for j in range(SWA_GROUP)], axis=0).astype(BF16)
            ws = slice(c * SWA_CHUNK, c * SWA_CHUNK + span)
            valid = (i * TQ + c * SWA_CHUNK - WINDOW + key_iota) >= 0
            o = _sink_attention(q4, kg[ws], vg[ws], sink_col, valid)
            for j in range(SWA_GROUP):
                hd = (g * SWA_GROUP + j) * SWA_HEAD_DIM
                o_ref[0, rs, hd:hd + SWA_HEAD_DIM] = o[j * SWA_CHUNK:(j + 1) * SWA_CHUNK].astype(BF16)


def _swa_prompt(sq, sk, sv, sinks):
    B, T, _ = sq.shape
    TQ = SWA_TQ
    cur = lambda b, i: (b, i, 0)
    prev = lambda b, i: (b, jnp.maximum(i - 1, 0), 0)
    return pl.pallas_call(
        _swa_prompt_kernel,
        grid=(B, T // TQ),
        in_specs=[pl.BlockSpec((1, TQ, SWA_WIDTH), cur),
                  pl.BlockSpec((1, TQ, SWA_KV_WIDTH), cur), pl.BlockSpec((1, TQ, SWA_KV_WIDTH), prev),
                  pl.BlockSpec((1, TQ, SWA_KV_WIDTH), cur), pl.BlockSpec((1, TQ, SWA_KV_WIDTH), prev),
                  pl.BlockSpec((1, SWA_HEADS), lambda b, i: (0, 0))],
        out_specs=pl.BlockSpec((1, TQ, SWA_WIDTH), cur),
        out_shape=jax.ShapeDtypeStruct((B, T, SWA_WIDTH), BF16),
        compiler_params=pltpu.CompilerParams(dimension_semantics=("arbitrary", "arbitrary"),
                                             vmem_limit_bytes=VMEM_LIMIT),
        name="swa_prompt",
    )(sq, sk, sk, sv, sv, sinks)


def _swa_sample_kernel(q_ref, kn_ref, vn_ref, kc_ref, vc_ref, sink_ref, o_ref, kw_ref, vw_ref):
    T = DEC_SEQ
    q = q_ref[0]
    k_all = jnp.concatenate([kc_ref[0], kn_ref[0]], axis=0)
    v_all = jnp.concatenate([vc_ref[0], vn_ref[0]], axis=0)
    kw_ref[0] = k_all[T:]
    vw_ref[0] = v_all[T:]
    kb, vb = k_all.astype(BF16), v_all.astype(BF16)
    for g in range(SWA_KV_HEADS):
        gs = slice(g * SWA_HEAD_DIM, (g + 1) * SWA_HEAD_DIM)
        q4 = jnp.concatenate(
            [q[:, (g * SWA_GROUP + j) * SWA_HEAD_DIM:(g * SWA_GROUP + j + 1) * SWA_HEAD_DIM]
             for j in range(SWA_GROUP)], axis=0).astype(BF16)
        o = _sink_attention(q4, kb[:, gs], vb[:, gs], _sink_column(sink_ref, g, T), None)
        for j in range(SWA_GROUP):
            hd = (g * SWA_GROUP + j) * SWA_HEAD_DIM
            o_ref[0, :, hd:hd + SWA_HEAD_DIM] = o[j * T:(j + 1) * T].astype(BF16)


def _swa_sample(sq, sk, sv, k_cache, v_cache, sinks):
    B, T, _ = sq.shape
    b3 = lambda b: (b, 0, 0)
    return pl.pallas_call(
        _swa_sample_kernel,
        grid=(B,),
        in_specs=[pl.BlockSpec((1, T, SWA_WIDTH), b3),
                  pl.BlockSpec((1, T, SWA_KV_WIDTH), b3), pl.BlockSpec((1, T, SWA_KV_WIDTH), b3),
                  pl.BlockSpec((1, WINDOW, SWA_KV_WIDTH), b3), pl.BlockSpec((1, WINDOW, SWA_KV_WIDTH), b3),
                  pl.BlockSpec((1, SWA_HEADS), lambda b: (0, 0))],
        out_specs=[pl.BlockSpec((1, T, SWA_WIDTH), b3),
                   pl.BlockSpec((1, WINDOW, SWA_KV_WIDTH), b3), pl.BlockSpec((1, WINDOW, SWA_KV_WIDTH), b3)],
        out_shape=[jax.ShapeDtypeStruct((B, T, SWA_WIDTH), BF16),
                   jax.ShapeDtypeStruct((B, WINDOW, SWA_KV_WIDTH), F32),
                   jax.ShapeDtypeStruct((B, WINDOW, SWA_KV_WIDTH), F32)],
        compiler_params=pltpu.CompilerParams(dimension_semantics=("arbitrary",),
                                             vmem_limit_bytes=VMEM_LIMIT),
        name="swa_sample",
    )(sq, sk, sv, k_cache, v_cache, sinks)


def _outproj_kernel(hm_ref, hs_ref, x_ref, wo_ref, nw_ref, x1_ref, h2_ref):
    mix = (jnp.dot(hm_ref[...], wo_ref[:ML_WIDTH, :], preferred_element_type=F32)
           + jnp.dot(hs_ref[...], wo_ref[ML_WIDTH:, :], preferred_element_type=F32))
    x1 = x_ref[...] + mix
    x1_ref[...] = x1
    h2_ref[...] = _rms(x1, nw_ref[...]).astype(BF16)


def _outproj(hm, hs, x2d, wo, nw):
    T = x2d.shape[0]
    TM = TM_OUTPROJ
    row = lambda i: (i, 0)
    const = lambda i: (0, 0)
    return pl.pallas_call(
        _outproj_kernel,
        grid=(T // TM,),
        in_specs=[pl.BlockSpec((TM, ML_WIDTH), row), pl.BlockSpec((TM, SWA_WIDTH), row),
                  pl.BlockSpec((TM, D_MODEL), row),
                  pl.BlockSpec((D_MODEL, D_MODEL), const), pl.BlockSpec((1, D_MODEL), const)],
        out_specs=[pl.BlockSpec((TM, D_MODEL), row), pl.BlockSpec((TM, D_MODEL), row)],
        out_shape=[jax.ShapeDtypeStruct((T, D_MODEL), F32), jax.ShapeDtypeStruct((T, D_MODEL), BF16)],
        compiler_params=pltpu.CompilerParams(dimension_semantics=("arbitrary",),
                                             vmem_limit_bytes=VMEM_LIMIT),
        name="outproj",
    )(hm, hs, x2d, wo, nw)


_CAND_NB = [PEER_TOPK // (a + 1) for a in range(PEER_TOPK)]
_CAND_ROWS = 16 + 8 * 7 + 8


def _extract_top16(S):
    R = S.shape[0]
    iota = lax.broadcasted_iota(jnp.int32, S.shape, 0)
    rank = jnp.full(S.shape, float(PEER_TOPK), F32)
    vals = []
    for r in range(PEER_TOPK):
        mx = jnp.max(S, axis=0, keepdims=True)
        idx = jnp.min(jnp.where(S == mx, iota, R), axis=0, keepdims=True)
        hit = iota == idx
        rank = jnp.where(hit, float(r), rank)
        S = jnp.where(hit, -jnp.inf, S)
        vals.append(mx)
    return vals, rank


def _route_kernel(h2_ref, wq_ref, sk1_ref, sk2_ref, r2_ref, e2_ref, lim_ref, e1_ref):
    N = h2_ref.shape[0]
    q = jnp.dot(h2_ref[...], wq_ref[...], preferred_element_type=F32).astype(BF16)
    row8 = lax.broadcasted_iota(jnp.int32, (8, N), 0)
    rowc = lax.broadcasted_iota(jnp.int32, (_CAND_ROWS, N), 0)
    mid = rowc - 16
    flat = jnp.where(rowc < 16, rowc,
                     jnp.where(rowc < _CAND_ROWS - 8,
                               PEER_TOPK * ((mid >> 3) + 1) + (mid & 7),
                               PEER_TOPK * (rowc - (_CAND_ROWS - 16))))
    for h in range(PEER_HEADS):
        c0 = h * PEER_KEY_DIM
        s1 = lax.dot_general(sk1_ref[...], q[:, c0:c0 + PEER_HALF], NT_DIMS, preferred_element_type=F32)
        s2 = lax.dot_general(sk2_ref[...], q[:, c0 + PEER_HALF:c0 + PEER_KEY_DIM], NT_DIMS,
                             preferred_element_type=F32)
        t1, r1 = _extract_top16(s1)
        t2, r2 = _extract_top16(s2)
        t2_16 = jnp.concatenate(t2, axis=0)
        t2_8 = t2_16[:8]
        blocks = [t1[0] + t2_16, t1[1] + t2_8]
        for a in range(2, 8):
            blocks.append(jnp.where(row8 < _CAND_NB[a], t1[a] + t2_8, -jnp.inf))
        blocks.append(jnp.concatenate(t1[8:], axis=0) + t2[0])
        cand = jnp.concatenate(blocks, axis=0)
        sel = jnp.zeros(cand.shape, F32)
        work = cand
        for _ in range(PEER_TOPK):
            mx = jnp.max(work, axis=0, keepdims=True)
            idx = jnp.min(jnp.where(work == mx, flat, PEER_TOPK * PEER_TOPK), axis=0, keepdims=True)
            hit = flat == idx
            sel = jnp.where(hit, 1.0, sel)
            work = jnp.where(hit, -jnp.inf, work)
        z = jnp.sum(jnp.where(sel > 0.0, jnp.exp(cand - cand[0:1]), 0.0), axis=0, keepdims=True)
        counts = [jnp.sum(sel[0:16], axis=0, keepdims=True)]
        for a in range(1, 8):
            counts.append(jnp.sum(sel[8 + 8 * a:16 + 8 * a], axis=0, keepdims=True))
        for a in range(8, PEER_TOPK):
            counts.append(sel[_CAND_ROWS - 16 + a:_CAND_ROWS - 15 + a])
        lim = jnp.zeros(r1.shape, F32)
        for a in range(PEER_TOPK):
            lim = jnp.where(r1 == float(a), counts[a], lim)
        r2_ref[h] = r2
        e2_ref[h] = jnp.exp(s2 - t2[0])
        lim_ref[h] = lim
        e1_ref[h] = jnp.exp(s1 - t1[0]) / z


def _route(h2, wq, sk1, sk2):
    T = h2.shape[0]
    TM = TM_ROUTE
    const = lambda i: (0, 0)
    tab = pl.BlockSpec((PEER_HEADS, N_KEYS, TM), lambda i: (0, 0, i))
    tab_shape = jax.ShapeDtypeStruct((PEER_HEADS, N_KEYS, T), F32)
    return pl.pallas_call(
        _route_kernel,
        grid=(T // TM,),
        in_specs=[pl.BlockSpec((TM, D_MODEL), lambda i: (i, 0)),
                  pl.BlockSpec((D_MODEL, PEER_HEADS * PEER_KEY_DIM), const),
                  pl.BlockSpec((N_KEYS, PEER_HALF), const), pl.BlockSpec((N_KEYS, PEER_HALF), const)],
        out_specs=[tab] * 4,
        out_shape=[tab_shape] * 4,
        compiler_params=pltpu.CompilerParams(dimension_semantics=("arbitrary",),
                                             vmem_limit_bytes=VMEM_LIMIT),
        name="route",
    )(h2, wq, sk1, sk2)


def _peer_kernel(h2_ref, x1_ref, u_ref, vt_ref, r2_ref, e2_ref, lim_ref, e1_ref, nfw_ref,
                 y_ref, acc_ref):
    e = pl.program_id(1)
    per_sub = PEER_SUB // N_KEYS

    @pl.when(e == 0)
    def _():
        acc_ref[...] = jnp.zeros(acc_ref.shape, F32)

    h2 = h2_ref[...]

    def sub_step(s, carry):
        act = lax.dot_general(u_ref[s], h2, NT_DIMS, preferred_element_type=F32)
        gact = 0.5 * act * (1.0 + lax.erf(act * np.float32(np.sqrt(0.5))))
        parts = []
        for jj in range(per_sub):
            j = (e * PEER_NSUB + s) * per_sub + jj
            g = jnp.zeros((N_KEYS, h2.shape[0]), F32)
            for h in range(PEER_HEADS):
                lim = lim_ref[h, pl.ds(j, 1), :]
                e1 = e1_ref[h, pl.ds(j, 1), :]
                g = g + jnp.where(r2_ref[h] < lim, e2_ref[h], 0.0) * e1
            parts.append(g)
        p = (jnp.concatenate(parts, axis=0) * gact).astype(BF16)
        acc_ref[...] += jnp.dot(vt_ref[s], p, preferred_element_type=F32)
        return carry

    lax.fori_loop(0, PEER_NSUB, sub_step, 0)

    @pl.when(e == pl.num_programs(1) - 1)
    def _():
        x = x1_ref[...] + acc_ref[...].T
        y_ref[...] = _rms(x, nfw_ref[...])


def _peer(h2, x1, u3, vt3, r2, e2, lim, e1, nfw):
    T = h2.shape[0]
    TM = TM_PEER
    n_e = N_EXPERTS // (PEER_SUB * PEER_NSUB)
    tok = lambda i, e: (i, 0)
    tab = pl.BlockSpec((PEER_HEADS, N_KEYS, TM), lambda i, e: (0, 0, i))
    return pl.pallas_call(
        _peer_kernel,
        grid=(T // TM, n_e),
        in_specs=[pl.BlockSpec((TM, D_MODEL), tok), pl.BlockSpec((TM, D_MODEL), tok),
                  pl.BlockSpec((PEER_NSUB, PEER_SUB, D_MODEL), lambda i, e: (e, 0, 0)),
                  pl.BlockSpec((PEER_NSUB, D_MODEL, PEER_SUB), lambda i, e: (e, 0, 0)),
                  tab, tab, tab, tab,
                  pl.BlockSpec((1, D_MODEL), lambda i, e: (0, 0))],
        out_specs=pl.BlockSpec((TM, D_MODEL), tok),
        out_shape=jax.ShapeDtypeStruct((T, D_MODEL), F32),
        scratch_shapes=[pltpu.VMEM((D_MODEL, TM), F32)],
        compiler_params=pltpu.CompilerParams(dimension_semantics=("arbitrary", "arbitrary"),
                                             vmem_limit_bytes=VMEM_LIMIT),
        name="peer",
    )(h2, x1, u3, vt3, r2, e2, lim, e1, nfw)


def _rope_tables(pos):
    inv = ROPE_THETA ** (-jnp.arange(ROPE_HALF, dtype=F32) * 2.0 / ROPE_DIM)
    ang = pos.astype(F32)[:, None] * inv[None, :]
    cos, sin = jnp.cos(ang), jnp.sin(ang)
    n = pos.shape[0]
    rest = SWA_HEAD_DIM - ROPE_DIM
    zh = jnp.zeros((n, ROPE_HALF), F32)
    cos_h = jnp.concatenate([cos, cos, jnp.ones((n, rest), F32)], axis=1)
    sina_h = jnp.concatenate([-sin, zh, jnp.zeros((n, rest), F32)], axis=1)
    sinb_h = jnp.concatenate([zh, sin, jnp.zeros((n, rest), F32)], axis=1)
    rep = LANES // SWA_HEAD_DIM
    return tuple(jnp.tile(t, (1, rep)) for t in (cos_h, sina_h, sinb_h))


def _layer_tokens(x2d, tables, tab_map, W):
    return _inproj(x2d, W["norm_mix"], W["w_in"], W["bias"], *tables, tab_map)


def _ffn(hm, hs, x2d, W):
    x1, h2 = _outproj(hm, hs, x2d, W["w_out"], W["norm_ffn"])
    r2, e2, lim, e1 = _route(h2, W["w_q"], W["sk1"], W["sk2"])
    return _peer(h2, x1, W["u3"], W["vt3"], r2, e2, lim, e1, W["norm_final"])


def kernel(x_prompt, x_sample, cache_swa_k, cache_swa_v, state_mlstm_c, state_mlstm_n, state_mlstm_m,
           norm_mix_w, w_in, mlstm_if_bias, mlstm_norm_w, swa_sinks, w_out, norm_ffn_w,
           peer_w_q, peer_sub_keys_1, peer_sub_keys_2, peer_u, peer_v, norm_final_w):
    B, S, _ = x_prompt.shape
    DB, DS, _ = x_sample.shape
    l = 0
    wi = w_in[l]
    s_q = 4 * ML_WIDTH + 2 * ML_HEADS
    w_perm = jnp.concatenate(
        [wi[:, :4 * ML_WIDTH], wi[:, s_q:], wi[:, 4 * ML_WIDTH:s_q],
         jnp.zeros((D_MODEL, LANES - 2 * ML_HEADS), F32)], axis=1).astype(BF16)
    bias_pad = jnp.concatenate([mlstm_if_bias[l], jnp.zeros((LANES - 2 * ML_HEADS,), F32)])[None, :]
    n_sub = N_EXPERTS // PEER_SUB
    W = {
        "norm_mix": norm_mix_w[l][None, :],
        "w_in": w_perm,
        "bias": bias_pad,
        "w_out": w_out[l].astype(BF16),
        "norm_ffn": norm_ffn_w[l][None, :],
        "w_q": peer_w_q[l].astype(BF16),
        "sk1": peer_sub_keys_1[l].astype(BF16),
        "sk2": peer_sub_keys_2[l].astype(BF16),
        "u3": peer_u[l].astype(BF16).reshape(n_sub, PEER_SUB, D_MODEL),
        "vt3": jnp.swapaxes(peer_v[l].astype(BF16).reshape(n_sub, PEER_SUB, D_MODEL), 1, 2),
        "norm_final": norm_final_w[None, :],
    }
    ml_nw = mlstm_norm_w[l][None, :]
    sinks = swa_sinks[l][None, :]

    xp = x_prompt.reshape(B * S, D_MODEL)
    tiles_per_seq = S // TM_INPROJ
    tabs_p = _rope_tables(jnp.arange(S, dtype=jnp.int32))
    qm, km, vm, og, sq, sk, sv, gc = _layer_tokens(xp, tabs_p, lambda i: (i % tiles_per_seq, 0), W)
    r3 = lambda a: a.reshape(B, S, a.shape[-1])
    zc = jnp.zeros((B, ML_HEADS, ML_HEAD_DIM, ML_HEAD_DIM), F32)
    zn = jnp.zeros((B, ML_HEADS, ML_HEAD_DIM), F32)
    zm = jnp.zeros((B, 1, LANES), F32)
    hm_p, c_p, n_p, m_p = _mlstm(r3(qm), r3(km), r3(vm), r3(og), r3(gc), zc, zn, zm, ml_nw, L_in=ML_CHUNK)
    sk3, sv3 = r3(sk), r3(sv)
    hs_p = _swa_prompt(r3(sq), sk3, sv3, sinks)
    y_p = _ffn(hm_p.reshape(B * S, ML_WIDTH), hs_p.reshape(B * S, SWA_WIDTH), xp, W)
    kv_shape = (1, B, WINDOW, SWA_KV_HEADS, SWA_HEAD_DIM)
    k_win_p = sk3[:, S - WINDOW:].reshape(kv_shape)
    v_win_p = sv3[:, S - WINDOW:].reshape(kv_shape)

    xs = x_sample.reshape(DB * DS, D_MODEL)
    pos_s = PAST_LEN + jnp.arange(DS, dtype=jnp.int32)
    tabs_s = tuple(jnp.tile(t, (DB, 1)) for t in _rope_tables(pos_s))
    qm, km, vm, og, sq, sk, sv, gc = _layer_tokens(xs, tabs_s, lambda i: (i, 0), W)
    r3s = lambda a: a.reshape(DB, DS, a.shape[-1])
    m0 = jnp.concatenate([state_mlstm_m[l], jnp.zeros((DB, LANES - ML_HEADS), F32)], axis=1)[:, None, :]
    hm_s, c_s, n_s, m_s = _mlstm(r3s(qm), r3s(km), r3s(vm), r3s(og), r3s(gc),
                                 state_mlstm_c[l], state_mlstm_n[l], m0, ml_nw, L_in=DS)
    kc = cache_swa_k[l].reshape(DB, WINDOW, SWA_KV_WIDTH)
    vc = cache_swa_v[l].reshape(DB, WINDOW, SWA_KV_WIDTH)
    hs_s, k_win_s, v_win_s = _swa_sample(r3s(sq), r3s(sk), r3s(sv), kc, vc, sinks)
    y_s = _ffn(hm_s.reshape(DB * DS, ML_WIDTH), hs_s.reshape(DB * DS, SWA_WIDTH), xs, W)
    kv_shape_s = (1, DB, WINDOW, SWA_KV_HEADS, SWA_HEAD_DIM)

    return (y_p.reshape(B, S, D_MODEL), y_s.reshape(DB, DS, D_MODEL),
            k_win_p, v_win_p, c_p[None], n_p[None], m_p[None, :, 0, :ML_HEADS],
            k_win_s.reshape(kv_shape_s), v_win_s.reshape(kv_shape_s),
            c_s[None], n_s[None], m_s[None, :, 0, :ML_HEADS])
```

```python
import functools

import jax
import jax.numpy as jnp
import numpy as np
from jax import lax
from jax.experimental import pallas as pl
from jax.experimental.pallas import tpu as pltpu

F32 = jnp.float32
BF16 = jnp.bfloat16

D_MODEL = 1024
SEQ = 8192
DEC_SEQ = 32
PAST_LEN = 4096
NORM_EPS = 1e-6
ML_HEADS = 4
ML_HEAD_DIM = 128
ML_WIDTH = ML_HEADS * ML_HEAD_DIM
SWA_HEADS = 8
SWA_KV_HEADS = 2
SWA_GROUP = SWA_HEADS // SWA_KV_HEADS
SWA_HEAD_DIM = 64
SWA_WIDTH = SWA_HEADS * SWA_HEAD_DIM
SWA_KV_WIDTH = SWA_KV_HEADS * SWA_HEAD_DIM
WINDOW = 128
SWA_CHUNK = 64
ROPE_THETA = 500000.0
ROPE_DIM = SWA_HEAD_DIM // 4
ROPE_HALF = ROPE_DIM // 2
PEER_HEADS = 8
N_KEYS = 128
N_EXPERTS = N_KEYS * N_KEYS
PEER_TOPK = 16
PEER_KEY_DIM = 256
PEER_HALF = PEER_KEY_DIM // 2

LANES = 128
VMEM_LIMIT = 52 * 1024 * 1024

COL_MQ, COL_MK, COL_MV, COL_MO = 0, ML_WIDTH, 2 * ML_WIDTH, 3 * ML_WIDTH
COL_SQ = 4 * ML_WIDTH
COL_SK = COL_SQ + SWA_WIDTH
COL_SV = COL_SK + SWA_KV_WIDTH
COL_G = COL_SV + SWA_KV_WIDTH
IN_COLS_PAD = COL_G + LANES

TM_INPROJ = 256
ML_CHUNK = 128
SWA_TQ = 256
TM_OUTPROJ = 256
TM_ROUTE = 128
TM_PEER = 256
PEER_SUB = 512
PEER_NSUB = 4
PEER_RB = 16

NT_DIMS = (((1,), (1,)), ((), ()))
TN_DIMS = (((0,), (0,)), ((), ()))


def _rms(x, w):
    return x * lax.rsqrt(jnp.mean(x * x, axis=-1, keepdims=True) + NORM_EPS) * w


def _inproj_kernel(x_ref, nw_ref, w_ref, bias_ref, cos_ref, sina_ref, sinb_ref,
                   qm_ref, km_ref, vm_ref, og_ref, sq_ref, sk_ref, sv_ref, gc_ref):
    h = _rms(x_ref[...], nw_ref[...])
    proj = jnp.dot(h.astype(BF16), w_ref[...], preferred_element_type=F32)
    qm_ref[...] = proj[:, COL_MQ:COL_MQ + ML_WIDTH]
    km_ref[...] = proj[:, COL_MK:COL_MK + ML_WIDTH] * (ML_HEAD_DIM ** -0.5)
    vm_ref[...] = proj[:, COL_MV:COL_MV + ML_WIDTH]
    og_ref[...] = jax.nn.sigmoid(proj[:, COL_MO:COL_MO + ML_WIDTH])
    cosf, sina, sinb = cos_ref[...], sina_ref[...], sinb_ref[...]

    def rope(xc):
        return (xc * cosf + pltpu.roll(xc, LANES - ROPE_HALF, 1) * sina
                + pltpu.roll(xc, ROPE_HALF, 1) * sinb)

    for j in range(SWA_WIDTH // LANES):
        sq_ref[:, j * LANES:(j + 1) * LANES] = rope(proj[:, COL_SQ + j * LANES:COL_SQ + (j + 1) * LANES])
    sk_ref[...] = rope(proj[:, COL_SK:COL_SK + LANES])
    sv_ref[...] = proj[:, COL_SV:COL_SV + LANES]
    g = proj[:, COL_G:COL_G + LANES] + bias_ref[...]
    lane = lax.broadcasted_iota(jnp.int32, g.shape, 1)
    gc_ref[...] = jnp.where(lane < ML_HEADS, g, jax.nn.log_sigmoid(g))


def _inproj(x2d, nw, w_perm, bias_pad, cos_t, sina_t, sinb_t, tab_map):
    T = x2d.shape[0]
    TM = TM_INPROJ
    row = lambda i: (i, 0)
    const = lambda i: (0, 0)
    f = lambda n: jax.ShapeDtypeStruct((T, n), F32)
    return pl.pallas_call(
        _inproj_kernel,
        grid=(T // TM,),
        in_specs=[pl.BlockSpec((TM, D_MODEL), row),
                  pl.BlockSpec((1, D_MODEL), const),
                  pl.BlockSpec((D_MODEL, IN_COLS_PAD), const),
                  pl.BlockSpec((1, LANES), const),
                  pl.BlockSpec((TM, LANES), tab_map),
                  pl.BlockSpec((TM, LANES), tab_map),
                  pl.BlockSpec((TM, LANES), tab_map)],
        out_specs=[pl.BlockSpec((TM, ML_WIDTH), row)] * 4
                  + [pl.BlockSpec((TM, SWA_WIDTH), row),
                     pl.BlockSpec((TM, LANES), row), pl.BlockSpec((TM, LANES), row),
                     pl.BlockSpec((TM, LANES), row)],
        out_shape=[f(ML_WIDTH)] * 4 + [f(SWA_WIDTH), f(LANES), f(LANES), f(LANES)],
        compiler_params=pltpu.CompilerParams(dimension_semantics=("arbitrary",),
                                             vmem_limit_bytes=VMEM_LIMIT),
        name="inproj",
    )(x2d, nw, w_perm, bias_pad, cos_t, sina_t, sinb_t)


def _mlstm_kernel(q_ref, k_ref, v_ref, og_ref, gc_ref, c0_ref, n0_ref, m0_ref, nw_ref,
                  hm_ref, c_ref, n_ref, m_ref, *, BB, L_in):
    LP = ML_CHUNK

    @pl.when(pl.program_id(1) == 0)
    def _():
        c_ref[...] = c0_ref[...]
        n_ref[...] = n0_ref[...]
        m_ref[...] = m0_ref[...]

    row = lax.broadcasted_iota(jnp.int32, (LP, LP), 0)
    col = lax.broadcasted_iota(jnp.int32, (LP, LP), 1)
    causal = row >= col
    tri = causal.astype(F32)
    tri_t = (row <= col).astype(F32)
    lane = lax.broadcasted_iota(jnp.int32, (LP, LANES), 1)
    lane1 = lax.broadcasted_iota(jnp.int32, (1, LANES), 1)

    def pad_rows(a):
        if L_in == LP:
            return a
        return jnp.concatenate([a, jnp.zeros((LP - L_in, a.shape[1]), a.dtype)], axis=0)

    for b in range(BB):
        gc = gc_ref[b]
        if L_in < LP:
            lane_pad = lax.broadcasted_iota(jnp.int32, (LP - L_in, LANES), 1)
            fill = jnp.where(lane_pad < ML_HEADS, -jnp.inf, 0.0).astype(F32)
            gc = jnp.concatenate([gc, fill], axis=0)
        gr = gc.T
        lf_c = jnp.where(lane >= ML_HEADS, gc, 0.0)
        b_c = jnp.dot(tri, lf_c, precision=lax.Precision.HIGHEST, preferred_element_type=F32)
        rowi = lax.broadcasted_iota(jnp.int32, (8, LP), 0)
        lf_r = jnp.where(rowi >= ML_HEADS, gr[:8], 0.0)
        b_r = jnp.dot(lf_r, tri_t, precision=lax.Precision.HIGHEST, preferred_element_type=F32)
        m_all = m_ref[b]
        m_out = m_all
        q_all = pad_rows(q_ref[b])
        k_all = pad_rows(k_ref[b])
        v_all = pad_rows(v_ref[b])
        og_all = og_ref[b]
        for h in range(ML_HEADS):
            sl = slice(h * ML_HEAD_DIM, (h + 1) * ML_HEAD_DIM)
            q, k, v = q_all[:, sl], k_all[:, sl], v_all[:, sl]
            qb, kb = q.astype(BF16), k.astype(BF16)
            ig_c = gc[:, h:h + 1]
            b_ch = b_c[:, ML_HEADS + h:ML_HEADS + h + 1]
            ig_r = gr[h:h + 1, :]
            b_rh = b_r[ML_HEADS + h:ML_HEADS + h + 1, :]
            m_prev = m_all[:, h:h + 1]
            logd = jnp.where(causal, b_ch - b_rh + ig_r, -jnp.inf)
            m_inter = b_ch + m_prev
            m_t = jnp.maximum(m_inter, jnp.max(logd, axis=1, keepdims=True))
            dmat = jnp.exp(logd - m_t)
            w_int = jnp.exp(m_inter - m_t)
            s = lax.dot_general(qb, kb, NT_DIMS, preferred_element_type=F32)
            qk = s * dmat
            c_old = c_ref[b, h]
            n_old = n_ref[b, h:h + 1, :]
            num = (w_int * lax.dot_general(qb, c_old.astype(BF16), NT_DIMS, preferred_element_type=F32)
                   + jnp.dot(qk.astype(BF16), v.astype(BF16), preferred_element_type=F32))
            den = (w_int * jnp.sum(q * n_old, axis=1, keepdims=True)
                   + jnp.sum(qk, axis=1, keepdims=True))
            hh = num / jnp.maximum(jnp.abs(den), jnp.exp(-m_t))
            b_last = b_ch[LP - 1:LP, :]
            logw = b_last - b_ch + ig_c
            m_new = jnp.maximum(b_last + m_prev, jnp.max(logw, axis=0, keepdims=True))
            w_c = jnp.exp(logw - m_new)
            decay = jnp.exp(b_last + m_prev - m_new)
            c_ref[b, h] = decay * c_old + lax.dot_general(
                (v * w_c).astype(BF16), kb, TN_DIMS, preferred_element_type=F32)
            n_ref[b, h:h + 1, :] = decay * n_old + jnp.sum(w_c * k, axis=0, keepdims=True)
            m_out = jnp.where(lane1 == h, m_new, m_out)
            y = _rms(hh, nw_ref[:, sl])
            hm_ref[b, :, sl] = (og_all[:, sl] * y[:L_in]).astype(BF16)
        m_ref[b] = m_out


def _mlstm(qm, km, vm, og, gc, c0, n0, m0, nw, *, L_in):
    B, T, _ = qm.shape
    BB = 2
    nchunks = T // L_in
    tok = lambda n: pl.BlockSpec((BB, L_in, n), lambda g, c: (g, c, 0))
    st_c = pl.BlockSpec((BB, ML_HEADS, ML_HEAD_DIM, ML_HEAD_DIM), lambda g, c: (g, 0, 0, 0))
    st_n = pl.BlockSpec((BB, ML_HEADS, ML_HEAD_DIM), lambda g, c: (g, 0, 0))
    st_m = pl.BlockSpec((BB, 1, LANES), lambda g, c: (g, 0, 0))
    return pl.pallas_call(
        functools.partial(_mlstm_kernel, BB=BB, L_in=L_in),
        grid=(B // BB, nchunks),
        in_specs=[tok(ML_WIDTH), tok(ML_WIDTH), tok(ML_WIDTH), tok(ML_WIDTH), tok(LANES),
                  st_c, st_n, st_m, pl.BlockSpec((1, ML_WIDTH), lambda g, c: (0, 0))],
        out_specs=[tok(ML_WIDTH), st_c, st_n, st_m],
        out_shape=[jax.ShapeDtypeStruct((B, T, ML_WIDTH), BF16),
                   jax.ShapeDtypeStruct(c0.shape, F32),
                   jax.ShapeDtypeStruct(n0.shape, F32),
                   jax.ShapeDtypeStruct(m0.shape, F32)],
        compiler_params=pltpu.CompilerParams(dimension_semantics=("arbitrary", "arbitrary"),
                                             vmem_limit_bytes=VMEM_LIMIT),
        name="mlstm",
    )(qm, km, vm, og, gc, c0, n0, m0, nw)


def _sink_attention(q4, kw, vw, sink_col, valid):
    s = lax.dot_general(q4, kw, NT_DIMS, preferred_element_type=F32) * (SWA_HEAD_DIM ** -0.5)
    if valid is not None:
        s = jnp.where(valid, s, -jnp.inf)
    mx = jnp.maximum(jnp.max(s, axis=1, keepdims=True), sink_col)
    p = jnp.exp(s - mx)
    den = jnp.sum(p, axis=1, keepdims=True) + jnp.exp(sink_col - mx)
    return jnp.dot(p.astype(BF16), vw, preferred_element_type=F32) / den


def _sink_column(sink_ref, g, rows):
    return jnp.concatenate(
        [jnp.broadcast_to(sink_ref[:, g * SWA_GROUP + j:g * SWA_GROUP + j + 1], (rows, 1))
         for j in range(SWA_GROUP)], axis=0)


def _swa_prompt_kernel(q_ref, kc_ref, kp_ref, vc_ref, vp_ref, sink_ref, o_ref):
    TQ = SWA_TQ
    i = pl.program_id(1)
    q = q_ref[0]
    kcat = jnp.concatenate([kp_ref[0, TQ - WINDOW:, :], kc_ref[0]], axis=0).astype(BF16)
    vcat = jnp.concatenate([vp_ref[0, TQ - WINDOW:, :], vc_ref[0]], axis=0).astype(BF16)
    span = WINDOW + SWA_CHUNK
    key_iota = lax.broadcasted_iota(jnp.int32, (1, span), 1)
    for g in range(SWA_KV_HEADS):
        gs = slice(g * SWA_HEAD_DIM, (g + 1) * SWA_HEAD_DIM)
        kg, vg = kcat[:, gs], vcat[:, gs]
        sink_col = _sink_column(sink_ref, g, SWA_CHUNK)
        for c in range(TQ // SWA_CHUNK):
            rs = slice(c * SWA_CHUNK, (c + 1) * SWA_CHUNK)
            q4 = jnp.concatenate(
                [q[rs, (g * SWA_GROUP + j) * SWA_HEAD_DIM:(g * SWA_GROUP + j + 1) * SWA_HEAD_DIM]
                 for j in range(SWA_GROUP)], axis=0).astype(BF16)
            ws = slice(c * SWA_CHUNK, c * SWA_CHUNK + span)
            valid = (i * TQ + c * SWA_CHUNK - WINDOW + key_iota) >= 0
            o = _sink_attention(q4, kg[ws], vg[ws], sink_col, valid)
            for j in range(SWA_GROUP):
                hd = (g * SWA_GROUP + j) * SWA_HEAD_DIM
                o_ref[0, rs, hd:hd + SWA_HEAD_DIM] = o[j * SWA_CHUNK:(j + 1) * SWA_CHUNK].astype(BF16)


def _swa_prompt(sq, sk, sv, sinks):
    B, T, _ = sq.shape
    TQ = SWA_TQ
    cur = lambda b, i: (b, i, 0)
    prev = lambda b, i: (b, jnp.maximum(i - 1, 0), 0)
    return pl.pallas_call(
        _swa_prompt_kernel,
        grid=(B, T // TQ),
        in_specs=[pl.BlockSpec((1, TQ, SWA_WIDTH), cur),
                  pl.BlockSpec((1, TQ, SWA_KV_WIDTH), cur), pl.BlockSpec((1, TQ, SWA_KV_WIDTH), prev),
                  pl.BlockSpec((1, TQ, SWA_KV_WIDTH), cur), pl.BlockSpec((1, TQ, SWA_KV_WIDTH), prev),
                  pl.BlockSpec((1, SWA_HEADS), lambda b, i: (0, 0))],
        out_specs=pl.BlockSpec((1, TQ, SWA_WIDTH), cur),
        out_shape=jax.ShapeDtypeStruct((B, T, SWA_WIDTH), BF16),
        compiler_params=pltpu.CompilerParams(dimension_semantics=("arbitrary", "arbitrary"),
                                             vmem_limit_bytes=VMEM_LIMIT),
        name="swa_prompt",
    )(sq, sk, sk, sv, sv, sinks)


def _swa_sample_kernel(q_ref, kn_ref, vn_ref, kc_ref, vc_ref, sink_ref, o_ref, kw_ref, vw_ref):
    T = DEC_SEQ
    q = q_ref[0]
    k_all = jnp.concatenate([kc_ref[0], kn_ref[0]], axis=0)
    v_all = jnp.concatenate([vc_ref[0], vn_ref[0]], axis=0)
    kw_ref[0] = k_all[T:]
    vw_ref[0] = v_all[T:]
    kb, vb = k_all.astype(BF16), v_all.astype(BF16)
    for g in range(SWA_KV_HEADS):
        gs = slice(g * SWA_HEAD_DIM, (g + 1) * SWA_HEAD_DIM)
        q4 = jnp.concatenate(
            [q[:, (g * SWA_GROUP + j) * SWA_HEAD_DIM:(g * SWA_GROUP + j + 1) * SWA_HEAD_DIM]
             for j in range(SWA_GROUP)], axis=0).astype(BF16)
        o = _sink_attention(q4, kb[:, gs], vb[:, gs], _sink_column(sink_ref, g, T), None)
        for j in range(SWA_GROUP):
            hd = (g * SWA_GROUP + j) * SWA_HEAD_DIM
            o_ref[0, :, hd:hd + SWA_HEAD_DIM] = o[j * T:(j + 1) * T].astype(BF16)


def _swa_sample(sq, sk, sv, k_cache, v_cache, sinks):
    B, T, _ = sq.shape
    b3 = lambda b: (b, 0, 0)
    return pl.pallas_call(
        _swa_sample_kernel,
        grid=(B,),
        in_specs=[pl.BlockSpec((1, T, SWA_WIDTH), b3),
                  pl.BlockSpec((1, T, SWA_KV_WIDTH), b3), pl.BlockSpec((1, T, SWA_KV_WIDTH), b3),
                  pl.BlockSpec((1, WINDOW, SWA_KV_WIDTH), b3), pl.BlockSpec((1, WINDOW, SWA_KV_WIDTH), b3),
                  pl.BlockSpec((1, SWA_HEADS), lambda b: (0, 0))],
        out_specs=[pl.BlockSpec((1, T, SWA_WIDTH), b3),
                   pl.BlockSpec((1, WINDOW, SWA_KV_WIDTH), b3), pl.BlockSpec((1, WINDOW, SWA_KV_WIDTH), b3)],
        out_shape=[jax.ShapeDtypeStruct((B, T, SWA_WIDTH), BF16),
                   jax.ShapeDtypeStruct((B, WINDOW, SWA_KV_WIDTH), F32),
                   jax.ShapeDtypeStruct((B, WINDOW, SWA_KV_WIDTH), F32)],
        compiler_params=pltpu.CompilerParams(dimension_semantics=("arbitrary",),
                                             vmem_limit_bytes=VMEM_LIMIT),
        name="swa_sample",
    )(sq, sk, sv, k_cache, v_cache, sinks)


def _outproj_kernel(hm_ref, hs_ref, x_ref, wo_ref, nw_ref, x1_ref, h2_ref):
    mix = (jnp.dot(hm_ref[...], wo_ref[:ML_WIDTH, :], preferred_element_type=F32)
           + jnp.dot(hs_ref[...], wo_ref[ML_WIDTH:, :], preferred_element_type=F32))
    x1 = x_ref[...] + mix
    x1_ref[...] = x1
    h2_ref[...] = _rms(x1, nw_ref[...]).astype(BF16)


def _outproj(hm, hs, x2d, wo, nw):
    T = x2d.shape[0]
    TM = TM_OUTPROJ
    row = lambda i: (i, 0)
    const = lambda i: (0, 0)
    return pl.pallas_call(
        _outproj_kernel,
        grid=(T // TM,),
        in_specs=[pl.BlockSpec((TM, ML_WIDTH), row), pl.BlockSpec((TM, SWA_WIDTH), row),
                  pl.BlockSpec((TM, D_MODEL), row),
                  pl.BlockSpec((D_MODEL, D_MODEL), const), pl.BlockSpec((1, D_MODEL), const)],
        out_specs=[pl.BlockSpec((TM, D_MODEL), row), pl.BlockSpec((TM, D_MODEL), row)],
        out_shape=[jax.ShapeDtypeStruct((T, D_MODEL), F32), jax.ShapeDtypeStruct((T, D_MODEL), BF16)],
        compiler_params=pltpu.CompilerParams(dimension_semantics=("arbitrary",),
                                             vmem_limit_bytes=VMEM_LIMIT),
        name="outproj",
    )(hm, hs, x2d, wo, nw)


_CAND_NB = [PEER_TOPK // (a + 1) for a in range(PEER_TOPK)]
_CAND_ROWS = 16 + 8 * 7 + 8


def _extract_top16(S):
    R = S.shape[0]
    iota = lax.broadcasted_iota(jnp.int32, S.shape, 0)
    rank = jnp.full(S.shape, float(PEER_TOPK), F32)
    vals = []
    for r in range(PEER_TOPK):
        mx = jnp.max(S, axis=0, keepdims=True)
        idx = jnp.min(jnp.where(S == mx, iota, R), axis=0, keepdims=True)
        hit = iota == idx
        rank = jnp.where(hit, float(r), rank)
        S = jnp.where(hit, -jnp.inf, S)
        vals.append(mx)
    return vals, rank


def _route_kernel(h2_ref, wq_ref, sk1_ref, sk2_ref, r2_ref, e2_ref, lim_ref, e1_ref):
    N = h2_ref.shape[0]
    q = jnp.dot(h2_ref[...], wq_ref[...], preferred_element_type=F32).astype(BF16)
    row8 = lax.broadcasted_iota(jnp.int32, (8, N), 0)
    rowc = lax.broadcasted_iota(jnp.int32, (_CAND_ROWS, N), 0)
    mid = rowc - 16
    flat = jnp.where(rowc < 16, rowc,
                     jnp.where(rowc < _CAND_ROWS - 8,
                               PEER_TOPK * ((mid >> 3) + 1) + (mid & 7),
                               PEER_TOPK * (rowc - (_CAND_ROWS - 16))))
    for h in range(PEER_HEADS):
        c0 = h * PEER_KEY_DIM
        s1 = lax.dot_general(sk1_ref[...], q[:, c0:c0 + PEER_HALF], NT_DIMS, preferred_element_type=F32)
        s2 = lax.dot_general(sk2_ref[...], q[:, c0 + PEER_HALF:c0 + PEER_KEY_DIM], NT_DIMS,
                             preferred_element_type=F32)
        t1, r1 = _extract_top16(s1)
        t2, r2 = _extract_top16(s2)
        t2_16 = jnp.concatenate(t2, axis=0)
        t2_8 = t2_16[:8]
        blocks = [t1[0] + t2_16, t1[1] + t2_8]
        for a in range(2, 8):
            blocks.append(jnp.where(row8 < _CAND_NB[a], t1[a] + t2_8, -jnp.inf))
        blocks.append(jnp.concatenate(t1[8:], axis=0) + t2[0])
        cand = jnp.concatenate(blocks, axis=0)
        sel = jnp.zeros(cand.shape, F32)
        work = cand
        for _ in range(PEER_TOPK):
            mx = jnp.max(work, axis=0, keepdims=True)
            idx = jnp.min(jnp.where(work == mx, flat, PEER_TOPK * PEER_TOPK), axis=0, keepdims=True)
            hit = flat == idx
            sel = jnp.where(hit, 1.0, sel)
            work = jnp.where(hit, -jnp.inf, work)
        z = jnp.sum(jnp.where(sel > 0.0, jnp.exp(cand - cand[0:1]), 0.0), axis=0, keepdims=True)
        counts = [jnp.sum(sel[0:16], axis=0, keepdims=True)]
        for a in range(1, 8):
            counts.append(jnp.sum(sel[8 + 8 * a:16 + 8 * a], axis=0, keepdims=True))
        for a in range(8, PEER_TOPK):
            counts.append(sel[_CAND_ROWS - 16 + a:_CAND_ROWS - 15 + a])
        lim = jnp.zeros(r1.shape, F32)
        for a in range(PEER_TOPK):
            lim = jnp.where(r1 == float(a), counts[a], lim)
        r2_ref[h] = pltpu.bitcast(r2.astype(BF16), jnp.uint32)
        e2_ref[h] = pltpu.bitcast(jnp.exp(s2 - t2[0]).astype(BF16), jnp.uint32)
        lim_ref[h] = lim
        e1_ref[h] = jnp.exp(s1 - t1[0]) / z


def _route(h2, wq, sk1, sk2):
    T = h2.shape[0]
    TM = TM_ROUTE
    const = lambda i: (0, 0)
    tab = pl.BlockSpec((PEER_HEADS, N_KEYS, TM), lambda i: (0, 0, i))
    tab_packed = pl.BlockSpec((PEER_HEADS, N_KEYS // 2, TM), lambda i: (0, 0, i))
    tab_shape = jax.ShapeDtypeStruct((PEER_HEADS, N_KEYS, T), F32)
    tab_packed_shape = jax.ShapeDtypeStruct((PEER_HEADS, N_KEYS // 2, T), jnp.uint32)
    return pl.pallas_call(
        _route_kernel,
        grid=(T // TM,),
        in_specs=[pl.BlockSpec((TM, D_MODEL), lambda i: (i, 0)),
                  pl.BlockSpec((D_MODEL, PEER_HEADS * PEER_KEY_DIM), const),
                  pl.BlockSpec((N_KEYS, PEER_HALF), const), pl.BlockSpec((N_KEYS, PEER_HALF), const)],
        out_specs=[tab_packed, tab_packed, tab, tab],
        out_shape=[tab_packed_shape, tab_packed_shape, tab_shape, tab_shape],
        compiler_params=pltpu.CompilerParams(dimension_semantics=("arbitrary",),
                                             vmem_limit_bytes=VMEM_LIMIT),
        name="route",
    )(h2, wq, sk1, sk2)


def _peer_kernel(h2_ref, x1_ref, u_ref, vt_ref, r2_ref, e2_ref, lim_ref, e1_ref, nfw_ref,
                 y_ref, acc_ref, act_ref, p_ref):
    e = pl.program_id(1)
    per_sub = PEER_SUB // N_KEYS

    @pl.when(e == 0)
    def _():
        acc_ref[...] = jnp.zeros(acc_ref.shape, F32)

    h2 = h2_ref[...]
    for s in range(PEER_NSUB):
        act_ref[s * PEER_SUB:(s + 1) * PEER_SUB, :] = lax.dot_general(
            u_ref[s * PEER_SUB:(s + 1) * PEER_SUB, :], h2, NT_DIMS, preferred_element_type=F32)

    n_rb = N_KEYS // PEER_RB
    for jb in range(PEER_NSUB * per_sub):
        j = e * (PEER_NSUB * per_sub) + jb
        for lh in range(h2.shape[0] // LANES):
            cols = slice(lh * LANES, (lh + 1) * LANES)
            g = [None] * n_rb
            zero = jnp.zeros((PEER_RB, LANES), BF16)
            for h in range(PEER_HEADS):
                lim = jnp.broadcast_to(lim_ref[h, pl.ds(j, 1), :][:, cols], (PEER_RB, LANES)).astype(BF16)
                e1 = jnp.broadcast_to(e1_ref[h, pl.ds(j, 1), :][:, cols], (PEER_RB, LANES)).astype(BF16)
                for rb in range(n_rb):
                    words = slice(rb * PEER_RB // 2, (rb + 1) * PEER_RB // 2)
                    r2 = pltpu.bitcast(r2_ref[h, words, cols], BF16)
                    e2 = pltpu.bitcast(e2_ref[h, words, cols], BF16)
                    t = jnp.where(r2 < lim, e2, zero) * e1
                    g[rb] = t if g[rb] is None else g[rb] + t
            for rb in range(n_rb):
                arows = slice(jb * N_KEYS + rb * PEER_RB, jb * N_KEYS + (rb + 1) * PEER_RB)
                a = act_ref[arows, cols]
                ga = 0.5 * a * (1.0 + lax.erf(a * np.float32(np.sqrt(0.5))))
                p_ref[arows, cols] = g[rb] * ga.astype(BF16)

    acc_ref[...] += jnp.dot(vt_ref[...], p_ref[...], preferred_element_type=F32)

    @pl.when(e == pl.num_programs(1) - 1)
    def _():
        x = x1_ref[...] + acc_ref[...].T
        y_ref[...] = _rms(x, nfw_ref[...])


def _peer(h2, x1, u, vt, r2, e2, lim, e1, nfw):
    T = h2.shape[0]
    TM = TM_PEER
    ET = PEER_SUB * PEER_NSUB
    n_e = N_EXPERTS // ET
    tok = lambda i, e: (i, 0)
    tab = pl.BlockSpec((PEER_HEADS, N_KEYS, TM), lambda i, e: (0, 0, i))
    tab_packed = pl.BlockSpec((PEER_HEADS, N_KEYS // 2, TM), lambda i, e: (0, 0, i))
    return pl.pallas_call(
        _peer_kernel,
        grid=(T // TM, n_e),
        in_specs=[pl.BlockSpec((TM, D_MODEL), tok), pl.BlockSpec((TM, D_MODEL), tok),
                  pl.BlockSpec((ET, D_MODEL), lambda i, e: (e, 0)),
                  pl.BlockSpec((D_MODEL, ET), lambda i, e: (0, e)),
                  tab_packed, tab_packed, tab, tab,
                  pl.BlockSpec((1, D_MODEL), lambda i, e: (0, 0))],
        out_specs=pl.BlockSpec((TM, D_MODEL), tok),
        out_shape=jax.ShapeDtypeStruct((T, D_MODEL), F32),
        scratch_shapes=[pltpu.VMEM((D_MODEL, TM), F32), pltpu.VMEM((ET, TM), F32), pltpu.VMEM((ET, TM), BF16)],
        compiler_params=pltpu.CompilerParams(dimension_semantics=("arbitrary", "arbitrary"),
                                             vmem_limit_bytes=VMEM_LIMIT),
        name="peer",
    )(h2, x1, u, vt, r2, e2, lim, e1, nfw)


def _rope_tables(pos):
    inv = ROPE_THETA ** (-jnp.arange(ROPE_HALF, dtype=F32) * 2.0 / ROPE_DIM)
    ang = pos.astype(F32)[:, None] * inv[None, :]
    cos, sin = jnp.cos(ang), jnp.sin(ang)
    n = pos.shape[0]
    rest = SWA_HEAD_DIM - ROPE_DIM
    zh = jnp.zeros((n, ROPE_HALF), F32)
    cos_h = jnp.concatenate([cos, cos, jnp.ones((n, rest), F32)], axis=1)
    sina_h = jnp.concatenate([-sin, zh, jnp.zeros((n, rest), F32)], axis=1)
    sinb_h = jnp.concatenate([zh, sin, jnp.zeros((n, rest), F32)], axis=1)
    rep = LANES // SWA_HEAD_DIM
    return tuple(jnp.tile(t, (1, rep)) for t in (cos_h, sina_h, sinb_h))


def _layer_tokens(x2d, tables, tab_map, W):
    return _inproj(x2d, W["norm_mix"], W["w_in"], W["bias"], *tables, tab_map)


def _ffn(hm, hs, x2d, W):
    x1, h2 = _outproj(hm, hs, x2d, W["w_out"], W["norm_ffn"])
    r2, e2, lim, e1 = _route(h2, W["w_q"], W["sk1"], W["sk2"])
    return _peer(h2, x1, W["u"], W["vt"], r2, e2, lim, e1, W["norm_final"])


def kernel(x_prompt, x_sample, cache_swa_k, cache_swa_v, state_mlstm_c, state_mlstm_n, state_mlstm_m,
           norm_mix_w, w_in, mlstm_if_bias, mlstm_norm_w, swa_sinks, w_out, norm_ffn_w,
           peer_w_q, peer_sub_keys_1, peer_sub_keys_2, peer_u, peer_v, norm_final_w):
    B, S, _ = x_prompt.shape
    DB, DS, _ = x_sample.shape
    l = 0
    wi = w_in[l]
    s_q = 4 * ML_WIDTH + 2 * ML_HEADS
    w_perm = jnp.concatenate(
        [wi[:, :4 * ML_WIDTH], wi[:, s_q:], wi[:, 4 * ML_WIDTH:s_q],
         jnp.zeros((D_MODEL, LANES - 2 * ML_HEADS), F32)], axis=1).astype(BF16)
    bias_pad = jnp.concatenate([mlstm_if_bias[l], jnp.zeros((LANES - 2 * ML_HEADS,), F32)])[None, :]
    W = {
        "norm_mix": norm_mix_w[l][None, :],
        "w_in": w_perm,
        "bias": bias_pad,
        "w_out": w_out[l].astype(BF16),
        "norm_ffn": norm_ffn_w[l][None, :],
        "w_q": peer_w_q[l].astype(BF16),
        "sk1": peer_sub_keys_1[l].astype(BF16),
        "sk2": peer_sub_keys_2[l].astype(BF16),
        "u": peer_u[l].astype(BF16),
        "vt": peer_v[l].astype(BF16).T,
        "norm_final": norm_final_w[None, :],
    }
    ml_nw = mlstm_norm_w[l][None, :]
    sinks = swa_sinks[l][None, :]

    xp = x_prompt.reshape(B * S, D_MODEL)
    tiles_per_seq = S // TM_INPROJ
    tabs_p = _rope_tables(jnp.arange(S, dtype=jnp.int32))
    qm, km, vm, og, sq, sk, sv, gc = _layer_tokens(xp, tabs_p, lambda i: (i % tiles_per_seq, 0), W)
    r3 = lambda a: a.reshape(B, S, a.shape[-1])
    zc = jnp.zeros((B, ML_HEADS, ML_HEAD_DIM, ML_HEAD_DIM), F32)
    zn = jnp.zeros((B, ML_HEADS, ML_HEAD_DIM), F32)
    zm = jnp.zeros((B, 1, LANES), F32)
    hm_p, c_p, n_p, m_p = _mlstm(r3(qm), r3(km), r3(vm), r3(og), r3(gc), zc, zn, zm, ml_nw, L_in=ML_CHUNK)
    sk3, sv3 = r3(sk), r3(sv)
    hs_p = _swa_prompt(r3(sq), sk3, sv3, sinks)
    y_p = _ffn(hm_p.reshape(B * S, ML_WIDTH), hs_p.reshape(B * S, SWA_WIDTH), xp, W)
    kv_shape = (1, B, WINDOW, SWA_KV_HEADS, SWA_HEAD_DIM)
    k_win_p = sk3[:, S - WINDOW:].reshape(kv_shape)
    v_win_p = sv3[:, S - WINDOW:].reshape(kv_shape)

    xs = x_sample.reshape(DB * DS, D_MODEL)
    pos_s = PAST_LEN + jnp.arange(DS, dtype=jnp.int32)
    tabs_s = tuple(jnp.tile(t, (DB, 1)) for t in _rope_tables(pos_s))
    qm, km, vm, og, sq, sk, sv, gc = _layer_tokens(xs, tabs_s, lambda i: (i, 0), W)
    r3s = lambda a: a.reshape(DB, DS, a.shape[-1])
    m0 = jnp.concatenate([state_mlstm_m[l], jnp.zeros((DB, LANES - ML_HEADS), F32)], axis=1)[:, None, :]
    hm_s, c_s, n_s, m_s = _mlstm(r3s(qm), r3s(km), r3s(vm), r3s(og), r3s(gc),
                                 state_mlstm_c[l], state_mlstm_n[l], m0, ml_nw, L_in=DS)
    kc = cache_swa_k[l].reshape(DB, WINDOW, SWA_KV_WIDTH)
    vc = cache_swa_v[l].reshape(DB, WINDOW, SWA_KV_WIDTH)
    hs_s, k_win_s, v_win_s = _swa_sample(r3s(sq), r3s(sk), r3s(sv), kc, vc, sinks)
    y_s = _ffn(hm_s.reshape(DB * DS, ML_WIDTH), hs_s.reshape(DB * DS, SWA_WIDTH), xs, W)
    kv_shape_s = (1, DB, WINDOW, SWA_KV_HEADS, SWA_HEAD_DIM)

    return (y_p.reshape(B, S, D_MODEL), y_s.reshape(DB, DS, D_MODEL),
            k_win_p, v_win_p, c_p[None], n_p[None], m_p[None, :, 0, :ML_HEADS],
            k_win_s.reshape(kv_shape_s), v_win_s.reshape(kv_shape_s),
            c_s[None], n_s[None], m_s[None, :, 0, :ML_HEADS])
```

```python
import functools

import jax
import jax.numpy as jnp
import numpy as np
from jax import lax
from jax.experimental import pallas as pl
from jax.experimental.pallas import tpu as pltpu

F32 = jnp.float32
BF16 = jnp.bfloat16

D_MODEL = 1024
SEQ = 8192
DEC_SEQ = 32
PAST_LEN = 4096
NORM_EPS = 1e-6
ML_HEADS = 4
ML_HEAD_DIM = 128
ML_WIDTH = ML_HEADS * ML_HEAD_DIM
SWA_HEADS = 8
SWA_KV_HEADS = 2
SWA_GROUP = SWA_HEADS // SWA_KV_HEADS
SWA_HEAD_DIM = 64
SWA_WIDTH = SWA_HEADS * SWA_HEAD_DIM
SWA_KV_WIDTH = SWA_KV_HEADS * SWA_HEAD_DIM
WINDOW = 128
SWA_CHUNK = 64
ROPE_THETA = 500000.0
ROPE_DIM = SWA_HEAD_DIM // 4
ROPE_HALF = ROPE_DIM // 2
PEER_HEADS = 8
N_KEYS = 128
N_EXPERTS = N_KEYS * N_KEYS
PEER_TOPK = 16
PEER_KEY_DIM = 256
PEER_HALF = PEER_KEY_DIM // 2

LANES = 128
VMEM_LIMIT = 52 * 1024 * 1024

COL_MQ, COL_MK, COL_MV, COL_MO = 0, ML_WIDTH, 2 * ML_WIDTH, 3 * ML_WIDTH
COL_SQ = 4 * ML_WIDTH
COL_SK = COL_SQ + SWA_WIDTH
COL_SV = COL_SK + SWA_KV_WIDTH
COL_G = COL_SV + SWA_KV_WIDTH
IN_COLS_PAD = COL_G + LANES

TM_INPROJ = 256
ML_CHUNK = 128
SWA_TQ = 256
TM_OUTPROJ = 256
TM_ROUTE = 128
TM_PEER = 512
PEER_SUB = 512
PEER_NSUB = 2
PEER_RB = 16

NT_DIMS = (((1,), (1,)), ((), ()))
TN_DIMS = (((0,), (0,)), ((), ()))


def _rms(x, w):
    return x * lax.rsqrt(jnp.mean(x * x, axis=-1, keepdims=True) + NORM_EPS) * w


def _inproj_kernel(x_ref, nw_ref, w_ref, bias_ref, cos_ref, sina_ref, sinb_ref,
                   qm_ref, km_ref, vm_ref, og_ref, sq_ref, sk_ref, sv_ref, gc_ref):
    h = _rms(x_ref[...], nw_ref[...])
    proj = jnp.dot(h.astype(BF16), w_ref[...], preferred_element_type=F32)
    qm_ref[...] = proj[:, COL_MQ:COL_MQ + ML_WIDTH]
    km_ref[...] = proj[:, COL_MK:COL_MK + ML_WIDTH] * (ML_HEAD_DIM ** -0.5)
    vm_ref[...] = proj[:, COL_MV:COL_MV + ML_WIDTH]
    og_ref[...] = jax.nn.sigmoid(proj[:, COL_MO:COL_MO + ML_WIDTH])
    cosf, sina, sinb = cos_ref[...], sina_ref[...], sinb_ref[...]

    def rope(xc):
        return (xc * cosf + pltpu.roll(xc, LANES - ROPE_HALF, 1) * sina
                + pltpu.roll(xc, ROPE_HALF, 1) * sinb)

    for j in range(SWA_WIDTH // LANES):
        sq_ref[:, j * LANES:(j + 1) * LANES] = rope(proj[:, COL_SQ + j * LANES:COL_SQ + (j + 1) * LANES])
    sk_ref[...] = rope(proj[:, COL_SK:COL_SK + LANES])
    sv_ref[...] = proj[:, COL_SV:COL_SV + LANES]
    g = proj[:, COL_G:COL_G + LANES] + bias_ref[...]
    lane = lax.broadcasted_iota(jnp.int32, g.shape, 1)
    gc_ref[...] = jnp.where(lane < ML_HEADS, g, jax.nn.log_sigmoid(g))


def _inproj(x2d, nw, w_perm, bias_pad, cos_t, sina_t, sinb_t, tab_map):
    T = x2d.shape[0]
    TM = TM_INPROJ
    row = lambda i: (i, 0)
    const = lambda i: (0, 0)
    f = lambda n: jax.ShapeDtypeStruct((T, n), F32)
    return pl.pallas_call(
        _inproj_kernel,
        grid=(T // TM,),
        in_specs=[pl.BlockSpec((TM, D_MODEL), row),
                  pl.BlockSpec((1, D_MODEL), const),
                  pl.BlockSpec((D_MODEL, IN_COLS_PAD), const),
                  pl.BlockSpec((1, LANES), const),
                  pl.BlockSpec((TM, LANES), tab_map),
                  pl.BlockSpec((TM, LANES), tab_map),
                  pl.BlockSpec((TM, LANES), tab_map)],
        out_specs=[pl.BlockSpec((TM, ML_WIDTH), row)] * 4
                  + [pl.BlockSpec((TM, SWA_WIDTH), row),
                     pl.BlockSpec((TM, LANES), row), pl.BlockSpec((TM, LANES), row),
                     pl.BlockSpec((TM, LANES), row)],
        out_shape=[f(ML_WIDTH)] * 4 + [f(SWA_WIDTH), f(LANES), f(LANES), f(LANES)],
        compiler_params=pltpu.CompilerParams(dimension_semantics=("arbitrary",),
                                             vmem_limit_bytes=VMEM_LIMIT),
        name="inproj",
    )(x2d, nw, w_perm, bias_pad, cos_t, sina_t, sinb_t)


def _mlstm_kernel(q_ref, k_ref, v_ref, og_ref, gc_ref, c0_ref, n0_ref, m0_ref, nw_ref,
                  hm_ref, c_ref, n_ref, m_ref, *, BB, L_in):
    LP = ML_CHUNK

    @pl.when(pl.program_id(1) == 0)
    def _():
        c_ref[...] = c0_ref[...]
        n_ref[...] = n0_ref[...]
        m_ref[...] = m0_ref[...]

    row = lax.broadcasted_iota(jnp.int32, (LP, LP), 0)
    col = lax.broadcasted_iota(jnp.int32, (LP, LP), 1)
    causal = row >= col
    tri = causal.astype(F32)
    tri_t = (row <= col).astype(F32)
    lane = lax.broadcasted_iota(jnp.int32, (LP, LANES), 1)
    lane1 = lax.broadcasted_iota(jnp.int32, (1, LANES), 1)

    def pad_rows(a):
        if L_in == LP:
            return a
        return jnp.concatenate([a, jnp.zeros((LP - L_in, a.shape[1]), a.dtype)], axis=0)

    for b in range(BB):
        gc = gc_ref[b]
        if L_in < LP:
            lane_pad = lax.broadcasted_iota(jnp.int32, (LP - L_in, LANES), 1)
            fill = jnp.where(lane_pad < ML_HEADS, -jnp.inf, 0.0).astype(F32)
            gc = jnp.concatenate([gc, fill], axis=0)
        gr = gc.T
        lf_c = jnp.where(lane >= ML_HEADS, gc, 0.0)
        b_c = jnp.dot(tri, lf_c, precision=lax.Precision.HIGHEST, preferred_element_type=F32)
        rowi = lax.broadcasted_iota(jnp.int32, (8, LP), 0)
        lf_r = jnp.where(rowi >= ML_HEADS, gr[:8], 0.0)
        b_r = jnp.dot(lf_r, tri_t, precision=lax.Precision.HIGHEST, preferred_element_type=F32)
        m_all = m_ref[b]
        m_out = m_all
        q_all = pad_rows(q_ref[b])
        k_all = pad_rows(k_ref[b])
        v_all = pad_rows(v_ref[b])
        og_all = og_ref[b]
        for h in range(ML_HEADS):
            sl = slice(h * ML_HEAD_DIM, (h + 1) * ML_HEAD_DIM)
            q, k, v = q_all[:, sl], k_all[:, sl], v_all[:, sl]
            qb, kb = q.astype(BF16), k.astype(BF16)
            ig_c = gc[:, h:h + 1]
            b_ch = b_c[:, ML_HEADS + h:ML_HEADS + h + 1]
            ig_r = gr[h:h + 1, :]
            b_rh = b_r[ML_HEADS + h:ML_HEADS + h + 1, :]
            m_prev = m_all[:, h:h + 1]
            logd = jnp.where(causal, b_ch - b_rh + ig_r, -jnp.inf)
            m_inter = b_ch + m_prev
            m_t = jnp.maximum(m_inter, jnp.max(logd, axis=1, keepdims=True))
            dmat = jnp.exp(logd - m_t)
            w_int = jnp.exp(m_inter - m_t)
            s = lax.dot_general(qb, kb, NT_DIMS, preferred_element_type=F32)
            qk = s * dmat
            c_old = c_ref[b, h]
            n_old = n_ref[b, h:h + 1, :]
            num = (w_int * lax.dot_general(qb, c_old.astype(BF16), NT_DIMS, preferred_element_type=F32)
                   + jnp.dot(qk.astype(BF16), v.astype(BF16), preferred_element_type=F32))
            den = (w_int * jnp.sum(q * n_old, axis=1, keepdims=True)
                   + jnp.sum(qk, axis=1, keepdims=True))
            hh = num / jnp.maximum(jnp.abs(den), jnp.exp(-m_t))
            b_last = b_ch[LP - 1:LP, :]
            logw = b_last - b_ch + ig_c
            m_new = jnp.maximum(b_last + m_prev, jnp.max(logw, axis=0, keepdims=True))
            w_c = jnp.exp(logw - m_new)
            decay = jnp.exp(b_last + m_prev - m_new)
            c_ref[b, h] = decay * c_old + lax.dot_general(
                (v * w_c).astype(BF16), kb, TN_DIMS, preferred_element_type=F32)
            n_ref[b, h:h + 1, :] = decay * n_old + jnp.sum(w_c * k, axis=0, keepdims=True)
            m_out = jnp.where(lane1 == h, m_new, m_out)
            y = _rms(hh, nw_ref[:, sl])
            hm_ref[b, :, sl] = (og_all[:, sl] * y[:L_in]).astype(BF16)
        m_ref[b] = m_out


def _mlstm(qm, km, vm, og, gc, c0, n0, m0, nw, *, L_in):
    B, T, _ = qm.shape
    BB = 2
    nchunks = T // L_in
    tok = lambda n: pl.BlockSpec((BB, L_in, n), lambda g, c: (g, c, 0))
    st_c = pl.BlockSpec((BB, ML_HEADS, ML_HEAD_DIM, ML_HEAD_DIM), lambda g, c: (g, 0, 0, 0))
    st_n = pl.BlockSpec((BB, ML_HEADS, ML_HEAD_DIM), lambda g, c: (g, 0, 0))
    st_m = pl.BlockSpec((BB, 1, LANES), lambda g, c: (g, 0, 0))
    return pl.pallas_call(
        functools.partial(_mlstm_kernel, BB=BB, L_in=L_in),
        grid=(B // BB, nchunks),
        in_specs=[tok(ML_WIDTH), tok(ML_WIDTH), tok(ML_WIDTH), tok(ML_WIDTH), tok(LANES),
                  st_c, st_n, st_m, pl.BlockSpec((1, ML_WIDTH), lambda g, c: (0, 0))],
        out_specs=[tok(ML_WIDTH), st_c, st_n, st_m],
        out_shape=[jax.ShapeDtypeStruct((B, T, ML_WIDTH), BF16),
                   jax.ShapeDtypeStruct(c0.shape, F32),
                   jax.ShapeDtypeStruct(n0.shape, F32),
                   jax.ShapeDtypeStruct(m0.shape, F32)],
        compiler_params=pltpu.CompilerParams(dimension_semantics=("arbitrary", "arbitrary"),
                                             vmem_limit_bytes=VMEM_LIMIT),
        name="mlstm",
    )(qm, km, vm, og, gc, c0, n0, m0, nw)


def _sink_attention(q4, kw, vw, sink_col, valid):
    s = lax.dot_general(q4, kw, NT_DIMS, preferred_element_type=F32) * (SWA_HEAD_DIM ** -0.5)
    if valid is not None:
        s = jnp.where(valid, s, -jnp.inf)
    mx = jnp.maximum(jnp.max(s, axis=1, keepdims=True), sink_col)
    p = jnp.exp(s - mx)
    den = jnp.sum(p, axis=1, keepdims=True) + jnp.exp(sink_col - mx)
    return jnp.dot(p.astype(BF16), vw, preferred_element_type=F32) / den


def _sink_column(sink_ref, g, rows):
    return jnp.concatenate(
        [jnp.broadcast_to(sink_ref[:, g * SWA_GROUP + j:g * SWA_GROUP + j + 1], (rows, 1))
         for j in range(SWA_GROUP)], axis=0)


def _swa_prompt_kernel(q_ref, kc_ref, kp_ref, vc_ref, vp_ref, sink_ref, o_ref):
    TQ = SWA_TQ
    i = pl.program_id(1)
    q = q_ref[0]
    kcat = jnp.concatenate([kp_ref[0, TQ - WINDOW:, :], kc_ref[0]], axis=0).astype(BF16)
    vcat = jnp.concatenate([vp_ref[0, TQ - WINDOW:, :], vc_ref[0]], axis=0).astype(BF16)
    span = WINDOW + SWA_CHUNK
    key_iota = lax.broadcasted_iota(jnp.int32, (1, span), 1)
    for g in range(SWA_KV_HEADS):
        gs = slice(g * SWA_HEAD_DIM, (g + 1) * SWA_HEAD_DIM)
        kg, vg = kcat[:, gs], vcat[:, gs]
        sink_col = _sink_column(sink_ref, g, SWA_CHUNK)
        for c in range(TQ // SWA_CHUNK):
            rs = slice(c * SWA_CHUNK, (c + 1) * SWA_CHUNK)
            q4 = jnp.concatenate(
                [q[rs, (g * SWA_GROUP + j) * SWA_HEAD_DIM:(g * SWA_GROUP + j + 1) * SWA_HEAD_DIM]
                 for j in range(SWA_GROUP)], axis=0).astype(BF16)
            ws = slice(c * SWA_CHUNK, c * SWA_CHUNK + span)
            valid = (i * TQ + c * SWA_CHUNK - WINDOW + key_iota) >= 0
            o = _sink_attention(q4, kg[ws], vg[ws], sink_col, valid)
            for j in range(SWA_GROUP):
                hd = (g * SWA_GROUP + j) * SWA_HEAD_DIM
                o_ref[0, rs, hd:hd + SWA_HEAD_DIM] = o[j * SWA_CHUNK:(j + 1) * SWA_CHUNK].astype(BF16)


def _swa_prompt(sq, sk, sv, sinks):
    B, T, _ = sq.shape
    TQ = SWA_TQ
    cur = lambda b, i: (b, i, 0)
    prev = lambda b, i: (b, jnp.maximum(i - 1, 0), 0)
    return pl.pallas_call(
        _swa_prompt_kernel,
        grid=(B, T // TQ),
        in_specs=[pl.BlockSpec((1, TQ, SWA_WIDTH), cur),
                  pl.BlockSpec((1, TQ, SWA_KV_WIDTH), cur), pl.BlockSpec((1, TQ, SWA_KV_WIDTH), prev),
                  pl.BlockSpec((1, TQ, SWA_KV_WIDTH), cur), pl.BlockSpec((1, TQ, SWA_KV_WIDTH), prev),
                  pl.BlockSpec((1, SWA_HEADS), lambda b, i: (0, 0))],
        out_specs=pl.BlockSpec((1, TQ, SWA_WIDTH), cur),
        out_shape=jax.ShapeDtypeStruct((B, T, SWA_WIDTH), BF16),
        compiler_params=pltpu.CompilerParams(dimension_semantics=("arbitrary", "arbitrary"),
                                             vmem_limit_bytes=VMEM_LIMIT),
        name="swa_prompt",
    )(sq, sk, sk, sv, sv, sinks)


def _swa_sample_kernel(q_ref, kn_ref, vn_ref, kc_ref, vc_ref, sink_ref, o_ref, kw_ref, vw_ref):
    T = DEC_SEQ
    q = q_ref[0]
    k_all = jnp.concatenate([kc_ref[0], kn_ref[0]], axis=0)
    v_all = jnp.concatenate([vc_ref[0], vn_ref[0]], axis=0)
    kw_ref[0] = k_all[T:]
    vw_ref[0] = v_all[T:]
    kb, vb = k_all.astype(BF16), v_all.astype(BF16)
    for g in range(SWA_KV_HEADS):
        gs = slice(g * SWA_HEAD_DIM, (g + 1) * SWA_HEAD_DIM)
        q4 = jnp.concatenate(
            [q[:, (g * SWA_GROUP + j) * SWA_HEAD_DIM:(g * SWA_GROUP + j + 1) * SWA_HEAD_DIM]
             for j in range(SWA_GROUP)], axis=0).astype(BF16)
        o = _sink_attention(q4, kb[:, gs], vb[:, gs], _sink_column(sink_ref, g, T), None)
        for j in range(SWA_GROUP):
            hd = (g * SWA_GROUP + j) * SWA_HEAD_DIM
            o_ref[0, :, hd:hd + SWA_HEAD_DIM] = o[j * T:(j + 1) * T].astype(BF16)


def _swa_sample(sq, sk, sv, k_cache, v_cache, sinks):
    B, T, _ = sq.shape
    b3 = lambda b: (b, 0, 0)
    return pl.pallas_call(
        _swa_sample_kernel,
        grid=(B,),
        in_specs=[pl.BlockSpec((1, T, SWA_WIDTH), b3),
                  pl.BlockSpec((1, T, SWA_KV_WIDTH), b3), pl.BlockSpec((1, T, SWA_KV_WIDTH), b3),
                  pl.BlockSpec((1, WINDOW, SWA_KV_WIDTH), b3), pl.BlockSpec((1, WINDOW, SWA_KV_WIDTH), b3),
                  pl.BlockSpec((1, SWA_HEADS), lambda b: (0, 0))],
        out_specs=[pl.BlockSpec((1, T, SWA_WIDTH), b3),
                   pl.BlockSpec((1, WINDOW, SWA_KV_WIDTH), b3), pl.BlockSpec((1, WINDOW, SWA_KV_WIDTH), b3)],
        out_shape=[jax.ShapeDtypeStruct((B, T, SWA_WIDTH), BF16),
                   jax.ShapeDtypeStruct((B, WINDOW, SWA_KV_WIDTH), F32),
                   jax.ShapeDtypeStruct((B, WINDOW, SWA_KV_WIDTH), F32)],
        compiler_params=pltpu.CompilerParams(dimension_semantics=("arbitrary",),
                                             vmem_limit_bytes=VMEM_LIMIT),
        name="swa_sample",
    )(sq, sk, sv, k_cache, v_cache, sinks)


def _outproj_kernel(hm_ref, hs_ref, x_ref, wo_ref, nw_ref, x1_ref, h2_ref):
    mix = (jnp.dot(hm_ref[...], wo_ref[:ML_WIDTH, :], preferred_element_type=F32)
           + jnp.dot(hs_ref[...], wo_ref[ML_WIDTH:, :], preferred_element_type=F32))
    x1 = x_ref[...] + mix
    x1_ref[...] = x1
    h2_ref[...] = pltpu.bitcast(_rms(x1, nw_ref[...]).astype(BF16), jnp.uint32)


def _outproj(hm, hs, x2d, wo, nw):
    T = x2d.shape[0]
    TM = TM_OUTPROJ
    row = lambda i: (i, 0)
    const = lambda i: (0, 0)
    return pl.pallas_call(
        _outproj_kernel,
        grid=(T // TM,),
        in_specs=[pl.BlockSpec((TM, ML_WIDTH), row), pl.BlockSpec((TM, SWA_WIDTH), row),
                  pl.BlockSpec((TM, D_MODEL), row),
                  pl.BlockSpec((D_MODEL, D_MODEL), const), pl.BlockSpec((1, D_MODEL), const)],
        out_specs=[pl.BlockSpec((TM, D_MODEL), row), pl.BlockSpec((TM // 2, D_MODEL), row)],
        out_shape=[jax.ShapeDtypeStruct((T, D_MODEL), F32),
                   jax.ShapeDtypeStruct((T // 2, D_MODEL), jnp.uint32)],
        compiler_params=pltpu.CompilerParams(dimension_semantics=("arbitrary",),
                                             vmem_limit_bytes=VMEM_LIMIT),
        name="outproj",
    )(hm, hs, x2d, wo, nw)


_CAND_NB = [PEER_TOPK // (a + 1) for a in range(PEER_TOPK)]
_CAND_ROWS = 16 + 8 * 7 + 8


def _extract_top16(S, exact):
    R = S.shape[0]
    iota = lax.broadcasted_iota(jnp.int32, S.shape, 0)
    rank = jnp.full(S.shape, float(PEER_TOPK), F32)
    vals = []
    for r in range(PEER_TOPK):
        mx = jnp.max(S, axis=0, keepdims=True)
        if exact:
            idx = jnp.min(jnp.where(S == mx, iota, R), axis=0, keepdims=True)
            hit = iota == idx
        else:
            hit = S == mx
        rank = jnp.where(hit, float(r), rank)
        S = jnp.where(hit, -jnp.inf, S)
        vals.append(mx)
    return vals, rank


def _count_excess(flags):
    return jnp.abs(jnp.sum(flags, axis=0, keepdims=True) - float(PEER_TOPK))


def _route_tables(q, sk1_ref, sk2_ref, r2_ref, e2_ref, lim_ref, e1_ref, exact):
    N = q.shape[0]
    row8 = lax.broadcasted_iota(jnp.int32, (8, N), 0)
    rowc = lax.broadcasted_iota(jnp.int32, (_CAND_ROWS, N), 0)
    mid = rowc - 16
    flat = jnp.where(rowc < 16, rowc,
                     jnp.where(rowc < _CAND_ROWS - 8,
                               PEER_TOPK * ((mid >> 3) + 1) + (mid & 7),
                               PEER_TOPK * (rowc - (_CAND_ROWS - 16))))
    excess = jnp.zeros((1, N), F32)
    for h in range(PEER_HEADS):
        c0 = h * PEER_KEY_DIM
        s1 = lax.dot_general(sk1_ref[...], q[:, c0:c0 + PEER_HALF], NT_DIMS, preferred_element_type=F32)
        s2 = lax.dot_general(sk2_ref[...], q[:, c0 + PEER_HALF:c0 + PEER_KEY_DIM], NT_DIMS,
                             preferred_element_type=F32)
        t1, r1 = _extract_top16(s1, exact)
        t2, r2 = _extract_top16(s2, exact)
        t2_16 = jnp.concatenate(t2, axis=0)
        t2_8 = t2_16[:8]
        blocks = [t1[0] + t2_16, t1[1] + t2_8]
        for a in range(2, 8):
            blocks.append(jnp.where(row8 < _CAND_NB[a], t1[a] + t2_8, -jnp.inf))
        blocks.append(jnp.concatenate(t1[8:], axis=0) + t2[0])
        cand = jnp.concatenate(blocks, axis=0)
        sel = jnp.zeros(cand.shape, F32)
        work = cand
        for _ in range(PEER_TOPK):
            mx = jnp.max(work, axis=0, keepdims=True)
            if exact:
                idx = jnp.min(jnp.where(work == mx, flat, PEER_TOPK * PEER_TOPK), axis=0, keepdims=True)
                hit = flat == idx
            else:
                hit = work == mx
            sel = jnp.where(hit, 1.0, sel)
            work = jnp.where(hit, -jnp.inf, work)
        if not exact:
            excess = jnp.maximum(excess, _count_excess(jnp.where(r1 < float(PEER_TOPK), 1.0, 0.0)))
            excess = jnp.maximum(excess, _count_excess(jnp.where(r2 < float(PEER_TOPK), 1.0, 0.0)))
            excess = jnp.maximum(excess, _count_excess(sel))
        z = jnp.sum(jnp.where(sel > 0.0, jnp.exp(cand - cand[0:1]), 0.0), axis=0, keepdims=True)
        counts = [jnp.sum(sel[0:16], axis=0, keepdims=True)]
        for a in range(1, 8):
            counts.append(jnp.sum(sel[8 + 8 * a:16 + 8 * a], axis=0, keepdims=True))
        for a in range(8, PEER_TOPK):
            counts.append(sel[_CAND_ROWS - 16 + a:_CAND_ROWS - 15 + a])
        lim = jnp.zeros(r1.shape, F32)
        for a in range(PEER_TOPK):
            lim = jnp.where(r1 == float(a), counts[a], lim)
        r2_ref[h] = pltpu.bitcast(r2.astype(BF16), jnp.uint32)
        e2_ref[h] = pltpu.bitcast(jnp.exp(s2 - t2[0]).astype(BF16), jnp.uint32)
        lim_ref[h] = lim
        e1_ref[h] = jnp.exp(s1 - t1[0]) / z
    return excess


def _route_kernel(h2_ref, wq_ref, sk1_ref, sk2_ref, r2_ref, e2_ref, lim_ref, e1_ref):
    h2 = pltpu.bitcast(h2_ref[...], BF16)
    q = jnp.dot(h2, wq_ref[...], preferred_element_type=F32).astype(BF16)
    outs = (r2_ref, e2_ref, lim_ref, e1_ref)
    excess = _route_tables(q, sk1_ref, sk2_ref, *outs, exact=False)

    @pl.when(jnp.max(excess) > 0.0)
    def _():
        _route_tables(q, sk1_ref, sk2_ref, *outs, exact=True)


def _route(h2, wq, sk1, sk2):
    T = 2 * h2.shape[0]
    TM = TM_ROUTE
    const = lambda i: (0, 0)
    tab = pl.BlockSpec((PEER_HEADS, N_KEYS, TM), lambda i: (0, 0, i))
    tab_packed = pl.BlockSpec((PEER_HEADS, N_KEYS // 2, TM), lambda i: (0, 0, i))
    tab_shape = jax.ShapeDtypeStruct((PEER_HEADS, N_KEYS, T), F32)
    tab_packed_shape = jax.ShapeDtypeStruct((PEER_HEADS, N_KEYS // 2, T), jnp.uint32)
    return pl.pallas_call(
        _route_kernel,
        grid=(T // TM,),
        in_specs=[pl.BlockSpec((TM // 2, D_MODEL), lambda i: (i, 0)),
                  pl.BlockSpec((D_MODEL, PEER_HEADS * PEER_KEY_DIM), const),
                  pl.BlockSpec((N_KEYS, PEER_HALF), const), pl.BlockSpec((N_KEYS, PEER_HALF), const)],
        out_specs=[tab_packed, tab_packed, tab, tab],
        out_shape=[tab_packed_shape, tab_packed_shape, tab_shape, tab_shape],
        compiler_params=pltpu.CompilerParams(dimension_semantics=("arbitrary",),
                                             vmem_limit_bytes=VMEM_LIMIT),
        name="route",
    )(h2, wq, sk1, sk2)


def _peer_kernel(h2_ref, x1_ref, u_ref, vt_ref, r2_ref, e2_ref, lim_ref, e1_ref, nfw_ref,
                 y_ref, acc_ref, act_ref, p_ref):
    e = pl.program_id(1)
    per_sub = PEER_SUB // N_KEYS

    @pl.when(e == 0)
    def _():
        acc_ref[...] = jnp.zeros(acc_ref.shape, F32)

    h2 = pltpu.bitcast(h2_ref[...], BF16)
    for s in range(PEER_NSUB):
        u = pltpu.bitcast(u_ref[s * PEER_SUB // 2:(s + 1) * PEER_SUB // 2, :], BF16)
        act_ref[s * PEER_SUB:(s + 1) * PEER_SUB, :] = lax.dot_general(
            u, h2, NT_DIMS, preferred_element_type=F32)

    n_rb = N_KEYS // PEER_RB
    for jb in range(PEER_NSUB * per_sub):
        j = e * (PEER_NSUB * per_sub) + jb
        for lh in range(h2.shape[0] // LANES):
            cols = slice(lh * LANES, (lh + 1) * LANES)
            g = [None] * n_rb
            zero = jnp.zeros((PEER_RB, LANES), BF16)
            for h in range(PEER_HEADS):
                lim = jnp.broadcast_to(lim_ref[h, pl.ds(j, 1), :][:, cols], (PEER_RB, LANES)).astype(BF16)
                e1 = jnp.broadcast_to(e1_ref[h, pl.ds(j, 1), :][:, cols], (PEER_RB, LANES)).astype(BF16)
                for rb in range(n_rb):
                    words = slice(rb * PEER_RB // 2, (rb + 1) * PEER_RB // 2)
                    r2 = pltpu.bitcast(r2_ref[h, words, cols], BF16)
                    e2 = pltpu.bitcast(e2_ref[h, words, cols], BF16)
                    t = jnp.where(r2 < lim, e2, zero) * e1
                    g[rb] = t if g[rb] is None else g[rb] + t
            for rb in range(n_rb):
                arows = slice(jb * N_KEYS + rb * PEER_RB, jb * N_KEYS + (rb + 1) * PEER_RB)
                a = act_ref[arows, cols]
                ga = 0.5 * a * (1.0 + lax.erf(a * np.float32(np.sqrt(0.5))))
                p_ref[arows, cols] = g[rb] * ga.astype(BF16)

    acc_ref[...] += jnp.dot(pltpu.bitcast(vt_ref[...], BF16), p_ref[...],
                            preferred_element_type=F32)

    @pl.when(e == pl.num_programs(1) - 1)
    def _():
        x = x1_ref[...] + acc_ref[...].T
        y_ref[...] = _rms(x, nfw_ref[...])


def _peer(h2, x1, u, vt, r2, e2, lim, e1, nfw):
    T = x1.shape[0]
    TM = TM_PEER
    ET = PEER_SUB * PEER_NSUB
    n_e = N_EXPERTS // ET
    tok = lambda i, e: (i, 0)
    tab = pl.BlockSpec((PEER_HEADS, N_KEYS, TM), lambda i, e: (0, 0, i))
    tab_packed = pl.BlockSpec((PEER_HEADS, N_KEYS // 2, TM), lambda i, e: (0, 0, i))
    return pl.pallas_call(
        _peer_kernel,
        grid=(T // TM, n_e),
        in_specs=[pl.BlockSpec((TM // 2, D_MODEL), tok), pl.BlockSpec((TM, D_MODEL), tok),
                  pl.BlockSpec((ET // 2, D_MODEL), lambda i, e: (e, 0)),
                  pl.BlockSpec((D_MODEL // 2, ET), lambda i, e: (0, e)),
                  tab_packed, tab_packed, tab, tab,
                  pl.BlockSpec((1, D_MODEL), lambda i, e: (0, 0))],
        out_specs=pl.BlockSpec((TM, D_MODEL), tok),
        out_shape=jax.ShapeDtypeStruct((T, D_MODEL), F32),
        scratch_shapes=[pltpu.VMEM((D_MODEL, TM), F32), pltpu.VMEM((ET, TM), F32), pltpu.VMEM((ET, TM), BF16)],
        compiler_params=pltpu.CompilerParams(dimension_semantics=("arbitrary", "arbitrary"),
                                             vmem_limit_bytes=VMEM_LIMIT),
        name="peer",
    )(h2, x1, u, vt, r2, e2, lim, e1, nfw)


def _rope_tables(pos):
    inv = ROPE_THETA ** (-jnp.arange(ROPE_HALF, dtype=F32) * 2.0 / ROPE_DIM)
    ang = pos.astype(F32)[:, None] * inv[None, :]
    cos, sin = jnp.cos(ang), jnp.sin(ang)
    n = pos.shape[0]
    rest = SWA_HEAD_DIM - ROPE_DIM
    zh = jnp.zeros((n, ROPE_HALF), F32)
    cos_h = jnp.concatenate([cos, cos, jnp.ones((n, rest), F32)], axis=1)
    sina_h = jnp.concatenate([-sin, zh, jnp.zeros((n, rest), F32)], axis=1)
    sinb_h = jnp.concatenate([zh, sin, jnp.zeros((n, rest), F32)], axis=1)
    rep = LANES // SWA_HEAD_DIM
    return tuple(jnp.tile(t, (1, rep)) for t in (cos_h, sina_h, sinb_h))


def _pack_row_pairs(x):
    r, c = x.shape
    return lax.bitcast_convert_type(jnp.swapaxes(x.reshape(r // 2, 2, c), 1, 2), jnp.uint32)


def _layer_tokens(x2d, tables, tab_map, W):
    return _inproj(x2d, W["norm_mix"], W["w_in"], W["bias"], *tables, tab_map)


def _ffn(hm, hs, x2d, W):
    x1, h2 = _outproj(hm, hs, x2d, W["w_out"], W["norm_ffn"])
    r2, e2, lim, e1 = _route(h2, W["w_q"], W["sk1"], W["sk2"])
    return _peer(h2, x1, W["u"], W["vt"], r2, e2, lim, e1, W["norm_final"])


def kernel(x_prompt, x_sample, cache_swa_k, cache_swa_v, state_mlstm_c, state_mlstm_n, state_mlstm_m,
           norm_mix_w, w_in, mlstm_if_bias, mlstm_norm_w, swa_sinks, w_out, norm_ffn_w,
           peer_w_q, peer_sub_keys_1, peer_sub_keys_2, peer_u, peer_v, norm_final_w):
    B, S, _ = x_prompt.shape
    DB, DS, _ = x_sample.shape
    l = 0
    wi = w_in[l]
    s_q = 4 * ML_WIDTH + 2 * ML_HEADS
    w_perm = jnp.concatenate(
        [wi[:, :4 * ML_WIDTH], wi[:, s_q:], wi[:, 4 * ML_WIDTH:s_q],
         jnp.zeros((D_MODEL, LANES - 2 * ML_HEADS), F32)], axis=1).astype(BF16)
    bias_pad = jnp.concatenate([mlstm_if_bias[l], jnp.zeros((LANES - 2 * ML_HEADS,), F32)])[None, :]
    W = {
        "norm_mix": norm_mix_w[l][None, :],
        "w_in": w_perm,
        "bias": bias_pad,
        "w_out": w_out[l].astype(BF16),
        "norm_ffn": norm_ffn_w[l][None, :],
        "w_q": peer_w_q[l].astype(BF16),
        "sk1": peer_sub_keys_1[l].astype(BF16),
        "sk2": peer_sub_keys_2[l].astype(BF16),
        "u": _pack_row_pairs(peer_u[l].astype(BF16)),
        "vt": _pack_row_pairs(peer_v[l].astype(BF16).T),
        "norm_final": norm_final_w[None, :],
    }
    ml_nw = mlstm_norm_w[l][None, :]
    sinks = swa_sinks[l][None, :]

    xp = x_prompt.reshape(B * S, D_MODEL)
    tiles_per_seq = S // TM_INPROJ
    tabs_p = _rope_tables(jnp.arange(S, dtype=jnp.int32))
    qm, km, vm, og, sq, sk, sv, gc = _layer_tokens(xp, tabs_p, lambda i: (i % tiles_per_seq, 0), W)
    r3 = lambda a: a.reshape(B, S, a.shape[-1])
    zc = jnp.zeros((B, ML_HEADS, ML_HEAD_DIM, ML_HEAD_DIM), F32)
    zn = jnp.zeros((B, ML_HEADS, ML_HEAD_DIM), F32)
    zm = jnp.zeros((B, 1, LANES), F32)
    hm_p, c_p, n_p, m_p = _mlstm(r3(qm), r3(km), r3(vm), r3(og), r3(gc), zc, zn, zm, ml_nw, L_in=ML_CHUNK)
    sk3, sv3 = r3(sk), r3(sv)
    hs_p = _swa_prompt(r3(sq), sk3, sv3, sinks)
    y_p = _ffn(hm_p.reshape(B * S, ML_WIDTH), hs_p.reshape(B * S, SWA_WIDTH), xp, W)
    kv_shape = (1, B, WINDOW, SWA_KV_HEADS, SWA_HEAD_DIM)
    k_win_p = sk3[:, S - WINDOW:].reshape(kv_shape)
    v_win_p = sv3[:, S - WINDOW:].reshape(kv_shape)

    xs = x_sample.reshape(DB * DS, D_MODEL)
    pos_s = PAST_LEN + jnp.arange(DS, dtype=jnp.int32)
    tabs_s = tuple(jnp.tile(t, (DB, 1)) for t in _rope_tables(pos_s))
    qm, km, vm, og, sq, sk, sv, gc = _layer_tokens(xs, tabs_s, lambda i: (i, 0), W)
    r3s = lambda a: a.reshape(DB, DS, a.shape[-1])
    m0 = jnp.concatenate([state_mlstm_m[l], jnp.zeros((DB, LANES - ML_HEADS), F32)], axis=1)[:, None, :]
    hm_s, c_s, n_s, m_s = _mlstm(r3s(qm), r3s(km), r3s(vm), r3s(og), r3s(gc),
                                 state_mlstm_c[l], state_mlstm_n[l], m0, ml_nw, L_in=DS)
    kc = cache_swa_k[l].reshape(DB, WINDOW, SWA_KV_WIDTH)
    vc = cache_swa_v[l].reshape(DB, WINDOW, SWA_KV_WIDTH)
    hs_s, k_win_s, v_win_s = _swa_sample(r3s(sq), r3s(sk), r3s(sv), kc, vc, sinks)
    y_s = _ffn(hm_s.reshape(DB * DS, ML_WIDTH), hs_s.reshape(DB * DS, SWA_WIDTH), xs, W)
    kv_shape_s = (1, DB, WINDOW, SWA_KV_HEADS, SWA_HEAD_DIM)

    return (y_p.reshape(B, S, D_MODEL), y_s.reshape(DB, DS, D_MODEL),
            k_win_p, v_win_p, c_p[None], n_p[None], m_p[None, :, 0, :ML_HEADS],
            k_win_s.reshape(kv_shape_s), v_win_s.reshape(kv_shape_s),
            c_s[None], n_s[None], m_s[None, :, 0, :ML_HEADS])
```

```python
import functools

import jax
import jax.numpy as jnp
import numpy as np
from jax import lax
from jax.experimental import pallas as pl
from jax.experimental.pallas import tpu as pltpu

F32 = jnp.float32
BF16 = jnp.bfloat16

D_MODEL = 1024
SEQ = 8192
DEC_SEQ = 32
PAST_LEN = 4096
NORM_EPS = 1e-6
ML_HEADS = 4
ML_HEAD_DIM = 128
ML_WIDTH = ML_HEADS * ML_HEAD_DIM
SWA_HEADS = 8
SWA_KV_HEADS = 2
SWA_GROUP = SWA_HEADS // SWA_KV_HEADS
SWA_HEAD_DIM = 64
SWA_WIDTH = SWA_HEADS * SWA_HEAD_DIM
SWA_KV_WIDTH = SWA_KV_HEADS * SWA_HEAD_DIM
WINDOW = 128
SWA_CHUNK = 64
ROPE_THETA = 500000.0
ROPE_DIM = SWA_HEAD_DIM // 4
ROPE_HALF = ROPE_DIM // 2
PEER_HEADS = 8
N_KEYS = 128
N_EXPERTS = N_KEYS * N_KEYS
PEER_TOPK = 16
PEER_KEY_DIM = 256
PEER_HALF = PEER_KEY_DIM // 2

LANES = 128
VMEM_LIMIT = 52 * 1024 * 1024

COL_MQ, COL_MK, COL_MV, COL_MO = 0, ML_WIDTH, 2 * ML_WIDTH, 3 * ML_WIDTH
COL_SQ = 4 * ML_WIDTH
COL_SK = COL_SQ + SWA_WIDTH
COL_SV = COL_SK + SWA_KV_WIDTH
COL_G = COL_SV + SWA_KV_WIDTH
IN_COLS_PAD = COL_G + LANES

TM_INPROJ = 256
ML_CHUNK = 128
SWA_TQ = 256
TM_OUTPROJ = 256
TM_ROUTE = 128
TM_PEER = 512
PEER_SUB = 512
PEER_NSUB = 4
PEER_RB = 16
PACK_ROWS = 512

NT_DIMS = (((1,), (1,)), ((), ()))
TN_DIMS = (((0,), (0,)), ((), ()))


def _rms(x, w):
    return x * lax.rsqrt(jnp.mean(x * x, axis=-1, keepdims=True) + NORM_EPS) * w


def _inproj_kernel(x_ref, nw_ref, w_ref, bias_ref, cos_ref, sina_ref, sinb_ref,
                   qm_ref, km_ref, vm_ref, og_ref, sq_ref, sk_ref, sv_ref, gc_ref):
    h = _rms(x_ref[...], nw_ref[...])
    proj = jnp.dot(h.astype(BF16), w_ref[...], preferred_element_type=F32)
    qm_ref[...] = proj[:, COL_MQ:COL_MQ + ML_WIDTH]
    km_ref[...] = proj[:, COL_MK:COL_MK + ML_WIDTH] * (ML_HEAD_DIM ** -0.5)
    vm_ref[...] = proj[:, COL_MV:COL_MV + ML_WIDTH]
    og_ref[...] = jax.nn.sigmoid(proj[:, COL_MO:COL_MO + ML_WIDTH])
    cosf, sina, sinb = cos_ref[...], sina_ref[...], sinb_ref[...]

    def rope(xc):
        return (xc * cosf + pltpu.roll(xc, LANES - ROPE_HALF, 1) * sina
                + pltpu.roll(xc, ROPE_HALF, 1) * sinb)

    for j in range(SWA_WIDTH // LANES):
        sq_ref[:, j * LANES:(j + 1) * LANES] = rope(proj[:, COL_SQ + j * LANES:COL_SQ + (j + 1) * LANES])
    sk_ref[...] = rope(proj[:, COL_SK:COL_SK + LANES])
    sv_ref[...] = proj[:, COL_SV:COL_SV + LANES]
    g = proj[:, COL_G:COL_G + LANES] + bias_ref[...]
    lane = lax.broadcasted_iota(jnp.int32, g.shape, 1)
    gc_ref[...] = jnp.where(lane < ML_HEADS, g, jax.nn.log_sigmoid(g))


def _inproj(x2d, nw, w_perm, bias_pad, cos_t, sina_t, sinb_t, tab_map):
    T = x2d.shape[0]
    TM = TM_INPROJ
    row = lambda i: (i, 0)
    const = lambda i: (0, 0)
    f = lambda n: jax.ShapeDtypeStruct((T, n), F32)
    return pl.pallas_call(
        _inproj_kernel,
        grid=(T // TM,),
        in_specs=[pl.BlockSpec((TM, D_MODEL), row),
                  pl.BlockSpec((1, D_MODEL), const),
                  pl.BlockSpec((D_MODEL, IN_COLS_PAD), const),
                  pl.BlockSpec((1, LANES), const),
                  pl.BlockSpec((TM, LANES), tab_map),
                  pl.BlockSpec((TM, LANES), tab_map),
                  pl.BlockSpec((TM, LANES), tab_map)],
        out_specs=[pl.BlockSpec((TM, ML_WIDTH), row)] * 4
                  + [pl.BlockSpec((TM, SWA_WIDTH), row),
                     pl.BlockSpec((TM, LANES), row), pl.BlockSpec((TM, LANES), row),
                     pl.BlockSpec((TM, LANES), row)],
        out_shape=[f(ML_WIDTH)] * 4 + [f(SWA_WIDTH), f(LANES), f(LANES), f(LANES)],
        compiler_params=pltpu.CompilerParams(dimension_semantics=("arbitrary",),
                                             vmem_limit_bytes=VMEM_LIMIT),
        name="inproj",
    )(x2d, nw, w_perm, bias_pad, cos_t, sina_t, sinb_t)


def _mlstm_kernel(q_ref, k_ref, v_ref, og_ref, gc_ref, c0_ref, n0_ref, m0_ref, nw_ref,
                  hm_ref, c_ref, n_ref, m_ref, *, BB, L_in):
    LP = ML_CHUNK

    @pl.when(pl.program_id(1) == 0)
    def _():
        c_ref[...] = c0_ref[...]
        n_ref[...] = n0_ref[...]
        m_ref[...] = m0_ref[...]

    row = lax.broadcasted_iota(jnp.int32, (LP, LP), 0)
    col = lax.broadcasted_iota(jnp.int32, (LP, LP), 1)
    causal = row >= col
    tri = causal.astype(F32)
    tri_t = (row <= col).astype(F32)
    lane = lax.broadcasted_iota(jnp.int32, (LP, LANES), 1)
    lane1 = lax.broadcasted_iota(jnp.int32, (1, LANES), 1)

    def pad_rows(a):
        if L_in == LP:
            return a
        return jnp.concatenate([a, jnp.zeros((LP - L_in, a.shape[1]), a.dtype)], axis=0)

    for b in range(BB):
        gc = gc_ref[b]
        if L_in < LP:
            lane_pad = lax.broadcasted_iota(jnp.int32, (LP - L_in, LANES), 1)
            fill = jnp.where(lane_pad < ML_HEADS, -jnp.inf, 0.0).astype(F32)
            gc = jnp.concatenate([gc, fill], axis=0)
        gr = gc.T
        lf_c = jnp.where(lane >= ML_HEADS, gc, 0.0)
        b_c = jnp.dot(tri, lf_c, precision=lax.Precision.HIGHEST, preferred_element_type=F32)
        rowi = lax.broadcasted_iota(jnp.int32, (8, LP), 0)
        lf_r = jnp.where(rowi >= ML_HEADS, gr[:8], 0.0)
        b_r = jnp.dot(lf_r, tri_t, precision=lax.Precision.HIGHEST, preferred_element_type=F32)
        m_all = m_ref[b]
        m_out = m_all
        q_all = pad_rows(q_ref[b])
        k_all = pad_rows(k_ref[b])
        v_all = pad_rows(v_ref[b])
        og_all = og_ref[b]
        for h in range(ML_HEADS):
            sl = slice(h * ML_HEAD_DIM, (h + 1) * ML_HEAD_DIM)
            q, k, v = q_all[:, sl], k_all[:, sl], v_all[:, sl]
            qb, kb = q.astype(BF16), k.astype(BF16)
            ig_c = gc[:, h:h + 1]
            b_ch = b_c[:, ML_HEADS + h:ML_HEADS + h + 1]
            ig_r = gr[h:h + 1, :]
            b_rh = b_r[ML_HEADS + h:ML_HEADS + h + 1, :]
            m_prev = m_all[:, h:h + 1]
            logd = jnp.where(causal, b_ch - b_rh + ig_r, -jnp.inf)
            m_inter = b_ch + m_prev
            m_t = jnp.maximum(m_inter, jnp.max(logd, axis=1, keepdims=True))
            dmat = jnp.exp(logd - m_t)
            w_int = jnp.exp(m_inter - m_t)
            s = lax.dot_general(qb, kb, NT_DIMS, preferred_element_type=F32)
            qk = s * dmat
            c_old = c_ref[b, h]
            n_old = n_ref[b, h:h + 1, :]
            num = (w_int * lax.dot_general(qb, c_old.astype(BF16), NT_DIMS, preferred_element_type=F32)
                   + jnp.dot(qk.astype(BF16), v.astype(BF16), preferred_element_type=F32))
            den = (w_int * jnp.sum(q * n_old, axis=1, keepdims=True)
                   + jnp.sum(qk, axis=1, keepdims=True))
            hh = num / jnp.maximum(jnp.abs(den), jnp.exp(-m_t))
            b_last = b_ch[LP - 1:LP, :]
            logw = b_last - b_ch + ig_c
            m_new = jnp.maximum(b_last + m_prev, jnp.max(logw, axis=0, keepdims=True))
            w_c = jnp.exp(logw - m_new)
            decay = jnp.exp(b_last + m_prev - m_new)
            c_ref[b, h] = decay * c_old + lax.dot_general(
                (v * w_c).astype(BF16), kb, TN_DIMS, preferred_element_type=F32)
            n_ref[b, h:h + 1, :] = decay * n_old + jnp.sum(w_c * k, axis=0, keepdims=True)
            m_out = jnp.where(lane1 == h, m_new, m_out)
            y = _rms(hh, nw_ref[:, sl])
            hm_ref[b, :, sl] = (og_all[:, sl] * y[:L_in]).astype(BF16)
        m_ref[b] = m_out


def _mlstm(qm, km, vm, og, gc, c0, n0, m0, nw, *, L_in):
    B, T, _ = qm.shape
    BB = 2
    nchunks = T // L_in
    tok = lambda n: pl.BlockSpec((BB, L_in, n), lambda g, c: (g, c, 0))
    st_c = pl.BlockSpec((BB, ML_HEADS, ML_HEAD_DIM, ML_HEAD_DIM), lambda g, c: (g, 0, 0, 0))
    st_n = pl.BlockSpec((BB, ML_HEADS, ML_HEAD_DIM), lambda g, c: (g, 0, 0))
    st_m = pl.BlockSpec((BB, 1, LANES), lambda g, c: (g, 0, 0))
    return pl.pallas_call(
        functools.partial(_mlstm_kernel, BB=BB, L_in=L_in),
        grid=(B // BB, nchunks),
        in_specs=[tok(ML_WIDTH), tok(ML_WIDTH), tok(ML_WIDTH), tok(ML_WIDTH), tok(LANES),
                  st_c, st_n, st_m, pl.BlockSpec((1, ML_WIDTH), lambda g, c: (0, 0))],
        out_specs=[tok(ML_WIDTH), st_c, st_n, st_m],
        out_shape=[jax.ShapeDtypeStruct((B, T, ML_WIDTH), BF16),
                   jax.ShapeDtypeStruct(c0.shape, F32),
                   jax.ShapeDtypeStruct(n0.shape, F32),
                   jax.ShapeDtypeStruct(m0.shape, F32)],
        compiler_params=pltpu.CompilerParams(dimension_semantics=("arbitrary", "arbitrary"),
                                             vmem_limit_bytes=VMEM_LIMIT),
        name="mlstm",
    )(qm, km, vm, og, gc, c0, n0, m0, nw)


def _sink_attention(q4, kw, vw, sink_col, valid):
    s = lax.dot_general(q4, kw, NT_DIMS, preferred_element_type=F32) * (SWA_HEAD_DIM ** -0.5)
    if valid is not None:
        s = jnp.where(valid, s, -jnp.inf)
    mx = jnp.maximum(jnp.max(s, axis=1, keepdims=True), sink_col)
    p = jnp.exp(s - mx)
    den = jnp.sum(p, axis=1, keepdims=True) + jnp.exp(sink_col - mx)
    return jnp.dot(p.astype(BF16), vw, preferred_element_type=F32) / den


def _sink_column(sink_ref, g, rows):
    return jnp.concatenate(
        [jnp.broadcast_to(sink_ref[:, g * SWA_GROUP + j:g * SWA_GROUP + j + 1], (rows, 1))
         for j in range(SWA_GROUP)], axis=0)


def _swa_prompt_kernel(q_ref, kc_ref, kp_ref, vc_ref, vp_ref, sink_ref, o_ref):
    TQ = SWA_TQ
    i = pl.program_id(1)
    q = q_ref[0]
    kcat = jnp.concatenate([kp_ref[0, TQ - WINDOW:, :], kc_ref[0]], axis=0).astype(BF16)
    vcat = jnp.concatenate([vp_ref[0, TQ - WINDOW:, :], vc_ref[0]], axis=0).astype(BF16)
    span = WINDOW + SWA_CHUNK
    key_iota = lax.broadcasted_iota(jnp.int32, (1, span), 1)
    for g in range(SWA_KV_HEADS):
        gs = slice(g * SWA_HEAD_DIM, (g + 1) * SWA_HEAD_DIM)
        kg, vg = kcat[:, gs], vcat[:, gs]
        sink_col = _sink_column(sink_ref, g, SWA_CHUNK)
        for c in range(TQ // SWA_CHUNK):
            rs = slice(c * SWA_CHUNK, (c + 1) * SWA_CHUNK)
            q4 = jnp.concatenate(
                [q[rs, (g * SWA_GROUP + j) * SWA_HEAD_DIM:(g * SWA_GROUP + j + 1) * SWA_HEAD_DIM]
                 for j in range(SWA_GROUP)], axis=0).astype(BF16)
            ws = slice(c * SWA_CHUNK, c * SWA_CHUNK + span)
            valid = (i * TQ + c * SWA_CHUNK - WINDOW + key_iota) >= 0
            o = _sink_attention(q4, kg[ws], vg[ws], sink_col, valid)
            for j in range(SWA_GROUP):
                hd = (g * SWA_GROUP + j) * SWA_HEAD_DIM
                o_ref[0, rs, hd:hd + SWA_HEAD_DIM] = o[j * SWA_CHUNK:(j + 1) * SWA_CHUNK].astype(BF16)


def _swa_prompt(sq, sk, sv, sinks):
    B, T, _ = sq.shape
    TQ = SWA_TQ
    cur = lambda b, i: (b, i, 0)
    prev = lambda b, i: (b, jnp.maximum(i - 1, 0), 0)
    return pl.pallas_call(
        _swa_prompt_kernel,
        grid=(B, T // TQ),
        in_specs=[pl.BlockSpec((1, TQ, SWA_WIDTH), cur),
                  pl.BlockSpec((1, TQ, SWA_KV_WIDTH), cur), pl.BlockSpec((1, TQ, SWA_KV_WIDTH), prev),
                  pl.BlockSpec((1, TQ, SWA_KV_WIDTH), cur), pl.BlockSpec((1, TQ, SWA_KV_WIDTH), prev),
                  pl.BlockSpec((1, SWA_HEADS), lambda b, i: (0, 0))],
        out_specs=pl.BlockSpec((1, TQ, SWA_WIDTH), cur),
        out_shape=jax.ShapeDtypeStruct((B, T, SWA_WIDTH), BF16),
        compiler_params=pltpu.CompilerParams(dimension_semantics=("arbitrary", "arbitrary"),
                                             vmem_limit_bytes=VMEM_LIMIT),
        name="swa_prompt",
    )(sq, sk, sk, sv, sv, sinks)


def _swa_sample_kernel(q_ref, kn_ref, vn_ref, kc_ref, vc_ref, sink_ref, o_ref, kw_ref, vw_ref):
    T = DEC_SEQ
    q = q_ref[0]
    k_all = jnp.concatenate([kc_ref[0], kn_ref[0]], axis=0)
    v_all = jnp.concatenate([vc_ref[0], vn_ref[0]], axis=0)
    kw_ref[0] = k_all[T:]
    vw_ref[0] = v_all[T:]
    kb, vb = k_all.astype(BF16), v_all.astype(BF16)
    for g in range(SWA_KV_HEADS):
        gs = slice(g * SWA_HEAD_DIM, (g + 1) * SWA_HEAD_DIM)
        q4 = jnp.concatenate(
            [q[:, (g * SWA_GROUP + j) * SWA_HEAD_DIM:(g * SWA_GROUP + j + 1) * SWA_HEAD_DIM]
             for j in range(SWA_GROUP)], axis=0).astype(BF16)
        o = _sink_attention(q4, kb[:, gs], vb[:, gs], _sink_column(sink_ref, g, T), None)
        for j in range(SWA_GROUP):
            hd = (g * SWA_GROUP + j) * SWA_HEAD_DIM
            o_ref[0, :, hd:hd + SWA_HEAD_DIM] = o[j * T:(j + 1) * T].astype(BF16)


def _swa_sample(sq, sk, sv, k_cache, v_cache, sinks):
    B, T, _ = sq.shape
    b3 = lambda b: (b, 0, 0)
    return pl.pallas_call(
        _swa_sample_kernel,
        grid=(B,),
        in_specs=[pl.BlockSpec((1, T, SWA_WIDTH), b3),
                  pl.BlockSpec((1, T, SWA_KV_WIDTH), b3), pl.BlockSpec((1, T, SWA_KV_WIDTH), b3),
                  pl.BlockSpec((1, WINDOW, SWA_KV_WIDTH), b3), pl.BlockSpec((1, WINDOW, SWA_KV_WIDTH), b3),
                  pl.BlockSpec((1, SWA_HEADS), lambda b: (0, 0))],
        out_specs=[pl.BlockSpec((1, T, SWA_WIDTH), b3),
                   pl.BlockSpec((1, WINDOW, SWA_KV_WIDTH), b3), pl.BlockSpec((1, WINDOW, SWA_KV_WIDTH), b3)],
        out_shape=[jax.ShapeDtypeStruct((B, T, SWA_WIDTH), BF16),
                   jax.ShapeDtypeStruct((B, WINDOW, SWA_KV_WIDTH), F32),
                   jax.ShapeDtypeStruct((B, WINDOW, SWA_KV_WIDTH), F32)],
        compiler_params=pltpu.CompilerParams(dimension_semantics=("arbitrary",),
                                             vmem_limit_bytes=VMEM_LIMIT),
        name="swa_sample",
    )(sq, sk, sv, k_cache, v_cache, sinks)


def _outproj_kernel(hm_ref, hs_ref, x_ref, wo_ref, nw_ref, x1_ref, h2_ref):
    mix = (jnp.dot(hm_ref[...], wo_ref[:ML_WIDTH, :], preferred_element_type=F32)
           + jnp.dot(hs_ref[...], wo_ref[ML_WIDTH:, :], preferred_element_type=F32))
    x1 = x_ref[...] + mix
    x1_ref[...] = x1
    h2_ref[...] = pltpu.bitcast(_rms(x1, nw_ref[...]).astype(BF16), jnp.uint32)


def _outproj(hm, hs, x2d, wo, nw):
    T = x2d.shape[0]
    TM = TM_OUTPROJ
    row = lambda i: (i, 0)
    const = lambda i: (0, 0)
    return pl.pallas_call(
        _outproj_kernel,
        grid=(T // TM,),
        in_specs=[pl.BlockSpec((TM, ML_WIDTH), row), pl.BlockSpec((TM, SWA_WIDTH), row),
                  pl.BlockSpec((TM, D_MODEL), row),
                  pl.BlockSpec((D_MODEL, D_MODEL), const), pl.BlockSpec((1, D_MODEL), const)],
        out_specs=[pl.BlockSpec((TM, D_MODEL), row), pl.BlockSpec((TM // 2, D_MODEL), row)],
        out_shape=[jax.ShapeDtypeStruct((T, D_MODEL), F32),
                   jax.ShapeDtypeStruct((T // 2, D_MODEL), jnp.uint32)],
        compiler_params=pltpu.CompilerParams(dimension_semantics=("arbitrary",),
                                             vmem_limit_bytes=VMEM_LIMIT),
        name="outproj",
    )(hm, hs, x2d, wo, nw)


_CAND_NB = [PEER_TOPK // (a + 1) for a in range(PEER_TOPK)]
_CAND_ROWS = 16 + 8 * 7 + 8


def _extract_top16(S, exact):
    R = S.shape[0]
    iota = lax.broadcasted_iota(jnp.int32, S.shape, 0)
    rank = jnp.full(S.shape, float(PEER_TOPK), F32)
    vals = []
    for r in range(PEER_TOPK):
        mx = jnp.max(S, axis=0, keepdims=True)
        if exact:
            idx = jnp.min(jnp.where(S == mx, iota, R), axis=0, keepdims=True)
            hit = iota == idx
        else:
            hit = S == mx
        rank = jnp.where(hit, float(r), rank)
        S = jnp.where(hit, -jnp.inf, S)
        vals.append(mx)
    return vals, rank


def _count_excess(flags):
    return jnp.abs(jnp.sum(flags, axis=0, keepdims=True) - float(PEER_TOPK))


def _route_tables(q, sk1_ref, sk2_ref, r2_ref, e2_ref, lim_ref, e1_ref, exact):
    N = q.shape[0]
    row8 = lax.broadcasted_iota(jnp.int32, (8, N), 0)
    rowc = lax.broadcasted_iota(jnp.int32, (_CAND_ROWS, N), 0)
    mid = rowc - 16
    flat = jnp.where(rowc < 16, rowc,
                     jnp.where(rowc < _CAND_ROWS - 8,
                               PEER_TOPK * ((mid >> 3) + 1) + (mid & 7),
                               PEER_TOPK * (rowc - (_CAND_ROWS - 16))))
    excess = jnp.zeros((1, N), F32)
    for h in range(PEER_HEADS):
        c0 = h * PEER_KEY_DIM
        s1 = lax.dot_general(sk1_ref[...], q[:, c0:c0 + PEER_HALF], NT_DIMS, preferred_element_type=F32)
        s2 = lax.dot_general(sk2_ref[...], q[:, c0 + PEER_HALF:c0 + PEER_KEY_DIM], NT_DIMS,
                             preferred_element_type=F32)
        t1, r1 = _extract_top16(s1, exact)
        t2, r2 = _extract_top16(s2, exact)
        t2_16 = jnp.concatenate(t2, axis=0)
        t2_8 = t2_16[:8]
        blocks = [t1[0] + t2_16, t1[1] + t2_8]
        for a in range(2, 8):
            blocks.append(jnp.where(row8 < _CAND_NB[a], t1[a] + t2_8, -jnp.inf))
        blocks.append(jnp.concatenate(t1[8:], axis=0) + t2[0])
        cand = jnp.concatenate(blocks, axis=0)
        sel = jnp.zeros(cand.shape, F32)
        work = cand
        for _ in range(PEER_TOPK):
            mx = jnp.max(work, axis=0, keepdims=True)
            if exact:
                idx = jnp.min(jnp.where(work == mx, flat, PEER_TOPK * PEER_TOPK), axis=0, keepdims=True)
                hit = flat == idx
            else:
                hit = work == mx
            sel = jnp.where(hit, 1.0, sel)
            work = jnp.where(hit, -jnp.inf, work)
        if not exact:
            excess = jnp.maximum(excess, _count_excess(jnp.where(r1 < float(PEER_TOPK), 1.0, 0.0)))
            excess = jnp.maximum(excess, _count_excess(jnp.where(r2 < float(PEER_TOPK), 1.0, 0.0)))
            excess = jnp.maximum(excess, _count_excess(sel))
        z = jnp.sum(jnp.where(sel > 0.0, jnp.exp(cand - cand[0:1]), 0.0), axis=0, keepdims=True)
        counts = [jnp.sum(sel[0:16], axis=0, keepdims=True)]
        for a in range(1, 8):
            counts.append(jnp.sum(sel[8 + 8 * a:16 + 8 * a], axis=0, keepdims=True))
        for a in range(8, PEER_TOPK):
            counts.append(sel[_CAND_ROWS - 16 + a:_CAND_ROWS - 15 + a])
        lim = jnp.zeros(r1.shape, F32)
        for a in range(PEER_TOPK):
            lim = jnp.where(r1 == float(a), counts[a], lim)
        r2_ref[h] = pltpu.bitcast(r2.astype(BF16), jnp.uint32)
        e2_ref[h] = pltpu.bitcast(jnp.exp(s2 - t2[0]).astype(BF16), jnp.uint32)
        lim_ref[h] = lim
        e1_ref[h] = jnp.exp(s1 - t1[0]) / z
    return excess


def _route_kernel(h2_ref, wq_ref, sk1_ref, sk2_ref, r2_ref, e2_ref, lim_ref, e1_ref):
    h2 = pltpu.bitcast(h2_ref[...], BF16)
    q = jnp.dot(h2, wq_ref[...], preferred_element_type=F32).astype(BF16)
    outs = (r2_ref, e2_ref, lim_ref, e1_ref)
    excess = _route_tables(q, sk1_ref, sk2_ref, *outs, exact=False)

    @pl.when(jnp.max(excess) > 0.0)
    def _():
        _route_tables(q, sk1_ref, sk2_ref, *outs, exact=True)


def _route(h2, wq, sk1, sk2):
    T = 2 * h2.shape[0]
    TM = TM_ROUTE
    const = lambda i: (0, 0)
    tab = pl.BlockSpec((None, PEER_HEADS, N_KEYS, TM), lambda i: (i, 0, 0, 0))
    tab_packed = pl.BlockSpec((None, PEER_HEADS, N_KEYS // 2, TM), lambda i: (i, 0, 0, 0))
    tab_shape = jax.ShapeDtypeStruct((T // TM, PEER_HEADS, N_KEYS, TM), F32)
    tab_packed_shape = jax.ShapeDtypeStruct((T // TM, PEER_HEADS, N_KEYS // 2, TM), jnp.uint32)
    return pl.pallas_call(
        _route_kernel,
        grid=(T // TM,),
        in_specs=[pl.BlockSpec((TM // 2, D_MODEL), lambda i: (i, 0)),
                  pl.BlockSpec((D_MODEL, PEER_HEADS * PEER_KEY_DIM), const),
                  pl.BlockSpec((N_KEYS, PEER_HALF), const), pl.BlockSpec((N_KEYS, PEER_HALF), const)],
        out_specs=[tab_packed, tab_packed, tab, tab],
        out_shape=[tab_packed_shape, tab_packed_shape, tab_shape, tab_shape],
        compiler_params=pltpu.CompilerParams(dimension_semantics=("arbitrary",),
                                             vmem_limit_bytes=VMEM_LIMIT),
        name="route",
    )(h2, wq, sk1, sk2)


def _peer_kernel(h2_ref, x1_ref, u_ref, vt_ref, r2_ref, e2_ref, lim_ref, e1_ref, nfw_ref,
                 y_ref, acc_ref, act_ref, p_ref):
    e = pl.program_id(1)
    per_sub = PEER_SUB // N_KEYS

    @pl.when(e == 0)
    def _():
        acc_ref[...] = jnp.zeros(acc_ref.shape, F32)

    h2 = pltpu.bitcast(h2_ref[...], BF16)
    n_rb = N_KEYS // PEER_RB
    zero = jnp.zeros((PEER_RB, LANES), BF16)

    def up_proj(s):
        u = pltpu.bitcast(u_ref[s * PEER_SUB // 2:(s + 1) * PEER_SUB // 2, :], BF16)
        act_ref[s * PEER_SUB:(s + 1) * PEER_SUB, :] = lax.dot_general(
            u, h2, NT_DIMS, preferred_element_type=F32)

    def gate(s):
        for jb in range(s * per_sub, (s + 1) * per_sub):
            j = e * (PEER_NSUB * per_sub) + jb
            for lh in range(h2.shape[0] // LANES):
                cols = slice(lh * LANES, (lh + 1) * LANES)
                g = [None] * n_rb
                for h in range(PEER_HEADS):
                    lim = jnp.broadcast_to(lim_ref[lh, h, pl.ds(j, 1), :], (PEER_RB, LANES)).astype(BF16)
                    e1 = jnp.broadcast_to(e1_ref[lh, h, pl.ds(j, 1), :], (PEER_RB, LANES)).astype(BF16)
                    for rb in range(n_rb):
                        words = slice(rb * PEER_RB // 2, (rb + 1) * PEER_RB // 2)
                        r2 = pltpu.bitcast(r2_ref[lh, h, words, :], BF16)
                        e2 = pltpu.bitcast(e2_ref[lh, h, words, :], BF16)
                        t = jnp.where(r2 < lim, e2, zero) * e1
                        g[rb] = t if g[rb] is None else g[rb] + t
                for rb in range(n_rb):
                    arows = slice(jb * N_KEYS + rb * PEER_RB, jb * N_KEYS + (rb + 1) * PEER_RB)
                    a = act_ref[arows, cols]
                    ga = 0.5 * a * (1.0 + lax.erf(a * np.float32(np.sqrt(0.5))))
                    p_ref[arows, cols] = g[rb] * ga.astype(BF16)

    def down_proj(s):
        vt = pltpu.bitcast(vt_ref[:, s * PEER_SUB:(s + 1) * PEER_SUB], BF16)
        acc_ref[...] += jnp.dot(vt, p_ref[s * PEER_SUB:(s + 1) * PEER_SUB, :],
                                preferred_element_type=F32)

    up_proj(0)
    for s in range(PEER_NSUB):
        if s + 1 < PEER_NSUB:
            up_proj(s + 1)
        gate(s)
        down_proj(s)

    @pl.when(e == pl.num_programs(1) - 1)
    def _():
        x = x1_ref[...] + acc_ref[...].T
        y_ref[...] = _rms(x, nfw_ref[...])


def _peer(h2, x1, u, vt, r2, e2, lim, e1, nfw):
    T = x1.shape[0]
    TM = TM_PEER
    ET = PEER_SUB * PEER_NSUB
    n_e = N_EXPERTS // ET
    tok = lambda i, e: (i, 0)
    rt = TM // TM_ROUTE
    tab = pl.BlockSpec((rt, PEER_HEADS, N_KEYS, TM_ROUTE), lambda i, e: (i, 0, 0, 0))
    tab_packed = pl.BlockSpec((rt, PEER_HEADS, N_KEYS // 2, TM_ROUTE), lambda i, e: (i, 0, 0, 0))
    return pl.pallas_call(
        _peer_kernel,
        grid=(T // TM, n_e),
        in_specs=[pl.BlockSpec((TM // 2, D_MODEL), tok), pl.BlockSpec((TM, D_MODEL), tok),
                  pl.BlockSpec((ET // 2, D_MODEL), lambda i, e: (e, 0)),
                  pl.BlockSpec((D_MODEL // 2, ET), lambda i, e: (0, e)),
                  tab_packed, tab_packed, tab, tab,
                  pl.BlockSpec((1, D_MODEL), lambda i, e: (0, 0))],
        out_specs=pl.BlockSpec((TM, D_MODEL), tok),
        out_shape=jax.ShapeDtypeStruct((T, D_MODEL), F32),
        scratch_shapes=[pltpu.VMEM((D_MODEL, TM), F32), pltpu.VMEM((ET, TM), F32), pltpu.VMEM((ET, TM), BF16)],
        compiler_params=pltpu.CompilerParams(dimension_semantics=("arbitrary", "arbitrary"),
                                             vmem_limit_bytes=VMEM_LIMIT),
        name="peer",
    )(h2, x1, u, vt, r2, e2, lim, e1, nfw)


def _rope_tables(pos):
    inv = ROPE_THETA ** (-jnp.arange(ROPE_HALF, dtype=F32) * 2.0 / ROPE_DIM)
    ang = pos.astype(F32)[:, None] * inv[None, :]
    cos, sin = jnp.cos(ang), jnp.sin(ang)
    n = pos.shape[0]
    rest = SWA_HEAD_DIM - ROPE_DIM
    zh = jnp.zeros((n, ROPE_HALF), F32)
    cos_h = jnp.concatenate([cos, cos, jnp.ones((n, rest), F32)], axis=1)
    sina_h = jnp.concatenate([-sin, zh, jnp.zeros((n, rest), F32)], axis=1)
    sinb_h = jnp.concatenate([zh, sin, jnp.zeros((n, rest), F32)], axis=1)
    rep = LANES // SWA_HEAD_DIM
    return tuple(jnp.tile(t, (1, rep)) for t in (cos_h, sina_h, sinb_h))


def _pack_kernel(x_ref, o_ref, *, transpose):
    x = x_ref[...]
    if transpose:
        x = x.T
    o_ref[...] = pltpu.bitcast(x.astype(BF16), jnp.uint32)


def _pack_expert_table(w, *, transpose):
    n, d = w.shape
    rows = PACK_ROWS
    if transpose:
        out_spec = pl.BlockSpec((d // 2, rows), lambda i: (0, i))
        out_shape = jax.ShapeDtypeStruct((d // 2, n), jnp.uint32)
    else:
        out_spec = pl.BlockSpec((rows // 2, d), lambda i: (i, 0))
        out_shape = jax.ShapeDtypeStruct((n // 2, d), jnp.uint32)
    return pl.pallas_call(
        functools.partial(_pack_kernel, transpose=transpose),
        grid=(n // rows,),
        in_specs=[pl.BlockSpec((rows, d), lambda i: (i, 0))],
        out_specs=out_spec,
        out_shape=out_shape,
        compiler_params=pltpu.CompilerParams(dimension_semantics=("arbitrary",),
                                             vmem_limit_bytes=VMEM_LIMIT),
        name="pack_vt" if transpose else "pack_u",
    )(w)


def _layer_tokens(x2d, tables, tab_map, W):
    return _inproj(x2d, W["norm_mix"], W["w_in"], W["bias"], *tables, tab_map)


def _ffn(hm, hs, x2d, W):
    x1, h2 = _outproj(hm, hs, x2d, W["w_out"], W["norm_ffn"])
    r2, e2, lim, e1 = _route(h2, W["w_q"], W["sk1"], W["sk2"])
    return _peer(h2, x1, W["u"], W["vt"], r2, e2, lim, e1, W["norm_final"])


def kernel(x_prompt, x_sample, cache_swa_k, cache_swa_v, state_mlstm_c, state_mlstm_n, state_mlstm_m,
           norm_mix_w, w_in, mlstm_if_bias, mlstm_norm_w, swa_sinks, w_out, norm_ffn_w,
           peer_w_q, peer_sub_keys_1, peer_sub_keys_2, peer_u, peer_v, norm_final_w):
    B, S, _ = x_prompt.shape
    DB, DS, _ = x_sample.shape
    l = 0
    wi = w_in[l]
    s_q = 4 * ML_WIDTH + 2 * ML_HEADS
    w_perm = jnp.concatenate(
        [wi[:, :4 * ML_WIDTH], wi[:, s_q:], wi[:, 4 * ML_WIDTH:s_q],
         jnp.zeros((D_MODEL, LANES - 2 * ML_HEADS), F32)], axis=1).astype(BF16)
    bias_pad = jnp.concatenate([mlstm_if_bias[l], jnp.zeros((LANES - 2 * ML_HEADS,), F32)])[None, :]
    W = {
        "norm_mix": norm_mix_w[l][None, :],
        "w_in": w_perm,
        "bias": bias_pad,
        "w_out": w_out[l].astype(BF16),
        "norm_ffn": norm_ffn_w[l][None, :],
        "w_q": peer_w_q[l].astype(BF16),
        "sk1": peer_sub_keys_1[l].astype(BF16),
        "sk2": peer_sub_keys_2[l].astype(BF16),
        "u": _pack_expert_table(peer_u[l], transpose=False),
        "vt": _pack_expert_table(peer_v[l], transpose=True),
        "norm_final": norm_final_w[None, :],
    }
    ml_nw = mlstm_norm_w[l][None, :]
    sinks = swa_sinks[l][None, :]

    xp = x_prompt.reshape(B * S, D_MODEL)
    tiles_per_seq = S // TM_INPROJ
    tabs_p = _rope_tables(jnp.arange(S, dtype=jnp.int32))
    qm, km, vm, og, sq, sk, sv, gc = _layer_tokens(xp, tabs_p, lambda i: (i % tiles_per_seq, 0), W)
    r3 = lambda a: a.reshape(B, S, a.shape[-1])
    zc = jnp.zeros((B, ML_HEADS, ML_HEAD_DIM, ML_HEAD_DIM), F32)
    zn = jnp.zeros((B, ML_HEADS, ML_HEAD_DIM), F32)
    zm = jnp.zeros((B, 1, LANES), F32)
    hm_p, c_p, n_p, m_p = _mlstm(r3(qm), r3(km), r3(vm), r3(og), r3(gc), zc, zn, zm, ml_nw, L_in=ML_CHUNK)
    sk3, sv3 = r3(sk), r3(sv)
    hs_p = _swa_prompt(r3(sq), sk3, sv3, sinks)
    y_p = _ffn(hm_p.reshape(B * S, ML_WIDTH), hs_p.reshape(B * S, SWA_WIDTH), xp, W)
    kv_shape = (1, B, WINDOW, SWA_KV_HEADS, SWA_HEAD_DIM)
    k_win_p = sk3[:, S - WINDOW:].reshape(kv_shape)
    v_win_p = sv3[:, S - WINDOW:].reshape(kv_shape)

    xs = x_sample.reshape(DB * DS, D_MODEL)
    pos_s = PAST_LEN + jnp.arange(DS, dtype=jnp.int32)
    tabs_s = tuple(jnp.tile(t, (DB, 1)) for t in _rope_tables(pos_s))
    qm, km, vm, og, sq, sk, sv, gc = _layer_tokens(xs, tabs_s, lambda i: (i, 0), W)
    r3s = lambda a: a.reshape(DB, DS, a.shape[-1])
    m0 = jnp.concatenate([state_mlstm_m[l], jnp.zeros((DB, LANES - ML_HEADS), F32)], axis=1)[:, None, :]
    hm_s, c_s, n_s, m_s = _mlstm(r3s(qm), r3s(km), r3s(vm), r3s(og), r3s(gc),
                                 state_mlstm_c[l], state_mlstm_n[l], m0, ml_nw, L_in=DS)
    kc = cache_swa_k[l].reshape(DB, WINDOW, SWA_KV_WIDTH)
    vc = cache_swa_v[l].reshape(DB, WINDOW, SWA_KV_WIDTH)
    hs_s, k_win_s, v_win_s = _swa_sample(r3s(sq), r3s(sk), r3s(sv), kc, vc, sinks)
    y_s = _ffn(hm_s.reshape(DB * DS, ML_WIDTH), hs_s.reshape(DB * DS, SWA_WIDTH), xs, W)
    kv_shape_s = (1, DB, WINDOW, SWA_KV_HEADS, SWA_HEAD_DIM)

    return (y_p.reshape(B, S, D_MODEL), y_s.reshape(DB, DS, D_MODEL),
            k_win_p, v_win_p, c_p[None], n_p[None], m_p[None, :, 0, :ML_HEADS],
            k_win_s.reshape(kv_shape_s), v_win_s.reshape(kv_shape_s),
            c_s[None], n_s[None], m_s[None, :, 0, :ML_HEADS])
```

```python
import functools

import jax
import jax.numpy as jnp
import numpy as np
from jax import lax
from jax.experimental import pallas as pl
from jax.experimental.pallas import tpu as pltpu

F32 = jnp.float32
BF16 = jnp.bfloat16

D_MODEL = 1024
SEQ = 8192
DEC_SEQ = 32
PAST_LEN = 4096
NORM_EPS = 1e-6
ML_HEADS = 4
ML_HEAD_DIM = 128
ML_WIDTH = ML_HEADS * ML_HEAD_DIM
SWA_HEADS = 8
SWA_KV_HEADS = 2
SWA_GROUP = SWA_HEADS // SWA_KV_HEADS
SWA_HEAD_DIM = 64
SWA_WIDTH = SWA_HEADS * SWA_HEAD_DIM
SWA_KV_WIDTH = SWA_KV_HEADS * SWA_HEAD_DIM
WINDOW = 128
SWA_CHUNK = 64
ROPE_THETA = 500000.0
ROPE_DIM = SWA_HEAD_DIM // 4
ROPE_HALF = ROPE_DIM // 2
PEER_HEADS = 8
N_KEYS = 128
N_EXPERTS = N_KEYS * N_KEYS
PEER_TOPK = 16
PEER_KEY_DIM = 256
PEER_HALF = PEER_KEY_DIM // 2

LANES = 128
VMEM_LIMIT = 52 * 1024 * 1024

COL_MQ, COL_MK, COL_MV, COL_MO = 0, ML_WIDTH, 2 * ML_WIDTH, 3 * ML_WIDTH
COL_SQ = 4 * ML_WIDTH
COL_SK = COL_SQ + SWA_WIDTH
COL_SV = COL_SK + SWA_KV_WIDTH
COL_G = COL_SV + SWA_KV_WIDTH
IN_COLS_PAD = COL_G + LANES

TM_INPROJ = 256
ML_CHUNK = 128
SWA_TQ = 256
TM_OUTPROJ = 256
TM_ROUTE = 128
TM_PEER = 512
PEER_SUB = 512
PEER_NSUB = 2
PEER_RB = 16
PACK_ROWS = 512

NT_DIMS = (((1,), (1,)), ((), ()))
TN_DIMS = (((0,), (0,)), ((), ()))


def _rms(x, w):
    return x * lax.rsqrt(jnp.mean(x * x, axis=-1, keepdims=True) + NORM_EPS) * w


def _inproj_kernel(x_ref, nw_ref, w_ref, bias_ref, cos_ref, sina_ref, sinb_ref,
                   qm_ref, km_ref, vm_ref, og_ref, sq_ref, sk_ref, sv_ref, gc_ref):
    h = _rms(x_ref[...], nw_ref[...])
    proj = jnp.dot(h.astype(BF16), w_ref[...], preferred_element_type=F32)
    qm_ref[...] = proj[:, COL_MQ:COL_MQ + ML_WIDTH]
    km_ref[...] = proj[:, COL_MK:COL_MK + ML_WIDTH] * (ML_HEAD_DIM ** -0.5)
    vm_ref[...] = proj[:, COL_MV:COL_MV + ML_WIDTH]
    og_ref[...] = jax.nn.sigmoid(proj[:, COL_MO:COL_MO + ML_WIDTH])
    cosf, sina, sinb = cos_ref[...], sina_ref[...], sinb_ref[...]

    def rope(xc):
        return (xc * cosf + pltpu.roll(xc, LANES - ROPE_HALF, 1) * sina
                + pltpu.roll(xc, ROPE_HALF, 1) * sinb)

    for j in range(SWA_WIDTH // LANES):
        sq_ref[:, j * LANES:(j + 1) * LANES] = rope(proj[:, COL_SQ + j * LANES:COL_SQ + (j + 1) * LANES])
    sk_ref[...] = rope(proj[:, COL_SK:COL_SK + LANES])
    sv_ref[...] = proj[:, COL_SV:COL_SV + LANES]
    g = proj[:, COL_G:COL_G + LANES] + bias_ref[...]
    lane = lax.broadcasted_iota(jnp.int32, g.shape, 1)
    gc_ref[...] = jnp.where(lane < ML_HEADS, g, jax.nn.log_sigmoid(g))


def _inproj(x2d, nw, w_perm, bias_pad, cos_t, sina_t, sinb_t, tab_map):
    T = x2d.shape[0]
    TM = TM_INPROJ
    row = lambda i: (i, 0)
    const = lambda i: (0, 0)
    f = lambda n: jax.ShapeDtypeStruct((T, n), F32)
    return pl.pallas_call(
        _inproj_kernel,
        grid=(T // TM,),
        in_specs=[pl.BlockSpec((TM, D_MODEL), row),
                  pl.BlockSpec((1, D_MODEL), const),
                  pl.BlockSpec((D_MODEL, IN_COLS_PAD), const),
                  pl.BlockSpec((1, LANES), const),
                  pl.BlockSpec((TM, LANES), tab_map),
                  pl.BlockSpec((TM, LANES), tab_map),
                  pl.BlockSpec((TM, LANES), tab_map)],
        out_specs=[pl.BlockSpec((TM, ML_WIDTH), row)] * 4
                  + [pl.BlockSpec((TM, SWA_WIDTH), row),
                     pl.BlockSpec((TM, LANES), row), pl.BlockSpec((TM, LANES), row),
                     pl.BlockSpec((TM, LANES), row)],
        out_shape=[f(ML_WIDTH)] * 4 + [f(SWA_WIDTH), f(LANES), f(LANES), f(LANES)],
        compiler_params=pltpu.CompilerParams(dimension_semantics=("arbitrary",),
                                             vmem_limit_bytes=VMEM_LIMIT),
        name="inproj",
    )(x2d, nw, w_perm, bias_pad, cos_t, sina_t, sinb_t)


def _mlstm_kernel(q_ref, k_ref, v_ref, og_ref, gc_ref, c0_ref, n0_ref, m0_ref, nw_ref,
                  hm_ref, c_ref, n_ref, m_ref, *, BB, L_in):
    LP = ML_CHUNK

    @pl.when(pl.program_id(1) == 0)
    def _():
        c_ref[...] = c0_ref[...]
        n_ref[...] = n0_ref[...]
        m_ref[...] = m0_ref[...]

    row = lax.broadcasted_iota(jnp.int32, (LP, LP), 0)
    col = lax.broadcasted_iota(jnp.int32, (LP, LP), 1)
    causal = row >= col
    tri = causal.astype(F32)
    tri_t = (row <= col).astype(F32)
    lane = lax.broadcasted_iota(jnp.int32, (LP, LANES), 1)
    lane1 = lax.broadcasted_iota(jnp.int32, (1, LANES), 1)

    def pad_rows(a):
        if L_in == LP:
            return a
        return jnp.concatenate([a, jnp.zeros((LP - L_in, a.shape[1]), a.dtype)], axis=0)

    for b in range(BB):
        gc = gc_ref[b]
        if L_in < LP:
            lane_pad = lax.broadcasted_iota(jnp.int32, (LP - L_in, LANES), 1)
            fill = jnp.where(lane_pad < ML_HEADS, -jnp.inf, 0.0).astype(F32)
            gc = jnp.concatenate([gc, fill], axis=0)
        gr = gc.T
        lf_c = jnp.where(lane >= ML_HEADS, gc, 0.0)
        b_c = jnp.dot(tri, lf_c, precision=lax.Precision.HIGHEST, preferred_element_type=F32)
        rowi = lax.broadcasted_iota(jnp.int32, (8, LP), 0)
        lf_r = jnp.where(rowi >= ML_HEADS, gr[:8], 0.0)
        b_r = jnp.dot(lf_r, tri_t, precision=lax.Precision.HIGHEST, preferred_element_type=F32)
        m_all = m_ref[b]
        m_out = m_all
        q_all = pad_rows(q_ref[b])
        k_all = pad_rows(k_ref[b])
        v_all = pad_rows(v_ref[b])
        og_all = og_ref[b]
        for h in range(ML_HEADS):
            sl = slice(h * ML_HEAD_DIM, (h + 1) * ML_HEAD_DIM)
            q, k, v = q_all[:, sl], k_all[:, sl], v_all[:, sl]
            qb, kb = q.astype(BF16), k.astype(BF16)
            ig_c = gc[:, h:h + 1]
            b_ch = b_c[:, ML_HEADS + h:ML_HEADS + h + 1]
            ig_r = gr[h:h + 1, :]
            b_rh = b_r[ML_HEADS + h:ML_HEADS + h + 1, :]
            m_prev = m_all[:, h:h + 1]
            logd = jnp.where(causal, b_ch - b_rh + ig_r, -jnp.inf)
            m_inter = b_ch + m_prev
            m_t = jnp.maximum(m_inter, jnp.max(logd, axis=1, keepdims=True))
            dmat = jnp.exp(logd - m_t)
            w_int = jnp.exp(m_inter - m_t)
            s = lax.dot_general(qb, kb, NT_DIMS, preferred_element_type=F32)
            qk = s * dmat
            c_old = c_ref[b, h]
            n_old = n_ref[b, h:h + 1, :]
            num = (w_int * lax.dot_general(qb, c_old.astype(BF16), NT_DIMS, preferred_element_type=F32)
                   + jnp.dot(qk.astype(BF16), v.astype(BF16), preferred_element_type=F32))
            den = (w_int * jnp.sum(q * n_old, axis=1, keepdims=True)
                   + jnp.sum(qk, axis=1, keepdims=True))
            hh = num / jnp.maximum(jnp.abs(den), jnp.exp(-m_t))
            b_last = b_ch[LP - 1:LP, :]
            logw = b_last - b_ch + ig_c
            m_new = jnp.maximum(b_last + m_prev, jnp.max(logw, axis=0, keepdims=True))
            w_c = jnp.exp(logw - m_new)
            decay = jnp.exp(b_last + m_prev - m_new)
            c_ref[b, h] = decay * c_old + lax.dot_general(
                (v * w_c).astype(BF16), kb, TN_DIMS, preferred_element_type=F32)
            n_ref[b, h:h + 1, :] = decay * n_old + jnp.sum(w_c * k, axis=0, keepdims=True)
            m_out = jnp.where(lane1 == h, m_new, m_out)
            y = _rms(hh, nw_ref[:, sl])
            hm_ref[b, :, sl] = (og_all[:, sl] * y[:L_in]).astype(BF16)
        m_ref[b] = m_out


def _mlstm(qm, km, vm, og, gc, c0, n0, m0, nw, *, L_in):
    B, T, _ = qm.shape
    BB = 2
    nchunks = T // L_in
    tok = lambda n: pl.BlockSpec((BB, L_in, n), lambda g, c: (g, c, 0))
    st_c = pl.BlockSpec((BB, ML_HEADS, ML_HEAD_DIM, ML_HEAD_DIM), lambda g, c: (g, 0, 0, 0))
    st_n = pl.BlockSpec((BB, ML_HEADS, ML_HEAD_DIM), lambda g, c: (g, 0, 0))
    st_m = pl.BlockSpec((BB, 1, LANES), lambda g, c: (g, 0, 0))
    return pl.pallas_call(
        functools.partial(_mlstm_kernel, BB=BB, L_in=L_in),
        grid=(B // BB, nchunks),
        in_specs=[tok(ML_WIDTH), tok(ML_WIDTH), tok(ML_WIDTH), tok(ML_WIDTH), tok(LANES),
                  st_c, st_n, st_m, pl.BlockSpec((1, ML_WIDTH), lambda g, c: (0, 0))],
        out_specs=[tok(ML_WIDTH), st_c, st_n, st_m],
        out_shape=[jax.ShapeDtypeStruct((B, T, ML_WIDTH), BF16),
                   jax.ShapeDtypeStruct(c0.shape, F32),
                   jax.ShapeDtypeStruct(n0.shape, F32),
                   jax.ShapeDtypeStruct(m0.shape, F32)],
        compiler_params=pltpu.CompilerParams(dimension_semantics=("arbitrary", "arbitrary"),
                                             vmem_limit_bytes=VMEM_LIMIT),
        name="mlstm",
    )(qm, km, vm, og, gc, c0, n0, m0, nw)


def _sink_attention(q4, kw, vw, sink_col, bias):
    s = lax.dot_general(q4, kw, NT_DIMS, preferred_element_type=F32) * (SWA_HEAD_DIM ** -0.5)
    if bias is not None:
        s = s + bias
    mx = jnp.maximum(jnp.max(s, axis=1, keepdims=True), sink_col)
    p = jnp.exp(s - mx)
    den = jnp.sum(p, axis=1, keepdims=True) + jnp.exp(sink_col - mx)
    return jnp.dot(p.astype(BF16), vw, preferred_element_type=F32) / den


def _sink_column(sink_ref, g, rows):
    return jnp.concatenate(
        [jnp.broadcast_to(sink_ref[:, g * SWA_GROUP + j:g * SWA_GROUP + j + 1], (rows, 1))
         for j in range(SWA_GROUP)], axis=0)


def _swa_prompt_kernel(q_ref, kc_ref, kp_ref, vc_ref, vp_ref, sink_ref, bias_ref, o_ref):
    TQ = SWA_TQ
    q = q_ref[0]
    kcat = jnp.concatenate([kp_ref[0, TQ - WINDOW:, :], kc_ref[0]], axis=0).astype(BF16)
    vcat_t = jnp.concatenate([vp_ref[0, TQ - WINDOW:, :], vc_ref[0]], axis=0).T.astype(BF16)
    bias_t = bias_ref[0]
    outs = []
    for g in range(SWA_KV_HEADS):
        gs = slice(g * SWA_HEAD_DIM, (g + 1) * SWA_HEAD_DIM)
        heads = range(g * SWA_GROUP, (g + 1) * SWA_GROUP)
        q4 = jnp.concatenate([q[:, h * SWA_HEAD_DIM:(h + 1) * SWA_HEAD_DIM] for h in heads],
                             axis=0).astype(BF16)
        sink_row = jnp.concatenate([jnp.broadcast_to(sink_ref[:, h:h + 1], (1, TQ)) for h in heads], axis=1)
        s_t = lax.dot_general(kcat[:, gs], q4, NT_DIMS, preferred_element_type=F32)
        s_t = s_t * (SWA_HEAD_DIM ** -0.5) + bias_t
        mx = jnp.maximum(jnp.max(s_t, axis=0, keepdims=True), sink_row)
        p = jnp.exp(s_t - mx)
        den = jnp.sum(p, axis=0, keepdims=True) + jnp.exp(sink_row - mx)
        o_t = jnp.dot(vcat_t[gs, :], p.astype(BF16), preferred_element_type=F32) / den
        outs += [o_t[:, j * TQ:(j + 1) * TQ] for j in range(SWA_GROUP)]
    o_ref[0] = jnp.concatenate(outs, axis=0).T.astype(BF16)


def _swa_prompt_bias():
    TQ = SWA_TQ
    t = np.arange(SWA_GROUP * TQ) % TQ
    k = np.arange(WINDOW + TQ)
    qc = (t // SWA_CHUNK)[None, :]
    kc = (k // SWA_CHUNK)[:, None]
    visible = (kc >= qc) & (kc <= qc + WINDOW // SWA_CHUNK)
    first = visible & (k[:, None] >= WINDOW)
    return np.where(np.stack([first, visible]), 0.0, -np.inf).astype(np.float32)


def _swa_prompt(sq, sk, sv, sinks):
    B, T, _ = sq.shape
    TQ = SWA_TQ
    cur = lambda b, i: (b, i, 0)
    prev = lambda b, i: (b, jnp.maximum(i - 1, 0), 0)
    bias = jnp.asarray(_swa_prompt_bias())
    return pl.pallas_call(
        _swa_prompt_kernel,
        grid=(B, T // TQ),
        in_specs=[pl.BlockSpec((1, TQ, SWA_WIDTH), cur),
                  pl.BlockSpec((1, TQ, SWA_KV_WIDTH), cur), pl.BlockSpec((1, TQ, SWA_KV_WIDTH), prev),
                  pl.BlockSpec((1, TQ, SWA_KV_WIDTH), cur), pl.BlockSpec((1, TQ, SWA_KV_WIDTH), prev),
                  pl.BlockSpec((1, SWA_HEADS), lambda b, i: (0, 0)),
                  pl.BlockSpec((1,) + bias.shape[1:], lambda b, i: (jnp.minimum(i, 1), 0, 0))],
        out_specs=pl.BlockSpec((1, TQ, SWA_WIDTH), cur),
        out_shape=jax.ShapeDtypeStruct((B, T, SWA_WIDTH), BF16),
        compiler_params=pltpu.CompilerParams(dimension_semantics=("arbitrary", "arbitrary"),
                                             vmem_limit_bytes=VMEM_LIMIT),
        name="swa_prompt",
    )(sq, sk, sk, sv, sv, sinks, bias)


def _swa_sample_kernel(q_ref, kn_ref, vn_ref, kc_ref, vc_ref, sink_ref, o_ref, kw_ref, vw_ref):
    T = DEC_SEQ
    q = q_ref[0]
    k_all = jnp.concatenate([kc_ref[0], kn_ref[0]], axis=0)
    v_all = jnp.concatenate([vc_ref[0], vn_ref[0]], axis=0)
    kw_ref[0] = k_all[T:]
    vw_ref[0] = v_all[T:]
    kb, vb = k_all.astype(BF16), v_all.astype(BF16)
    for g in range(SWA_KV_HEADS):
        gs = slice(g * SWA_HEAD_DIM, (g + 1) * SWA_HEAD_DIM)
        q4 = jnp.concatenate(
            [q[:, (g * SWA_GROUP + j) * SWA_HEAD_DIM:(g * SWA_GROUP + j + 1) * SWA_HEAD_DIM]
             for j in range(SWA_GROUP)], axis=0).astype(BF16)
        o = _sink_attention(q4, kb[:, gs], vb[:, gs], _sink_column(sink_ref, g, T), None)
        for j in range(SWA_GROUP):
            hd = (g * SWA_GROUP + j) * SWA_HEAD_DIM
            o_ref[0, :, hd:hd + SWA_HEAD_DIM] = o[j * T:(j + 1) * T].astype(BF16)


def _swa_sample(sq, sk, sv, k_cache, v_cache, sinks):
    B, T, _ = sq.shape
    b3 = lambda b: (b, 0, 0)
    return pl.pallas_call(
        _swa_sample_kernel,
        grid=(B,),
        in_specs=[pl.BlockSpec((1, T, SWA_WIDTH), b3),
                  pl.BlockSpec((1, T, SWA_KV_WIDTH), b3), pl.BlockSpec((1, T, SWA_KV_WIDTH), b3),
                  pl.BlockSpec((1, WINDOW, SWA_KV_WIDTH), b3), pl.BlockSpec((1, WINDOW, SWA_KV_WIDTH), b3),
                  pl.BlockSpec((1, SWA_HEADS), lambda b: (0, 0))],
        out_specs=[pl.BlockSpec((1, T, SWA_WIDTH), b3),
                   pl.BlockSpec((1, WINDOW, SWA_KV_WIDTH), b3), pl.BlockSpec((1, WINDOW, SWA_KV_WIDTH), b3)],
        out_shape=[jax.ShapeDtypeStruct((B, T, SWA_WIDTH), BF16),
                   jax.ShapeDtypeStruct((B, WINDOW, SWA_KV_WIDTH), F32),
                   jax.ShapeDtypeStruct((B, WINDOW, SWA_KV_WIDTH), F32)],
        compiler_params=pltpu.CompilerParams(dimension_semantics=("arbitrary",),
                                             vmem_limit_bytes=VMEM_LIMIT),
        name="swa_sample",
    )(sq, sk, sv, k_cache, v_cache, sinks)


def _outproj_kernel(hm_ref, hs_ref, x_ref, wo_ref, nw_ref, x1_ref, h2_ref):
    mix = (jnp.dot(hm_ref[...], wo_ref[:ML_WIDTH, :], preferred_element_type=F32)
           + jnp.dot(hs_ref[...], wo_ref[ML_WIDTH:, :], preferred_element_type=F32))
    x1 = x_ref[...] + mix
    x1_ref[...] = x1
    h2_ref[...] = pltpu.bitcast(_rms(x1, nw_ref[...]).astype(BF16), jnp.uint32)


def _outproj(hm, hs, x2d, wo, nw):
    T = x2d.shape[0]
    TM = TM_OUTPROJ
    row = lambda i: (i, 0)
    const = lambda i: (0, 0)
    return pl.pallas_call(
        _outproj_kernel,
        grid=(T // TM,),
        in_specs=[pl.BlockSpec((TM, ML_WIDTH), row), pl.BlockSpec((TM, SWA_WIDTH), row),
                  pl.BlockSpec((TM, D_MODEL), row),
                  pl.BlockSpec((D_MODEL, D_MODEL), const), pl.BlockSpec((1, D_MODEL), const)],
        out_specs=[pl.BlockSpec((TM, D_MODEL), row), pl.BlockSpec((TM // 2, D_MODEL), row)],
        out_shape=[jax.ShapeDtypeStruct((T, D_MODEL), F32),
                   jax.ShapeDtypeStruct((T // 2, D_MODEL), jnp.uint32)],
        compiler_params=pltpu.CompilerParams(dimension_semantics=("arbitrary",),
                                             vmem_limit_bytes=VMEM_LIMIT),
        name="outproj",
    )(hm, hs, x2d, wo, nw)


_CAND_NB = [PEER_TOPK // (a + 1) for a in range(PEER_TOPK)]
_CAND_ROWS = 16 + 8 * 7 + 8


def _extract_top16(S, exact):
    R = S.shape[0]
    iota = lax.broadcasted_iota(jnp.int32, S.shape, 0)
    rank = jnp.full(S.shape, float(PEER_TOPK), F32)
    vals = []
    for r in range(PEER_TOPK):
        mx = jnp.max(S, axis=0, keepdims=True)
        if exact:
            idx = jnp.min(jnp.where(S == mx, iota, R), axis=0, keepdims=True)
            hit = iota == idx
        else:
            hit = S == mx
        rank = jnp.where(hit, float(r), rank)
        S = jnp.where(hit, -jnp.inf, S)
        vals.append(mx)
    return vals, rank


def _count_excess(flags):
    return jnp.abs(jnp.sum(flags, axis=0, keepdims=True) - float(PEER_TOPK))


def _route_tables(q, sk1_ref, sk2_ref, r2_ref, e2_ref, lim_ref, e1_ref, exact):
    N = q.shape[0]
    row8 = lax.broadcasted_iota(jnp.int32, (8, N), 0)
    rowc = lax.broadcasted_iota(jnp.int32, (_CAND_ROWS, N), 0)
    mid = rowc - 16
    flat = jnp.where(rowc < 16, rowc,
                     jnp.where(rowc < _CAND_ROWS - 8,
                               PEER_TOPK * ((mid >> 3) + 1) + (mid & 7),
                               PEER_TOPK * (rowc - (_CAND_ROWS - 16))))
    excess = jnp.zeros((1, N), F32)
    for h in range(PEER_HEADS):
        c0 = h * PEER_KEY_DIM
        s1 = lax.dot_general(sk1_ref[...], q[:, c0:c0 + PEER_HALF], NT_DIMS, preferred_element_type=F32)
        s2 = lax.dot_general(sk2_ref[...], q[:, c0 + PEER_HALF:c0 + PEER_KEY_DIM], NT_DIMS,
                             preferred_element_type=F32)
        t1, r1 = _extract_top16(s1, exact)
        t2, r2 = _extract_top16(s2, exact)
        t2_16 = jnp.concatenate(t2, axis=0)
        t2_8 = t2_16[:8]
        blocks = [t1[0] + t2_16, t1[1] + t2_8]
        for a in range(2, 8):
            blocks.append(jnp.where(row8 < _CAND_NB[a], t1[a] + t2_8, -jnp.inf))
        blocks.append(jnp.concatenate(t1[8:], axis=0) + t2[0])
        cand = jnp.concatenate(blocks, axis=0)
        sel = jnp.zeros(cand.shape, F32)
        work = cand
        for _ in range(PEER_TOPK):
            mx = jnp.max(work, axis=0, keepdims=True)
            if exact:
                idx = jnp.min(jnp.where(work == mx, flat, PEER_TOPK * PEER_TOPK), axis=0, keepdims=True)
                hit = flat == idx
            else:
                hit = work == mx
            sel = jnp.where(hit, 1.0, sel)
            work = jnp.where(hit, -jnp.inf, work)
        if not exact:
            excess = jnp.maximum(excess, _count_excess(jnp.where(r1 < float(PEER_TOPK), 1.0, 0.0)))
            excess = jnp.maximum(excess, _count_excess(jnp.where(r2 < float(PEER_TOPK), 1.0, 0.0)))
            excess = jnp.maximum(excess, _count_excess(sel))
        z = jnp.sum(jnp.where(sel > 0.0, jnp.exp(cand - cand[0:1]), 0.0), axis=0, keepdims=True)
        counts = [jnp.sum(sel[0:16], axis=0, keepdims=True)]
        for a in range(1, 8):
            counts.append(jnp.sum(sel[8 + 8 * a:16 + 8 * a], axis=0, keepdims=True))
        for a in range(8, PEER_TOPK):
            counts.append(sel[_CAND_ROWS - 16 + a:_CAND_ROWS - 15 + a])
        lim = jnp.zeros(r1.shape, F32)
        for a in range(PEER_TOPK):
            lim = jnp.where(r1 == float(a), counts[a], lim)
        r2_ref[h] = pltpu.bitcast(r2.astype(BF16), jnp.uint32)
        e2_ref[h] = pltpu.bitcast(jnp.exp(s2 - t2[0]).astype(BF16), jnp.uint32)
        lim_ref[h] = lim
        e1_ref[h] = jnp.exp(s1 - t1[0]) / z
    return excess


def _route_kernel(h2_ref, wq_ref, sk1_ref, sk2_ref, r2_ref, e2_ref, lim_ref, e1_ref):
    h2 = pltpu.bitcast(h2_ref[...], BF16)
    q = jnp.dot(h2, wq_ref[...], preferred_element_type=F32).astype(BF16)
    outs = (r2_ref, e2_ref, lim_ref, e1_ref)
    excess = _route_tables(q, sk1_ref, sk2_ref, *outs, exact=False)

    @pl.when(jnp.max(excess) > 0.0)
    def _():
        _route_tables(q, sk1_ref, sk2_ref, *outs, exact=True)


def _route(h2, wq, sk1, sk2):
    T = 2 * h2.shape[0]
    TM = TM_ROUTE
    const = lambda i: (0, 0)
    tab = pl.BlockSpec((None, PEER_HEADS, N_KEYS, TM), lambda i: (i, 0, 0, 0))
    tab_packed = pl.BlockSpec((None, PEER_HEADS, N_KEYS // 2, TM), lambda i: (i, 0, 0, 0))
    tab_shape = jax.ShapeDtypeStruct((T // TM, PEER_HEADS, N_KEYS, TM), F32)
    tab_packed_shape = jax.ShapeDtypeStruct((T // TM, PEER_HEADS, N_KEYS // 2, TM), jnp.uint32)
    return pl.pallas_call(
        _route_kernel,
        grid=(T // TM,),
        in_specs=[pl.BlockSpec((TM // 2, D_MODEL), lambda i: (i, 0)),
                  pl.BlockSpec((D_MODEL, PEER_HEADS * PEER_KEY_DIM), const),
                  pl.BlockSpec((N_KEYS, PEER_HALF), const), pl.BlockSpec((N_KEYS, PEER_HALF), const)],
        out_specs=[tab_packed, tab_packed, tab, tab],
        out_shape=[tab_packed_shape, tab_packed_shape, tab_shape, tab_shape],
        compiler_params=pltpu.CompilerParams(dimension_semantics=("arbitrary",),
                                             vmem_limit_bytes=VMEM_LIMIT),
        name="route",
    )(h2, wq, sk1, sk2)


def _peer_kernel(h2_ref, x1_ref, u_ref, vt_ref, r2_ref, e2_ref, lim_ref, e1_ref, nfw_ref,
                 y_ref, acc_ref, act_ref, p_ref, *, n_e, n_work):
    g = pl.program_id(0)
    per_blk = PEER_SUB * PEER_NSUB // N_KEYS
    e_b = jnp.clip(g - 1, 0, n_work - 1) % n_e
    e_c = jnp.clip(g - 2, 0, n_work - 1) % n_e

    @pl.when(g == 0)
    def _():
        act_ref[...] = jnp.zeros(act_ref.shape, F32)
        p_ref[...] = jnp.zeros(p_ref.shape, BF16)

    @pl.when(e_c == 0)
    def _():
        acc_ref[...] = jnp.zeros(acc_ref.shape, F32)

    def stages(slot_a, slot_b):
        h2 = pltpu.bitcast(h2_ref[...], BF16)
        for s in range(PEER_NSUB):
            u = pltpu.bitcast(u_ref[s * PEER_SUB // 2:(s + 1) * PEER_SUB // 2, :], BF16)
            act_ref[slot_a, s * PEER_SUB:(s + 1) * PEER_SUB, :] = lax.dot_general(
                u, h2, NT_DIMS, preferred_element_type=F32)

        n_rb = N_KEYS // PEER_RB
        zero = jnp.zeros((PEER_RB, LANES), BF16)
        for jb in range(per_blk):
            j = e_b * per_blk + jb
            for lh in range(h2.shape[0] // LANES):
                cols = slice(lh * LANES, (lh + 1) * LANES)
                gw = [None] * n_rb
                for h in range(PEER_HEADS):
                    lim = jnp.broadcast_to(lim_ref[lh, h, pl.ds(j, 1), :], (PEER_RB, LANES)).astype(BF16)
                    e1 = jnp.broadcast_to(e1_ref[lh, h, pl.ds(j, 1), :], (PEER_RB, LANES)).astype(BF16)
                    for rb in range(n_rb):
                        words = slice(rb * PEER_RB // 2, (rb + 1) * PEER_RB // 2)
                        r2 = pltpu.bitcast(r2_ref[lh, h, words, :], BF16)
                        e2 = pltpu.bitcast(e2_ref[lh, h, words, :], BF16)
                        t = jnp.where(r2 < lim, e2, zero) * e1
                        gw[rb] = t if gw[rb] is None else gw[rb] + t
                for rb in range(n_rb):
                    arows = slice(jb * N_KEYS + rb * PEER_RB, jb * N_KEYS + (rb + 1) * PEER_RB)
                    a = act_ref[slot_b, arows, cols]
                    ga = 0.5 * a * (1.0 + lax.erf(a * np.float32(np.sqrt(0.5))))
                    p_ref[slot_b, arows, cols] = gw[rb] * ga.astype(BF16)

        acc_ref[...] += jnp.dot(pltpu.bitcast(vt_ref[...], BF16), p_ref[slot_a],
                                preferred_element_type=F32)

    @pl.when(g % 2 == 0)
    def _():
        stages(0, 1)

    @pl.when(g % 2 == 1)
    def _():
        stages(1, 0)

    @pl.when(jnp.logical_and(g >= 2, e_c == n_e - 1))
    def _():
        x = x1_ref[...] + acc_ref[...].T
        y_ref[...] = _rms(x, nfw_ref[...])


def _peer(h2, x1, u, vt, r2, e2, lim, e1, nfw):
    T = x1.shape[0]
    TM = TM_PEER
    ET = PEER_SUB * PEER_NSUB
    n_e = N_EXPERTS // ET
    n_work = (T // TM) * n_e
    item = lambda g, lag: jnp.clip(g - lag, 0, n_work - 1)
    rt = TM // TM_ROUTE
    tab = pl.BlockSpec((rt, PEER_HEADS, N_KEYS, TM_ROUTE), lambda g: (item(g, 1) // n_e, 0, 0, 0))
    tab_packed = pl.BlockSpec((rt, PEER_HEADS, N_KEYS // 2, TM_ROUTE),
                              lambda g: (item(g, 1) // n_e, 0, 0, 0))
    tok_c = lambda g: (item(g, 2) // n_e, 0)
    return pl.pallas_call(
        functools.partial(_peer_kernel, n_e=n_e, n_work=n_work),
        grid=(n_work + 2,),
        in_specs=[pl.BlockSpec((TM // 2, D_MODEL), lambda g: (item(g, 0) // n_e, 0)),
                  pl.BlockSpec((TM, D_MODEL), tok_c),
                  pl.BlockSpec((ET // 2, D_MODEL), lambda g: (item(g, 0) % n_e, 0)),
                  pl.BlockSpec((D_MODEL // 2, ET), lambda g: (0, item(g, 2) % n_e)),
                  tab_packed, tab_packed, tab, tab,
                  pl.BlockSpec((1, D_MODEL), lambda g: (0, 0))],
        out_specs=pl.BlockSpec((TM, D_MODEL), tok_c),
        out_shape=jax.ShapeDtypeStruct((T, D_MODEL), F32),
        scratch_shapes=[pltpu.VMEM((D_MODEL, TM), F32), pltpu.VMEM((2, ET, TM), F32),
                        pltpu.VMEM((2, ET, TM), BF16)],
        compiler_params=pltpu.CompilerParams(dimension_semantics=("arbitrary",),
                                             vmem_limit_bytes=VMEM_LIMIT),
        name="peer",
    )(h2, x1, u, vt, r2, e2, lim, e1, nfw)


def _rope_tables(pos):
    inv = ROPE_THETA ** (-jnp.arange(ROPE_HALF, dtype=F32) * 2.0 / ROPE_DIM)
    ang = pos.astype(F32)[:, None] * inv[None, :]
    cos, sin = jnp.cos(ang), jnp.sin(ang)
    n = pos.shape[0]
    rest = SWA_HEAD_DIM - ROPE_DIM
    zh = jnp.zeros((n, ROPE_HALF), F32)
    cos_h = jnp.concatenate([cos, cos, jnp.ones((n, rest), F32)], axis=1)
    sina_h = jnp.concatenate([-sin, zh, jnp.zeros((n, rest), F32)], axis=1)
    sinb_h = jnp.concatenate([zh, sin, jnp.zeros((n, rest), F32)], axis=1)
    rep = LANES // SWA_HEAD_DIM
    return tuple(jnp.tile(t, (1, rep)) for t in (cos_h, sina_h, sinb_h))


def _pack_kernel(x_ref, o_ref, *, transpose):
    x = x_ref[...]
    if transpose:
        x = x.T
    o_ref[...] = pltpu.bitcast(x.astype(BF16), jnp.uint32)


def _pack_expert_table(w, *, transpose):
    n, d = w.shape
    rows = PACK_ROWS
    if transpose:
        out_spec = pl.BlockSpec((d // 2, rows), lambda i: (0, i))
        out_shape = jax.ShapeDtypeStruct((d // 2, n), jnp.uint32)
    else:
        out_spec = pl.BlockSpec((rows // 2, d), lambda i: (i, 0))
        out_shape = jax.ShapeDtypeStruct((n // 2, d), jnp.uint32)
    return pl.pallas_call(
        functools.partial(_pack_kernel, transpose=transpose),
        grid=(n // rows,),
        in_specs=[pl.BlockSpec((rows, d), lambda i: (i, 0))],
        out_specs=out_spec,
        out_shape=out_shape,
        compiler_params=pltpu.CompilerParams(dimension_semantics=("arbitrary",),
                                             vmem_limit_bytes=VMEM_LIMIT),
        name="pack_vt" if transpose else "pack_u",
    )(w)


def _layer_tokens(x2d, tables, tab_map, W):
    return _inproj(x2d, W["norm_mix"], W["w_in"], W["bias"], *tables, tab_map)


def _ffn(hm, hs, x2d, W):
    x1, h2 = _outproj(hm, hs, x2d, W["w_out"], W["norm_ffn"])
    r2, e2, lim, e1 = _route(h2, W["w_q"], W["sk1"], W["sk2"])
    return _peer(h2, x1, W["u"], W["vt"], r2, e2, lim, e1, W["norm_final"])


def kernel(x_prompt, x_sample, cache_swa_k, cache_swa_v, state_mlstm_c, state_mlstm_n, state_mlstm_m,
           norm_mix_w, w_in, mlstm_if_bias, mlstm_norm_w, swa_sinks, w_out, norm_ffn_w,
           peer_w_q, peer_sub_keys_1, peer_sub_keys_2, peer_u, peer_v, norm_final_w):
    B, S, _ = x_prompt.shape
    DB, DS, _ = x_sample.shape
    l = 0
    wi = w_in[l]
    s_q = 4 * ML_WIDTH + 2 * ML_HEADS
    w_perm = jnp.concatenate(
        [wi[:, :4 * ML_WIDTH], wi[:, s_q:], wi[:, 4 * ML_WIDTH:s_q],
         jnp.zeros((D_MODEL, LANES - 2 * ML_HEADS), F32)], axis=1).astype(BF16)
    bias_pad = jnp.concatenate([mlstm_if_bias[l], jnp.zeros((LANES - 2 * ML_HEADS,), F32)])[None, :]
    W = {
        "norm_mix": norm_mix_w[l][None, :],
        "w_in": w_perm,
        "bias": bias_pad,
        "w_out": w_out[l].astype(BF16),
        "norm_ffn": norm_ffn_w[l][None, :],
        "w_q": peer_w_q[l].astype(BF16),
        "sk1": peer_sub_keys_1[l].astype(BF16),
        "sk2": peer_sub_keys_2[l].astype(BF16),
        "u": _pack_expert_table(peer_u[l], transpose=False),
        "vt": _pack_expert_table(peer_v[l], transpose=True),
        "norm_final": norm_final_w[None, :],
    }
    ml_nw = mlstm_norm_w[l][None, :]
    sinks = swa_sinks[l][None, :]

    xp = x_prompt.reshape(B * S, D_MODEL)
    tiles_per_seq = S // TM_INPROJ
    tabs_p = _rope_tables(jnp.arange(S, dtype=jnp.int32))
    qm, km, vm, og, sq, sk, sv, gc = _layer_tokens(xp, tabs_p, lambda i: (i % tiles_per_seq, 0), W)
    r3 = lambda a: a.reshape(B, S, a.shape[-1])
    zc = jnp.zeros((B, ML_HEADS, ML_HEAD_DIM, ML_HEAD_DIM), F32)
    zn = jnp.zeros((B, ML_HEADS, ML_HEAD_DIM), F32)
    zm = jnp.zeros((B, 1, LANES), F32)
    hm_p, c_p, n_p, m_p = _mlstm(r3(qm), r3(km), r3(vm), r3(og), r3(gc), zc, zn, zm, ml_nw, L_in=ML_CHUNK)
    sk3, sv3 = r3(sk), r3(sv)
    hs_p = _swa_prompt(r3(sq), sk3, sv3, sinks)
    y_p = _ffn(hm_p.reshape(B * S, ML_WIDTH), hs_p.reshape(B * S, SWA_WIDTH), xp, W)
    kv_shape = (1, B, WINDOW, SWA_KV_HEADS, SWA_HEAD_DIM)
    k_win_p = sk3[:, S - WINDOW:].reshape(kv_shape)
    v_win_p = sv3[:, S - WINDOW:].reshape(kv_shape)

    xs = x_sample.reshape(DB * DS, D_MODEL)
    pos_s = PAST_LEN + jnp.arange(DS, dtype=jnp.int32)
    tabs_s = tuple(jnp.tile(t, (DB, 1)) for t in _rope_tables(pos_s))
    qm, km, vm, og, sq, sk, sv, gc = _layer_tokens(xs, tabs_s, lambda i: (i, 0), W)
    r3s = lambda a: a.reshape(DB, DS, a.shape[-1])
    m0 = jnp.concatenate([state_mlstm_m[l], jnp.zeros((DB, LANES - ML_HEADS), F32)], axis=1)[:, None, :]
    hm_s, c_s, n_s, m_s = _mlstm(r3s(qm), r3s(km), r3s(vm), r3s(og), r3s(gc),
                                 state_mlstm_c[l], state_mlstm_n[l], m0, ml_nw, L_in=DS)
    kc = cache_swa_k[l].reshape(DB, WINDOW, SWA_KV_WIDTH)
    vc = cache_swa_v[l].reshape(DB, WINDOW, SWA_KV_WIDTH)
    hs_s, k_win_s, v_win_s = _swa_sample(r3s(sq), r3s(sk), r3s(sv), kc, vc, sinks)
    y_s = _ffn(hm_s.reshape(DB * DS, ML_WIDTH), hs_s.reshape(DB * DS, SWA_WIDTH), xs, W)
    kv_shape_s = (1, DB, WINDOW, SWA_KV_HEADS, SWA_HEAD_DIM)

    return (y_p.reshape(B, S, D_MODEL), y_s.reshape(DB, DS, D_MODEL),
            k_win_p, v_win_p, c_p[None], n_p[None], m_p[None, :, 0, :ML_HEADS],
            k_win_s.reshape(kv_shape_s), v_win_s.reshape(kv_shape_s),
            c_s[None], n_s[None], m_s[None, :, 0, :ML_HEADS])
```

```python
import functools

import jax
import jax.numpy as jnp
import numpy as np
from jax import lax
from jax.experimental import pallas as pl
from jax.experimental.pallas import tpu as pltpu

F32 = jnp.float32
BF16 = jnp.bfloat16

D_MODEL = 1024
SEQ = 8192
DEC_SEQ = 32
PAST_LEN = 4096
NORM_EPS = 1e-6
ML_HEADS = 4
ML_HEAD_DIM = 128
ML_WIDTH = ML_HEADS * ML_HEAD_DIM
SWA_HEADS = 8
SWA_KV_HEADS = 2
SWA_GROUP = SWA_HEADS // SWA_KV_HEADS
SWA_HEAD_DIM = 64
SWA_WIDTH = SWA_HEADS * SWA_HEAD_DIM
SWA_KV_WIDTH = SWA_KV_HEADS * SWA_HEAD_DIM
WINDOW = 128
SWA_CHUNK = 64
ROPE_THETA = 500000.0
ROPE_DIM = SWA_HEAD_DIM // 4
ROPE_HALF = ROPE_DIM // 2
PEER_HEADS = 8
N_KEYS = 128
N_EXPERTS = N_KEYS * N_KEYS
PEER_TOPK = 16
PEER_KEY_DIM = 256
PEER_HALF = PEER_KEY_DIM // 2

LANES = 128
VMEM_LIMIT = 52 * 1024 * 1024

COL_MQ, COL_MK, COL_MV, COL_MO = 0, ML_WIDTH, 2 * ML_WIDTH, 3 * ML_WIDTH
COL_SQ = 4 * ML_WIDTH
COL_SK = COL_SQ + SWA_WIDTH
COL_SV = COL_SK + SWA_KV_WIDTH
COL_G = COL_SV + SWA_KV_WIDTH
IN_COLS_PAD = COL_G + LANES

TM_INPROJ = 256
ML_CHUNK = 128
SWA_TQ = 256
TM_OUTPROJ = 256
TM_ROUTE = 128
TM_PEER = 512
PEER_SUB = 512
PEER_NSUB = 2
PEER_RB = 16
PEER_CROWS = 256
PACK_ROWS = 512

NT_DIMS = (((1,), (1,)), ((), ()))
TN_DIMS = (((0,), (0,)), ((), ()))


def _rms(x, w):
    return x * lax.rsqrt(jnp.mean(x * x, axis=-1, keepdims=True) + NORM_EPS) * w


def _inproj_kernel(x_ref, nw_ref, w_ref, bias_ref, cos_ref, sina_ref, sinb_ref,
                   qm_ref, km_ref, vm_ref, og_ref, sq_ref, sk_ref, sv_ref, gc_ref):
    h = _rms(x_ref[...], nw_ref[...])
    proj = jnp.dot(h.astype(BF16), w_ref[...], preferred_element_type=F32)
    qm_ref[...] = proj[:, COL_MQ:COL_MQ + ML_WIDTH]
    km_ref[...] = proj[:, COL_MK:COL_MK + ML_WIDTH] * (ML_HEAD_DIM ** -0.5)
    vm_ref[...] = proj[:, COL_MV:COL_MV + ML_WIDTH]
    og_ref[...] = jax.nn.sigmoid(proj[:, COL_MO:COL_MO + ML_WIDTH])
    cosf, sina, sinb = cos_ref[...], sina_ref[...], sinb_ref[...]

    def rope(xc):
        return (xc * cosf + pltpu.roll(xc, LANES - ROPE_HALF, 1) * sina
                + pltpu.roll(xc, ROPE_HALF, 1) * sinb)

    for j in range(SWA_WIDTH // LANES):
        sq_ref[:, j * LANES:(j + 1) * LANES] = rope(proj[:, COL_SQ + j * LANES:COL_SQ + (j + 1) * LANES])
    sk_ref[...] = rope(proj[:, COL_SK:COL_SK + LANES])
    sv_ref[...] = proj[:, COL_SV:COL_SV + LANES]
    g = proj[:, COL_G:COL_G + LANES] + bias_ref[...]
    lane = lax.broadcasted_iota(jnp.int32, g.shape, 1)
    gc_ref[...] = jnp.where(lane < ML_HEADS, g, jax.nn.log_sigmoid(g))


def _inproj(x2d, nw, w_perm, bias_pad, cos_t, sina_t, sinb_t, tab_map):
    T = x2d.shape[0]
    TM = TM_INPROJ
    row = lambda i: (i, 0)
    const = lambda i: (0, 0)
    f = lambda n: jax.ShapeDtypeStruct((T, n), F32)
    return pl.pallas_call(
        _inproj_kernel,
        grid=(T // TM,),
        in_specs=[pl.BlockSpec((TM, D_MODEL), row),
                  pl.BlockSpec((1, D_MODEL), const),
                  pl.BlockSpec((D_MODEL, IN_COLS_PAD), const),
                  pl.BlockSpec((1, LANES), const),
                  pl.BlockSpec((TM, LANES), tab_map),
                  pl.BlockSpec((TM, LANES), tab_map),
                  pl.BlockSpec((TM, LANES), tab_map)],
        out_specs=[pl.BlockSpec((TM, ML_WIDTH), row)] * 4
                  + [pl.BlockSpec((TM, SWA_WIDTH), row),
                     pl.BlockSpec((TM, LANES), row), pl.BlockSpec((TM, LANES), row),
                     pl.BlockSpec((TM, LANES), row)],
        out_shape=[f(ML_WIDTH)] * 4 + [f(SWA_WIDTH), f(LANES), f(LANES), f(LANES)],
        compiler_params=pltpu.CompilerParams(dimension_semantics=("arbitrary",),
                                             vmem_limit_bytes=VMEM_LIMIT),
        name="inproj",
    )(x2d, nw, w_perm, bias_pad, cos_t, sina_t, sinb_t)


def _mlstm_kernel(q_ref, k_ref, v_ref, og_ref, gc_ref, c0_ref, n0_ref, m0_ref, nw_ref,
                  hm_ref, c_ref, n_ref, m_ref, *, BB, L_in):
    LP = ML_CHUNK

    @pl.when(pl.program_id(1) == 0)
    def _():
        c_ref[...] = c0_ref[...]
        n_ref[...] = n0_ref[...]
        m_ref[...] = m0_ref[...]

    row = lax.broadcasted_iota(jnp.int32, (LP, LP), 0)
    col = lax.broadcasted_iota(jnp.int32, (LP, LP), 1)
    causal = row >= col
    tri = causal.astype(F32)
    tri_t = (row <= col).astype(F32)
    lane = lax.broadcasted_iota(jnp.int32, (LP, LANES), 1)
    lane1 = lax.broadcasted_iota(jnp.int32, (1, LANES), 1)

    def pad_rows(a):
        if L_in == LP:
            return a
        return jnp.concatenate([a, jnp.zeros((LP - L_in, a.shape[1]), a.dtype)], axis=0)

    for b in range(BB):
        gc = gc_ref[b]
        if L_in < LP:
            lane_pad = lax.broadcasted_iota(jnp.int32, (LP - L_in, LANES), 1)
            fill = jnp.where(lane_pad < ML_HEADS, -jnp.inf, 0.0).astype(F32)
            gc = jnp.concatenate([gc, fill], axis=0)
        gr = gc.T
        lf_c = jnp.where(lane >= ML_HEADS, gc, 0.0)
        b_c = jnp.dot(tri, lf_c, precision=lax.Precision.HIGHEST, preferred_element_type=F32)
        rowi = lax.broadcasted_iota(jnp.int32, (8, LP), 0)
        lf_r = jnp.where(rowi >= ML_HEADS, gr[:8], 0.0)
        b_r = jnp.dot(lf_r, tri_t, precision=lax.Precision.HIGHEST, preferred_element_type=F32)
        m_all = m_ref[b]
        m_out = m_all
        q_all = pad_rows(q_ref[b])
        k_all = pad_rows(k_ref[b])
        v_all = pad_rows(v_ref[b])
        og_all = og_ref[b]
        for h in range(ML_HEADS):
            sl = slice(h * ML_HEAD_DIM, (h + 1) * ML_HEAD_DIM)
            q, k, v = q_all[:, sl], k_all[:, sl], v_all[:, sl]
            qb, kb = q.astype(BF16), k.astype(BF16)
            ig_c = gc[:, h:h + 1]
            b_ch = b_c[:, ML_HEADS + h:ML_HEADS + h + 1]
            ig_r = gr[h:h + 1, :]
            b_rh = b_r[ML_HEADS + h:ML_HEADS + h + 1, :]
            m_prev = m_all[:, h:h + 1]
            logd = jnp.where(causal, b_ch - b_rh + ig_r, -jnp.inf)
            m_inter = b_ch + m_prev
            m_t = jnp.maximum(m_inter, jnp.max(logd, axis=1, keepdims=True))
            dmat = jnp.exp(logd - m_t)
            w_int = jnp.exp(m_inter - m_t)
            s = lax.dot_general(qb, kb, NT_DIMS, preferred_element_type=F32)
            qk = s * dmat
            c_old = c_ref[b, h]
            n_old = n_ref[b, h:h + 1, :]
            num = (w_int * lax.dot_general(qb, c_old.astype(BF16), NT_DIMS, preferred_element_type=F32)
                   + jnp.dot(qk.astype(BF16), v.astype(BF16), preferred_element_type=F32))
            den = (w_int * jnp.sum(q * n_old, axis=1, keepdims=True)
                   + jnp.sum(qk, axis=1, keepdims=True))
            hh = num / jnp.maximum(jnp.abs(den), jnp.exp(-m_t))
            b_last = b_ch[LP - 1:LP, :]
            logw = b_last - b_ch + ig_c
            m_new = jnp.maximum(b_last + m_prev, jnp.max(logw, axis=0, keepdims=True))
            w_c = jnp.exp(logw - m_new)
            decay = jnp.exp(b_last + m_prev - m_new)
            c_ref[b, h] = decay * c_old + lax.dot_general(
                (v * w_c).astype(BF16), kb, TN_DIMS, preferred_element_type=F32)
            n_ref[b, h:h + 1, :] = decay * n_old + jnp.sum(w_c * k, axis=0, keepdims=True)
            m_out = jnp.where(lane1 == h, m_new, m_out)
            y = _rms(hh, nw_ref[:, sl])
            hm_ref[b, :, sl] = (og_all[:, sl] * y[:L_in]).astype(BF16)
        m_ref[b] = m_out


def _mlstm(qm, km, vm, og, gc, c0, n0, m0, nw, *, L_in):
    B, T, _ = qm.shape
    BB = 2
    nchunks = T // L_in
    tok = lambda n: pl.BlockSpec((BB, L_in, n), lambda g, c: (g, c, 0))
    st_c = pl.BlockSpec((BB, ML_HEADS, ML_HEAD_DIM, ML_HEAD_DIM), lambda g, c: (g, 0, 0, 0))
    st_n = pl.BlockSpec((BB, ML_HEADS, ML_HEAD_DIM), lambda g, c: (g, 0, 0))
    st_m = pl.BlockSpec((BB, 1, LANES), lambda g, c: (g, 0, 0))
    return pl.pallas_call(
        functools.partial(_mlstm_kernel, BB=BB, L_in=L_in),
        grid=(B // BB, nchunks),
        in_specs=[tok(ML_WIDTH), tok(ML_WIDTH), tok(ML_WIDTH), tok(ML_WIDTH), tok(LANES),
                  st_c, st_n, st_m, pl.BlockSpec((1, ML_WIDTH), lambda g, c: (0, 0))],
        out_specs=[tok(ML_WIDTH), st_c, st_n, st_m],
        out_shape=[jax.ShapeDtypeStruct((B, T, ML_WIDTH), BF16),
                   jax.ShapeDtypeStruct(c0.shape, F32),
                   jax.ShapeDtypeStruct(n0.shape, F32),
                   jax.ShapeDtypeStruct(m0.shape, F32)],
        compiler_params=pltpu.CompilerParams(dimension_semantics=("arbitrary", "arbitrary"),
                                             vmem_limit_bytes=VMEM_LIMIT),
        name="mlstm",
    )(qm, km, vm, og, gc, c0, n0, m0, nw)


def _sink_attention(q4, kw, vw, sink_col, bias):
    s = lax.dot_general(q4, kw, NT_DIMS, preferred_element_type=F32) * (SWA_HEAD_DIM ** -0.5)
    if bias is not None:
        s = s + bias
    mx = jnp.maximum(jnp.max(s, axis=1, keepdims=True), sink_col)
    p = jnp.exp(s - mx)
    den = jnp.sum(p, axis=1, keepdims=True) + jnp.exp(sink_col - mx)
    return jnp.dot(p.astype(BF16), vw, preferred_element_type=F32) / den


def _sink_column(sink_ref, g, rows):
    return jnp.concatenate(
        [jnp.broadcast_to(sink_ref[:, g * SWA_GROUP + j:g * SWA_GROUP + j + 1], (rows, 1))
         for j in range(SWA_GROUP)], axis=0)


def _swa_prompt_kernel(q_ref, kc_ref, kp_ref, vc_ref, vp_ref, sink_ref, bias_ref, o_ref):
    TQ = SWA_TQ
    q = q_ref[0]
    kcat = jnp.concatenate([kp_ref[0, TQ - WINDOW:, :], kc_ref[0]], axis=0).astype(BF16)
    vcat_t = jnp.concatenate([vp_ref[0, TQ - WINDOW:, :], vc_ref[0]], axis=0).T.astype(BF16)
    bias_t = bias_ref[0]
    outs = []
    for g in range(SWA_KV_HEADS):
        gs = slice(g * SWA_HEAD_DIM, (g + 1) * SWA_HEAD_DIM)
        heads = range(g * SWA_GROUP, (g + 1) * SWA_GROUP)
        q4 = jnp.concatenate([q[:, h * SWA_HEAD_DIM:(h + 1) * SWA_HEAD_DIM] for h in heads],
                             axis=0).astype(BF16)
        sink_row = jnp.concatenate([jnp.broadcast_to(sink_ref[:, h:h + 1], (1, TQ)) for h in heads], axis=1)
        s_t = lax.dot_general(kcat[:, gs], q4, NT_DIMS, preferred_element_type=F32)
        s_t = s_t * (SWA_HEAD_DIM ** -0.5) + bias_t
        mx = jnp.maximum(jnp.max(s_t, axis=0, keepdims=True), sink_row)
        p = jnp.exp(s_t - mx)
        den = jnp.sum(p, axis=0, keepdims=True) + jnp.exp(sink_row - mx)
        o_t = jnp.dot(vcat_t[gs, :], p.astype(BF16), preferred_element_type=F32) / den
        outs += [o_t[:, j * TQ:(j + 1) * TQ] for j in range(SWA_GROUP)]
    o_ref[0] = jnp.concatenate(outs, axis=0).T.astype(BF16)


def _swa_prompt_bias():
    TQ = SWA_TQ
    t = np.arange(SWA_GROUP * TQ) % TQ
    k = np.arange(WINDOW + TQ)
    qc = (t // SWA_CHUNK)[None, :]
    kc = (k // SWA_CHUNK)[:, None]
    visible = (kc >= qc) & (kc <= qc + WINDOW // SWA_CHUNK)
    first = visible & (k[:, None] >= WINDOW)
    return np.where(np.stack([first, visible]), 0.0, -np.inf).astype(np.float32)


def _swa_prompt(sq, sk, sv, sinks):
    B, T, _ = sq.shape
    TQ = SWA_TQ
    cur = lambda b, i: (b, i, 0)
    prev = lambda b, i: (b, jnp.maximum(i - 1, 0), 0)
    bias = jnp.asarray(_swa_prompt_bias())
    return pl.pallas_call(
        _swa_prompt_kernel,
        grid=(B, T // TQ),
        in_specs=[pl.BlockSpec((1, TQ, SWA_WIDTH), cur),
                  pl.BlockSpec((1, TQ, SWA_KV_WIDTH), cur), pl.BlockSpec((1, TQ, SWA_KV_WIDTH), prev),
                  pl.BlockSpec((1, TQ, SWA_KV_WIDTH), cur), pl.BlockSpec((1, TQ, SWA_KV_WIDTH), prev),
                  pl.BlockSpec((1, SWA_HEADS), lambda b, i: (0, 0)),
                  pl.BlockSpec((1,) + bias.shape[1:], lambda b, i: (jnp.minimum(i, 1), 0, 0))],
        out_specs=pl.BlockSpec((1, TQ, SWA_WIDTH), cur),
        out_shape=jax.ShapeDtypeStruct((B, T, SWA_WIDTH), BF16),
        compiler_params=pltpu.CompilerParams(dimension_semantics=("arbitrary", "arbitrary"),
                                             vmem_limit_bytes=VMEM_LIMIT),
        name="swa_prompt",
    )(sq, sk, sk, sv, sv, sinks, bias)


def _swa_sample_kernel(q_ref, kn_ref, vn_ref, kc_ref, vc_ref, sink_ref, o_ref, kw_ref, vw_ref):
    T = DEC_SEQ
    q = q_ref[0]
    k_all = jnp.concatenate([kc_ref[0], kn_ref[0]], axis=0)
    v_all = jnp.concatenate([vc_ref[0], vn_ref[0]], axis=0)
    kw_ref[0] = k_all[T:]
    vw_ref[0] = v_all[T:]
    kb, vb = k_all.astype(BF16), v_all.astype(BF16)
    for g in range(SWA_KV_HEADS):
        gs = slice(g * SWA_HEAD_DIM, (g + 1) * SWA_HEAD_DIM)
        q4 = jnp.concatenate(
            [q[:, (g * SWA_GROUP + j) * SWA_HEAD_DIM:(g * SWA_GROUP + j + 1) * SWA_HEAD_DIM]
             for j in range(SWA_GROUP)], axis=0).astype(BF16)
        o = _sink_attention(q4, kb[:, gs], vb[:, gs], _sink_column(sink_ref, g, T), None)
        for j in range(SWA_GROUP):
            hd = (g * SWA_GROUP + j) * SWA_HEAD_DIM
            o_ref[0, :, hd:hd + SWA_HEAD_DIM] = o[j * T:(j + 1) * T].astype(BF16)


def _swa_sample(sq, sk, sv, k_cache, v_cache, sinks):
    B, T, _ = sq.shape
    b3 = lambda b: (b, 0, 0)
    return pl.pallas_call(
        _swa_sample_kernel,
        grid=(B,),
        in_specs=[pl.BlockSpec((1, T, SWA_WIDTH), b3),
                  pl.BlockSpec((1, T, SWA_KV_WIDTH), b3), pl.BlockSpec((1, T, SWA_KV_WIDTH), b3),
                  pl.BlockSpec((1, WINDOW, SWA_KV_WIDTH), b3), pl.BlockSpec((1, WINDOW, SWA_KV_WIDTH), b3),
                  pl.BlockSpec((1, SWA_HEADS), lambda b: (0, 0))],
        out_specs=[pl.BlockSpec((1, T, SWA_WIDTH), b3),
                   pl.BlockSpec((1, WINDOW, SWA_KV_WIDTH), b3), pl.BlockSpec((1, WINDOW, SWA_KV_WIDTH), b3)],
        out_shape=[jax.ShapeDtypeStruct((B, T, SWA_WIDTH), BF16),
                   jax.ShapeDtypeStruct((B, WINDOW, SWA_KV_WIDTH), F32),
                   jax.ShapeDtypeStruct((B, WINDOW, SWA_KV_WIDTH), F32)],
        compiler_params=pltpu.CompilerParams(dimension_semantics=("arbitrary",),
                                             vmem_limit_bytes=VMEM_LIMIT),
        name="swa_sample",
    )(sq, sk, sv, k_cache, v_cache, sinks)


def _outproj_kernel(hm_ref, hs_ref, x_ref, wo_ref, nw_ref, wq_ref, x1_ref, h2_ref, q_ref):
    mix = (jnp.dot(hm_ref[...], wo_ref[:ML_WIDTH, :], preferred_element_type=F32)
           + jnp.dot(hs_ref[...], wo_ref[ML_WIDTH:, :], preferred_element_type=F32))
    x1 = x_ref[...] + mix
    x1_ref[...] = x1
    h2 = _rms(x1, nw_ref[...]).astype(BF16)
    h2_ref[...] = pltpu.bitcast(h2, jnp.uint32)
    q = jnp.dot(h2, wq_ref[...], preferred_element_type=F32)
    q_ref[...] = pltpu.bitcast(q.astype(BF16), jnp.uint32)


def _outproj(hm, hs, x2d, wo, nw, wq):
    T = x2d.shape[0]
    TM = TM_OUTPROJ
    QW = PEER_HEADS * PEER_KEY_DIM
    row = lambda i: (i, 0)
    const = lambda i: (0, 0)
    return pl.pallas_call(
        _outproj_kernel,
        grid=(T // TM,),
        in_specs=[pl.BlockSpec((TM, ML_WIDTH), row), pl.BlockSpec((TM, SWA_WIDTH), row),
                  pl.BlockSpec((TM, D_MODEL), row),
                  pl.BlockSpec((D_MODEL, D_MODEL), const), pl.BlockSpec((1, D_MODEL), const),
                  pl.BlockSpec((D_MODEL, QW), const)],
        out_specs=[pl.BlockSpec((TM, D_MODEL), row), pl.BlockSpec((TM // 2, D_MODEL), row),
                   pl.BlockSpec((TM // 2, QW), row)],
        out_shape=[jax.ShapeDtypeStruct((T, D_MODEL), F32),
                   jax.ShapeDtypeStruct((T // 2, D_MODEL), jnp.uint32),
                   jax.ShapeDtypeStruct((T // 2, QW), jnp.uint32)],
        compiler_params=pltpu.CompilerParams(dimension_semantics=("arbitrary",),
                                             vmem_limit_bytes=VMEM_LIMIT),
        name="outproj",
    )(hm, hs, x2d, wo, nw, wq)


_CAND_NB = [PEER_TOPK // (a + 1) for a in range(PEER_TOPK)]
_CAND_ROWS = 16 + 8 * 7 + 8


def _extract_top16(S, exact):
    R = S.shape[0]
    iota = lax.broadcasted_iota(jnp.int32, S.shape, 0)
    rank = jnp.full(S.shape, float(PEER_TOPK), F32)
    vals = []
    for r in range(PEER_TOPK):
        mx = jnp.max(S, axis=0, keepdims=True)
        if exact:
            idx = jnp.min(jnp.where(S == mx, iota, R), axis=0, keepdims=True)
            hit = iota == idx
        else:
            hit = S == mx
        rank = jnp.where(hit, float(r), rank)
        S = jnp.where(hit, -jnp.inf, S)
        vals.append(mx)
    return vals, rank


def _count_excess(flags):
    return jnp.abs(jnp.sum(flags, axis=0, keepdims=True) - float(PEER_TOPK))


def _route_tables(q, sk1_ref, sk2_ref, r2_ref, e2_ref, lim_ref, e1_ref, exact):
    N = q.shape[0]
    row8 = lax.broadcasted_iota(jnp.int32, (8, N), 0)
    rowc = lax.broadcasted_iota(jnp.int32, (_CAND_ROWS, N), 0)
    mid = rowc - 16
    flat = jnp.where(rowc < 16, rowc,
                     jnp.where(rowc < _CAND_ROWS - 8,
                               PEER_TOPK * ((mid >> 3) + 1) + (mid & 7),
                               PEER_TOPK * (rowc - (_CAND_ROWS - 16))))
    excess = jnp.zeros((1, N), F32)
    for h in range(PEER_HEADS):
        c0 = h * PEER_KEY_DIM
        s1 = lax.dot_general(sk1_ref[...], q[:, c0:c0 + PEER_HALF], NT_DIMS, preferred_element_type=F32)
        s2 = lax.dot_general(sk2_ref[...], q[:, c0 + PEER_HALF:c0 + PEER_KEY_DIM], NT_DIMS,
                             preferred_element_type=F32)
        t1, r1 = _extract_top16(s1, exact)
        t2, r2 = _extract_top16(s2, exact)
        t2_16 = jnp.concatenate(t2, axis=0)
        t2_8 = t2_16[:8]
        blocks = [t1[0] + t2_16, t1[1] + t2_8]
        for a in range(2, 8):
            blocks.append(jnp.where(row8 < _CAND_NB[a], t1[a] + t2_8, -jnp.inf))
        blocks.append(jnp.concatenate(t1[8:], axis=0) + t2[0])
        cand = jnp.concatenate(blocks, axis=0)
        sel = jnp.zeros(cand.shape, F32)
        work = cand
        for _ in range(PEER_TOPK):
            mx = jnp.max(work, axis=0, keepdims=True)
            if exact:
                idx = jnp.min(jnp.where(work == mx, flat, PEER_TOPK * PEER_TOPK), axis=0, keepdims=True)
                hit = flat == idx
            else:
                hit = work == mx
            sel = jnp.where(hit, 1.0, sel)
            work = jnp.where(hit, -jnp.inf, work)
        if not exact:
            excess = jnp.maximum(excess, _count_excess(jnp.where(r1 < float(PEER_TOPK), 1.0, 0.0)))
            excess = jnp.maximum(excess, _count_excess(jnp.where(r2 < float(PEER_TOPK), 1.0, 0.0)))
            excess = jnp.maximum(excess, _count_excess(sel))
        z = jnp.sum(jnp.where(sel > 0.0, jnp.exp(cand - cand[0:1]), 0.0), axis=0, keepdims=True)
        counts = [jnp.sum(sel[0:16], axis=0, keepdims=True)]
        for a in range(1, 8):
            counts.append(jnp.sum(sel[8 + 8 * a:16 + 8 * a], axis=0, keepdims=True))
        for a in range(8, PEER_TOPK):
            counts.append(sel[_CAND_ROWS - 16 + a:_CAND_ROWS - 15 + a])
        lim = jnp.zeros(r1.shape, F32)
        for a in range(PEER_TOPK):
            lim = jnp.where(r1 == float(a), counts[a], lim)
        r2_ref[h] = pltpu.bitcast(r2.astype(BF16), jnp.uint32)
        e2_ref[h] = pltpu.bitcast(jnp.exp(s2 - t2[0]).astype(BF16), jnp.uint32)
        lim_ref[h] = lim
        e1_ref[h] = jnp.exp(s1 - t1[0]) / z
    return excess


def _route_kernel(q_ref, sk1_ref, sk2_ref, r2_ref, e2_ref, lim_ref, e1_ref):
    q = pltpu.bitcast(q_ref[...], BF16)
    outs = (r2_ref, e2_ref, lim_ref, e1_ref)
    excess = _route_tables(q, sk1_ref, sk2_ref, *outs, exact=False)

    @pl.when(jnp.max(excess) > 0.0)
    def _():
        _route_tables(q, sk1_ref, sk2_ref, *outs, exact=True)


def _route(q, sk1, sk2):
    T = 2 * q.shape[0]
    TM = TM_ROUTE
    const = lambda i: (0, 0)
    tab = pl.BlockSpec((None, PEER_HEADS, N_KEYS, TM), lambda i: (i, 0, 0, 0))
    tab_packed = pl.BlockSpec((None, PEER_HEADS, N_KEYS // 2, TM), lambda i: (i, 0, 0, 0))
    tab_shape = jax.ShapeDtypeStruct((T // TM, PEER_HEADS, N_KEYS, TM), F32)
    tab_packed_shape = jax.ShapeDtypeStruct((T // TM, PEER_HEADS, N_KEYS // 2, TM), jnp.uint32)
    return pl.pallas_call(
        _route_kernel,
        grid=(T // TM,),
        in_specs=[pl.BlockSpec((TM // 2, PEER_HEADS * PEER_KEY_DIM), lambda i: (i, 0)),
                  pl.BlockSpec((N_KEYS, PEER_HALF), const), pl.BlockSpec((N_KEYS, PEER_HALF), const)],
        out_specs=[tab_packed, tab_packed, tab, tab],
        out_shape=[tab_packed_shape, tab_packed_shape, tab_shape, tab_shape],
        compiler_params=pltpu.CompilerParams(dimension_semantics=("arbitrary",),
                                             vmem_limit_bytes=VMEM_LIMIT),
        name="route",
    )(q, sk1, sk2)


def _peer_kernel(h2_ref, x1_ref, u_ref, vt_ref, r2_ref, e2_ref, lim_ref, e1_ref, nfw_ref,
                 y_ref, acc_ref, act0_ref, act1_ref, p0_ref, p1_ref, *, n_e, n_work):
    g = pl.program_id(0)
    per_blk = PEER_SUB * PEER_NSUB // N_KEYS
    e_b = jnp.clip(g - 1, 0, n_work - 1) % n_e
    e_c = jnp.clip(g - 2, 0, n_work - 1) % n_e
    act_bufs = (act0_ref, act1_ref)
    p_bufs = (p0_ref, p1_ref)

    @pl.when(g == 0)
    def _():
        for buf in act_bufs + p_bufs:
            buf[...] = jnp.zeros(buf.shape, buf.dtype)

    @pl.when(e_c == 0)
    def _():
        acc_ref[...] = jnp.zeros(acc_ref.shape, F32)

    def stages(slot_a, slot_b):
        def stage_a(s):
            h2 = pltpu.bitcast(h2_ref[...], BF16)
            u = pltpu.bitcast(u_ref[s * PEER_SUB // 2:(s + 1) * PEER_SUB // 2, :], BF16)
            act_bufs[slot_a][s * PEER_SUB:(s + 1) * PEER_SUB, :] = lax.dot_general(
                u, h2, NT_DIMS, preferred_element_type=F32)

        def stage_b(jbs):
            n_rb = N_KEYS // PEER_RB
            zero = jnp.zeros((PEER_RB, LANES), BF16)
            for jb in jbs:
                j = e_b * per_blk + jb
                for lh in range(TM_PEER // LANES):
                    cols = slice(lh * LANES, (lh + 1) * LANES)
                    gw = [None] * n_rb
                    for h in range(PEER_HEADS):
                        lim = jnp.broadcast_to(lim_ref[lh, h, pl.ds(j, 1), :], (PEER_RB, LANES)).astype(BF16)
                        e1 = jnp.broadcast_to(e1_ref[lh, h, pl.ds(j, 1), :], (PEER_RB, LANES)).astype(BF16)
                        for rb in range(n_rb):
                            words = slice(rb * PEER_RB // 2, (rb + 1) * PEER_RB // 2)
                            r2 = pltpu.bitcast(r2_ref[lh, h, words, :], BF16)
                            e2 = pltpu.bitcast(e2_ref[lh, h, words, :], BF16)
                            t = jnp.where(r2 < lim, e2, zero) * e1
                            gw[rb] = t if gw[rb] is None else gw[rb] + t
                    for rb in range(n_rb):
                        arows = slice(jb * N_KEYS + rb * PEER_RB, jb * N_KEYS + (rb + 1) * PEER_RB)
                        a = act_bufs[slot_b][arows, cols]
                        ga = 0.5 * a * (1.0 + lax.erf(a * np.float32(np.sqrt(0.5))))
                        p_bufs[slot_b][arows, cols] = gw[rb] * ga.astype(BF16)

        def stage_c():
            acc_ref[...] += jnp.dot(pltpu.bitcast(vt_ref[...], BF16), p_bufs[slot_a][...],
                                    preferred_element_type=F32)

        half = per_blk // 2
        stage_b(range(0, half))
        stage_a(0)
        stage_b(range(half, per_blk))
        stage_a(1)
        stage_c()

    @pl.when(g % 2 == 0)
    def _():
        stages(0, 1)

    @pl.when(g % 2 == 1)
    def _():
        stages(1, 0)

    @pl.when(jnp.logical_and(g >= 2, e_c == n_e - 1))
    def _():
        x = x1_ref[...] + acc_ref[...].T
        y_ref[...] = _rms(x, nfw_ref[...])


def _peer(h2, x1, u, vt, r2, e2, lim, e1, nfw):
    T = x1.shape[0]
    TM = TM_PEER
    ET = PEER_SUB * PEER_NSUB
    n_e = N_EXPERTS // ET
    n_work = (T // TM) * n_e
    item = lambda g, lag: jnp.clip(g - lag, 0, n_work - 1)
    rt = TM // TM_ROUTE
    tab = pl.BlockSpec((rt, PEER_HEADS, N_KEYS, TM_ROUTE), lambda g: (item(g, 1) // n_e, 0, 0, 0))
    tab_packed = pl.BlockSpec((rt, PEER_HEADS, N_KEYS // 2, TM_ROUTE),
                              lambda g: (item(g, 1) // n_e, 0, 0, 0))
    tok_c = lambda g: (item(g, 2) // n_e, 0)
    return pl.pallas_call(
        functools.partial(_peer_kernel, n_e=n_e, n_work=n_work),
        grid=(n_work + 2,),
        in_specs=[pl.BlockSpec((TM // 2, D_MODEL), lambda g: (item(g, 0) // n_e, 0)),
                  pl.BlockSpec((TM, D_MODEL), tok_c),
                  pl.BlockSpec((ET // 2, D_MODEL), lambda g: (item(g, 0) % n_e, 0)),
                  pl.BlockSpec((D_MODEL // 2, ET), lambda g: (0, item(g, 2) % n_e)),
                  tab_packed, tab_packed, tab, tab,
                  pl.BlockSpec((1, D_MODEL), lambda g: (0, 0))],
        out_specs=pl.BlockSpec((TM, D_MODEL), tok_c),
        out_shape=jax.ShapeDtypeStruct((T, D_MODEL), F32),
        scratch_shapes=[pltpu.VMEM((D_MODEL, TM), F32),
                        pltpu.VMEM((ET, TM), F32), pltpu.VMEM((ET, TM), F32),
                        pltpu.VMEM((ET, TM), BF16), pltpu.VMEM((ET, TM), BF16)],
        compiler_params=pltpu.CompilerParams(dimension_semantics=("arbitrary",),
                                             vmem_limit_bytes=VMEM_LIMIT),
        name="peer",
    )(h2, x1, u, vt, r2, e2, lim, e1, nfw)


def _rope_tables(pos):
    inv = ROPE_THETA ** (-jnp.arange(ROPE_HALF, dtype=F32) * 2.0 / ROPE_DIM)
    ang = pos.astype(F32)[:, None] * inv[None, :]
    cos, sin = jnp.cos(ang), jnp.sin(ang)
    n = pos.shape[0]
    rest = SWA_HEAD_DIM - ROPE_DIM
    zh = jnp.zeros((n, ROPE_HALF), F32)
    cos_h = jnp.concatenate([cos, cos, jnp.ones((n, rest), F32)], axis=1)
    sina_h = jnp.concatenate([-sin, zh, jnp.zeros((n, rest), F32)], axis=1)
    sinb_h = jnp.concatenate([zh, sin, jnp.zeros((n, rest), F32)], axis=1)
    rep = LANES // SWA_HEAD_DIM
    return tuple(jnp.tile(t, (1, rep)) for t in (cos_h, sina_h, sinb_h))


def _pack_kernel(x_ref, o_ref, *, transpose):
    x = x_ref[...]
    if transpose:
        x = x.T
    o_ref[...] = pltpu.bitcast(x.astype(BF16), jnp.uint32)


def _pack_expert_table(w, *, transpose):
    n, d = w.shape
    rows = PACK_ROWS
    if transpose:
        out_spec = pl.BlockSpec((d // 2, rows), lambda i: (0, i))
        out_shape = jax.ShapeDtypeStruct((d // 2, n), jnp.uint32)
    else:
        out_spec = pl.BlockSpec((rows // 2, d), lambda i: (i, 0))
        out_shape = jax.ShapeDtypeStruct((n // 2, d), jnp.uint32)
    return pl.pallas_call(
        functools.partial(_pack_kernel, transpose=transpose),
        grid=(n // rows,),
        in_specs=[pl.BlockSpec((rows, d), lambda i: (i, 0))],
        out_specs=out_spec,
        out_shape=out_shape,
        compiler_params=pltpu.CompilerParams(dimension_semantics=("arbitrary",),
                                             vmem_limit_bytes=VMEM_LIMIT),
        name="pack_vt" if transpose else "pack_u",
    )(w)


def _layer_tokens(x2d, tables, tab_map, W):
    return _inproj(x2d, W["norm_mix"], W["w_in"], W["bias"], *tables, tab_map)


def _ffn(hm, hs, x2d, W):
    x1, h2, q = _outproj(hm, hs, x2d, W["w_out"], W["norm_ffn"], W["w_q"])
    r2, e2, lim, e1 = _route(q, W["sk1"], W["sk2"])
    return _peer(h2, x1, W["u"], W["vt"], r2, e2, lim, e1, W["norm_final"])


def kernel(x_prompt, x_sample, cache_swa_k, cache_swa_v, state_mlstm_c, state_mlstm_n, state_mlstm_m,
           norm_mix_w, w_in, mlstm_if_bias, mlstm_norm_w, swa_sinks, w_out, norm_ffn_w,
           peer_w_q, peer_sub_keys_1, peer_sub_keys_2, peer_u, peer_v, norm_final_w):
    B, S, _ = x_prompt.shape
    DB, DS, _ = x_sample.shape
    l = 0
    wi = w_in[l]
    s_q = 4 * ML_WIDTH + 2 * ML_HEADS
    w_perm = jnp.concatenate(
        [wi[:, :4 * ML_WIDTH], wi[:, s_q:], wi[:, 4 * ML_WIDTH:s_q],
         jnp.zeros((D_MODEL, LANES - 2 * ML_HEADS), F32)], axis=1).astype(BF16)
    bias_pad = jnp.concatenate([mlstm_if_bias[l], jnp.zeros((LANES - 2 * ML_HEADS,), F32)])[None, :]
    W = {
        "norm_mix": norm_mix_w[l][None, :],
        "w_in": w_perm,
        "bias": bias_pad,
        "w_out": w_out[l].astype(BF16),
        "norm_ffn": norm_ffn_w[l][None, :],
        "w_q": peer_w_q[l].astype(BF16),
        "sk1": peer_sub_keys_1[l].astype(BF16),
        "sk2": peer_sub_keys_2[l].astype(BF16),
        "u": _pack_expert_table(peer_u[l], transpose=False),
        "vt": _pack_expert_table(peer_v[l], transpose=True),
        "norm_final": norm_final_w[None, :],
    }
    ml_nw = mlstm_norm_w[l][None, :]
    sinks = swa_sinks[l][None, :]

    xp = x_prompt.reshape(B * S, D_MODEL)
    tiles_per_seq = S // TM_INPROJ
    tabs_p = _rope_tables(jnp.arange(S, dtype=jnp.int32))
    qm, km, vm, og, sq, sk, sv, gc = _layer_tokens(xp, tabs_p, lambda i: (i % tiles_per_seq, 0), W)
    r3 = lambda a: a.reshape(B, S, a.shape[-1])
    zc = jnp.zeros((B, ML_HEADS, ML_HEAD_DIM, ML_HEAD_DIM), F32)
    zn = jnp.zeros((B, ML_HEADS, ML_HEAD_DIM), F32)
    zm = jnp.zeros((B, 1, LANES), F32)
    hm_p, c_p, n_p, m_p = _mlstm(r3(qm), r3(km), r3(vm), r3(og), r3(gc), zc, zn, zm, ml_nw, L_in=ML_CHUNK)
    sk3, sv3 = r3(sk), r3(sv)
    hs_p = _swa_prompt(r3(sq), sk3, sv3, sinks)
    y_p = _ffn(hm_p.reshape(B * S, ML_WIDTH), hs_p.reshape(B * S, SWA_WIDTH), xp, W)
    kv_shape = (1, B, WINDOW, SWA_KV_HEADS, SWA_HEAD_DIM)
    k_win_p = sk3[:, S - WINDOW:].reshape(kv_shape)
    v_win_p = sv3[:, S - WINDOW:].reshape(kv_shape)

    xs = x_sample.reshape(DB * DS, D_MODEL)
    pos_s = PAST_LEN + jnp.arange(DS, dtype=jnp.int32)
    tabs_s = tuple(jnp.tile(t, (DB, 1)) for t in _rope_tables(pos_s))
    qm, km, vm, og, sq, sk, sv, gc = _layer_tokens(xs, tabs_s, lambda i: (i, 0), W)
    r3s = lambda a: a.reshape(DB, DS, a.shape[-1])
    m0 = jnp.concatenate([state_mlstm_m[l], jnp.zeros((DB, LANES - ML_HEADS), F32)], axis=1)[:, None, :]
    hm_s, c_s, n_s, m_s = _mlstm(r3s(qm), r3s(km), r3s(vm), r3s(og), r3s(gc),
                                 state_mlstm_c[l], state_mlstm_n[l], m0, ml_nw, L_in=DS)
    kc = cache_swa_k[l].reshape(DB, WINDOW, SWA_KV_WIDTH)
    vc = cache_swa_v[l].reshape(DB, WINDOW, SWA_KV_WIDTH)
    hs_s, k_win_s, v_win_s = _swa_sample(r3s(sq), r3s(sk), r3s(sv), kc, vc, sinks)
    y_s = _ffn(hm_s.reshape(DB * DS, ML_WIDTH), hs_s.reshape(DB * DS, SWA_WIDTH), xs, W)
    kv_shape_s = (1, DB, WINDOW, SWA_KV_HEADS, SWA_HEAD_DIM)

    return (y_p.reshape(B, S, D_MODEL), y_s.reshape(DB, DS, D_MODEL),
            k_win_p, v_win_p, c_p[None], n_p[None], m_p[None, :, 0, :ML_HEADS],
            k_win_s.reshape(kv_shape_s), v_win_s.reshape(kv_shape_s),
            c_s[None], n_s[None], m_s[None, :, 0, :ML_HEADS])
```

```python
import functools

import jax
import jax.numpy as jnp
import numpy as np
from jax import lax
from jax.experimental import pallas as pl
from jax.experimental.pallas import tpu as pltpu

F32 = jnp.float32
BF16 = jnp.bfloat16

D_MODEL = 1024
SEQ = 8192
DEC_SEQ = 32
PAST_LEN = 4096
NORM_EPS = 1e-6
ML_HEADS = 4
ML_HEAD_DIM = 128
ML_WIDTH = ML_HEADS * ML_HEAD_DIM
SWA_HEADS = 8
SWA_KV_HEADS = 2
SWA_GROUP = SWA_HEADS // SWA_KV_HEADS
SWA_HEAD_DIM = 64
SWA_WIDTH = SWA_HEADS * SWA_HEAD_DIM
SWA_KV_WIDTH = SWA_KV_HEADS * SWA_HEAD_DIM
WINDOW = 128
SWA_CHUNK = 64
ROPE_THETA = 500000.0
ROPE_DIM = SWA_HEAD_DIM // 4
ROPE_HALF = ROPE_DIM // 2
PEER_HEADS = 8
N_KEYS = 128
N_EXPERTS = N_KEYS * N_KEYS
PEER_TOPK = 16
PEER_KEY_DIM = 256
PEER_HALF = PEER_KEY_DIM // 2

LANES = 128
VMEM_LIMIT = 52 * 1024 * 1024

COL_MQ, COL_MK, COL_MV, COL_MO = 0, ML_WIDTH, 2 * ML_WIDTH, 3 * ML_WIDTH
COL_SQ = 4 * ML_WIDTH
COL_SK = COL_SQ + SWA_WIDTH
COL_SV = COL_SK + SWA_KV_WIDTH
COL_G = COL_SV + SWA_KV_WIDTH
IN_COLS_PAD = COL_G + LANES

TM_INPROJ = 256
ML_CHUNK = 128
SWA_TQ = 256
TM_OUTPROJ = 256
TM_ROUTE = 128
TM_PEER = 512
PEER_SUB = 512
PEER_NSUB = 2
PEER_RB = 16
PACK_ROWS = 512

NT_DIMS = (((1,), (1,)), ((), ()))
TN_DIMS = (((0,), (0,)), ((), ()))


def _rms(x, w):
    return x * lax.rsqrt(jnp.mean(x * x, axis=-1, keepdims=True) + NORM_EPS) * w


def _inproj_kernel(x_ref, nw_ref, w_ref, bias_ref, cos_ref, sina_ref, sinb_ref,
                   qm_ref, km_ref, vm_ref, og_ref, sq_ref, sk_ref, sv_ref, gc_ref):
    h = _rms(x_ref[...], nw_ref[...])
    proj = jnp.dot(h.astype(BF16), w_ref[...], preferred_element_type=F32)
    qm_ref[...] = proj[:, COL_MQ:COL_MQ + ML_WIDTH]
    km_ref[...] = proj[:, COL_MK:COL_MK + ML_WIDTH] * (ML_HEAD_DIM ** -0.5)
    vm_ref[...] = proj[:, COL_MV:COL_MV + ML_WIDTH]
    og_ref[...] = jax.nn.sigmoid(proj[:, COL_MO:COL_MO + ML_WIDTH])
    cosf, sina, sinb = cos_ref[...], sina_ref[...], sinb_ref[...]

    def rope(xc):
        return (xc * cosf + pltpu.roll(xc, LANES - ROPE_HALF, 1) * sina
                + pltpu.roll(xc, ROPE_HALF, 1) * sinb)

    for j in range(SWA_WIDTH // LANES):
        sq_ref[:, j * LANES:(j + 1) * LANES] = rope(proj[:, COL_SQ + j * LANES:COL_SQ + (j + 1) * LANES])
    sk_ref[...] = rope(proj[:, COL_SK:COL_SK + LANES])
    sv_ref[...] = proj[:, COL_SV:COL_SV + LANES]
    g = proj[:, COL_G:COL_G + LANES] + bias_ref[...]
    lane = lax.broadcasted_iota(jnp.int32, g.shape, 1)
    gc_ref[...] = jnp.where(lane < ML_HEADS, g, jax.nn.log_sigmoid(g))


def _inproj(x2d, nw, w_perm, bias_pad, cos_t, sina_t, sinb_t, tab_map):
    T = x2d.shape[0]
    TM = TM_INPROJ
    row = lambda i: (i, 0)
    const = lambda i: (0, 0)
    f = lambda n: jax.ShapeDtypeStruct((T, n), F32)
    return pl.pallas_call(
        _inproj_kernel,
        grid=(T // TM,),
        in_specs=[pl.BlockSpec((TM, D_MODEL), row),
                  pl.BlockSpec((1, D_MODEL), const),
                  pl.BlockSpec((D_MODEL, IN_COLS_PAD), const),
                  pl.BlockSpec((1, LANES), const),
                  pl.BlockSpec((TM, LANES), tab_map),
                  pl.BlockSpec((TM, LANES), tab_map),
                  pl.BlockSpec((TM, LANES), tab_map)],
        out_specs=[pl.BlockSpec((TM, ML_WIDTH), row)] * 4
                  + [pl.BlockSpec((TM, SWA_WIDTH), row),
                     pl.BlockSpec((TM, LANES), row), pl.BlockSpec((TM, LANES), row),
                     pl.BlockSpec((TM, LANES), row)],
        out_shape=[f(ML_WIDTH)] * 4 + [f(SWA_WIDTH), f(LANES), f(LANES), f(LANES)],
        compiler_params=pltpu.CompilerParams(dimension_semantics=("arbitrary",),
                                             vmem_limit_bytes=VMEM_LIMIT),
        name="inproj",
    )(x2d, nw, w_perm, bias_pad, cos_t, sina_t, sinb_t)


def _mlstm_kernel(q_ref, k_ref, v_ref, og_ref, gc_ref, c0_ref, n0_ref, m0_ref, nw_ref,
                  hm_ref, c_ref, n_ref, m_ref, *, BB, L_in):
    LP = ML_CHUNK

    @pl.when(pl.program_id(1) == 0)
    def _():
        c_ref[...] = c0_ref[...]
        n_ref[...] = n0_ref[...]
        m_ref[...] = m0_ref[...]

    src = lax.broadcasted_iota(jnp.int32, (LP, LP), 0)
    qry = lax.broadcasted_iota(jnp.int32, (LP, LP), 1)
    visible = src <= qry
    tri_t = visible.astype(F32)
    row8 = lax.broadcasted_iota(jnp.int32, (8, LP), 0)
    lane1 = lax.broadcasted_iota(jnp.int32, (1, LANES), 1)
    hi = lax.Precision.HIGHEST

    def pad_rows(a):
        if L_in == LP:
            return a
        return jnp.concatenate([a, jnp.zeros((LP - L_in, a.shape[1]), a.dtype)], axis=0)

    streams = [(b, h) for b in range(BB) for h in range(ML_HEADS)]
    gr, b_r, m_all, qkv = {}, {}, {}, {}
    for b in range(BB):
        gc = gc_ref[b]
        if L_in < LP:
            lane_pad = lax.broadcasted_iota(jnp.int32, (LP - L_in, LANES), 1)
            fill = jnp.where(lane_pad < ML_HEADS, -jnp.inf, 0.0).astype(F32)
            gc = jnp.concatenate([gc, fill], axis=0)
        gr[b] = gc.T[:8]
        lf_r = jnp.where(row8 >= ML_HEADS, gr[b], 0.0)
        b_r[b] = jnp.dot(lf_r, tri_t, precision=hi, preferred_element_type=F32)
        m_all[b] = m_ref[b]
        qkv[b] = (pad_rows(q_ref[b]), pad_rows(k_ref[b]), pad_rows(v_ref[b]))

    st = {}
    for b, h in streams:
        sl = slice(h * ML_HEAD_DIM, (h + 1) * ML_HEAD_DIM)
        q, k, v = (a[:, sl] for a in qkv[b])
        ig_r = gr[b][h:h + 1, :]
        b_rh = b_r[b][ML_HEADS + h:ML_HEADS + h + 1, :]
        m_prev = m_all[b][:, h:h + 1]
        src_term = jnp.broadcast_to(ig_r - b_rh, (LP, LP)).T
        logd = jnp.where(visible, b_rh + src_term, -jnp.inf)
        m_inter = b_rh + m_prev
        m_t = jnp.maximum(m_inter, jnp.max(logd, axis=0, keepdims=True))
        b_last = b_rh[:, LP - 1:LP]
        logw = b_last - b_rh + ig_r
        m_new = jnp.maximum(b_last + m_prev, jnp.max(logw, axis=1, keepdims=True))
        st[b, h] = dict(sl=sl, q=q, k=k, qb=q.astype(BF16), kb=k.astype(BF16), v_t=v.T, m_t=m_t,
                        dmat=jnp.exp(logd - m_t), w_int=jnp.exp(m_inter - m_t), m_new=m_new,
                        w_r=jnp.exp(logw - m_new), decay=jnp.exp(b_last + m_prev - m_new),
                        c_old=c_ref[b, h], n_old=n_ref[b, h:h + 1, :])

    for key in streams:
        d = st[key]
        d["s"] = lax.dot_general(d["kb"], d["qb"], NT_DIMS, preferred_element_type=F32)
        d["cq"] = lax.dot_general(d["c_old"].astype(BF16), d["qb"], NT_DIMS, preferred_element_type=F32)
        d["nq"] = lax.dot_general(jnp.broadcast_to(d["n_old"], (8, ML_HEAD_DIM)), d["q"], NT_DIMS,
                                  precision=hi, preferred_element_type=F32)[0:1]

    for b, h in streams:
        d = st[b, h]
        c_ref[b, h] = d["decay"] * d["c_old"] + jnp.dot(
            (d["v_t"] * d["w_r"]).astype(BF16), d["kb"], preferred_element_type=F32)
        n_ref[b, h:h + 1, :] = d["decay"] * d["n_old"] + jnp.dot(
            jnp.broadcast_to(d["w_r"], (8, LP)), d["k"], precision=hi, preferred_element_type=F32)[0:1]

    for key in streams:
        d = st[key]
        qk = d["s"] * d["dmat"]
        num = d["w_int"] * d["cq"] + jnp.dot(d["v_t"].astype(BF16), qk.astype(BF16),
                                             preferred_element_type=F32)
        den = d["w_int"] * d["nq"] + jnp.sum(qk, axis=0, keepdims=True)
        d["hh"] = num / jnp.maximum(jnp.abs(den), jnp.exp(-d["m_t"]))

    for b in range(BB):
        og_all = og_ref[b]
        m_out = m_all[b]
        for h in range(ML_HEADS):
            d = st[b, h]
            hh, sl = d["hh"], d["sl"]
            y_t = hh * lax.rsqrt(jnp.mean(hh * hh, axis=0, keepdims=True) + NORM_EPS)
            y = y_t.T * nw_ref[:, sl]
            hm_ref[b, :, sl] = (og_all[:, sl] * y[:L_in]).astype(BF16)
            m_out = jnp.where(lane1 == h, d["m_new"], m_out)
        m_ref[b] = m_out


def _mlstm(qm, km, vm, og, gc, c0, n0, m0, nw, *, L_in):
    B, T, _ = qm.shape
    BB = 2
    nchunks = T // L_in
    tok = lambda n: pl.BlockSpec((BB, L_in, n), lambda g, c: (g, c, 0))
    st_c = pl.BlockSpec((BB, ML_HEADS, ML_HEAD_DIM, ML_HEAD_DIM), lambda g, c: (g, 0, 0, 0))
    st_n = pl.BlockSpec((BB, ML_HEADS, ML_HEAD_DIM), lambda g, c: (g, 0, 0))
    st_m = pl.BlockSpec((BB, 1, LANES), lambda g, c: (g, 0, 0))
    return pl.pallas_call(
        functools.partial(_mlstm_kernel, BB=BB, L_in=L_in),
        grid=(B // BB, nchunks),
        in_specs=[tok(ML_WIDTH), tok(ML_WIDTH), tok(ML_WIDTH), tok(ML_WIDTH), tok(LANES),
                  st_c, st_n, st_m, pl.BlockSpec((1, ML_WIDTH), lambda g, c: (0, 0))],
        out_specs=[tok(ML_WIDTH), st_c, st_n, st_m],
        out_shape=[jax.ShapeDtypeStruct((B, T, ML_WIDTH), BF16),
                   jax.ShapeDtypeStruct(c0.shape, F32),
                   jax.ShapeDtypeStruct(n0.shape, F32),
                   jax.ShapeDtypeStruct(m0.shape, F32)],
        compiler_params=pltpu.CompilerParams(dimension_semantics=("arbitrary", "arbitrary"),
                                             vmem_limit_bytes=VMEM_LIMIT),
        name="mlstm",
    )(qm, km, vm, og, gc, c0, n0, m0, nw)


def _sink_attention(q4, kw, vw, sink_col, bias):
    s = lax.dot_general(q4, kw, NT_DIMS, preferred_element_type=F32) * (SWA_HEAD_DIM ** -0.5)
    if bias is not None:
        s = s + bias
    mx = jnp.maximum(jnp.max(s, axis=1, keepdims=True), sink_col)
    p = jnp.exp(s - mx)
    den = jnp.sum(p, axis=1, keepdims=True) + jnp.exp(sink_col - mx)
    return jnp.dot(p.astype(BF16), vw, preferred_element_type=F32) / den


def _sink_column(sink_ref, g, rows):
    return jnp.concatenate(
        [jnp.broadcast_to(sink_ref[:, g * SWA_GROUP + j:g * SWA_GROUP + j + 1], (rows, 1))
         for j in range(SWA_GROUP)], axis=0)


def _swa_prompt_kernel(q_ref, kc_ref, kp_ref, vc_ref, vp_ref, sink_ref, bias_ref, o_ref):
    TQ = SWA_TQ
    q = q_ref[0]
    kcat = jnp.concatenate([kp_ref[0, TQ - WINDOW:, :], kc_ref[0]], axis=0).astype(BF16)
    vcat_t = jnp.concatenate([vp_ref[0, TQ - WINDOW:, :], vc_ref[0]], axis=0).T.astype(BF16)
    bias_t = bias_ref[0]
    outs = []
    for g in range(SWA_KV_HEADS):
        gs = slice(g * SWA_HEAD_DIM, (g + 1) * SWA_HEAD_DIM)
        heads = range(g * SWA_GROUP, (g + 1) * SWA_GROUP)
        q4 = jnp.concatenate([q[:, h * SWA_HEAD_DIM:(h + 1) * SWA_HEAD_DIM] for h in heads],
                             axis=0).astype(BF16)
        sink_row = jnp.concatenate([jnp.broadcast_to(sink_ref[:, h:h + 1], (1, TQ)) for h in heads], axis=1)
        s_t = lax.dot_general(kcat[:, gs], q4, NT_DIMS, preferred_element_type=F32)
        s_t = s_t * (SWA_HEAD_DIM ** -0.5) + bias_t
        mx = jnp.maximum(jnp.max(s_t, axis=0, keepdims=True), sink_row)
        p = jnp.exp(s_t - mx)
        den = jnp.sum(p, axis=0, keepdims=True) + jnp.exp(sink_row - mx)
        o_t = jnp.dot(vcat_t[gs, :], p.astype(BF16), preferred_element_type=F32) / den
        outs += [o_t[:, j * TQ:(j + 1) * TQ] for j in range(SWA_GROUP)]
    o_ref[0] = jnp.concatenate(outs, axis=0).T.astype(BF16)


def _swa_prompt_bias():
    TQ = SWA_TQ
    t = np.arange(SWA_GROUP * TQ) % TQ
    k = np.arange(WINDOW + TQ)
    qc = (t // SWA_CHUNK)[None, :]
    kc = (k // SWA_CHUNK)[:, None]
    visible = (kc >= qc) & (kc <= qc + WINDOW // SWA_CHUNK)
    first = visible & (k[:, None] >= WINDOW)
    return np.where(np.stack([first, visible]), 0.0, -np.inf).astype(np.float32)


def _swa_prompt(sq, sk, sv, sinks):
    B, T, _ = sq.shape
    TQ = SWA_TQ
    cur = lambda b, i: (b, i, 0)
    prev = lambda b, i: (b, jnp.maximum(i - 1, 0), 0)
    bias = jnp.asarray(_swa_prompt_bias())
    return pl.pallas_call(
        _swa_prompt_kernel,
        grid=(B, T // TQ),
        in_specs=[pl.BlockSpec((1, TQ, SWA_WIDTH), cur),
                  pl.BlockSpec((1, TQ, SWA_KV_WIDTH), cur), pl.BlockSpec((1, TQ, SWA_KV_WIDTH), prev),
                  pl.BlockSpec((1, TQ, SWA_KV_WIDTH), cur), pl.BlockSpec((1, TQ, SWA_KV_WIDTH), prev),
                  pl.BlockSpec((1, SWA_HEADS), lambda b, i: (0, 0)),
                  pl.BlockSpec((1,) + bias.shape[1:], lambda b, i: (jnp.minimum(i, 1), 0, 0))],
        out_specs=pl.BlockSpec((1, TQ, SWA_WIDTH), cur),
        out_shape=jax.ShapeDtypeStruct((B, T, SWA_WIDTH), BF16),
        compiler_params=pltpu.CompilerParams(dimension_semantics=("arbitrary", "arbitrary"),
                                             vmem_limit_bytes=VMEM_LIMIT),
        name="swa_prompt",
    )(sq, sk, sk, sv, sv, sinks, bias)


def _swa_sample_kernel(q_ref, kn_ref, vn_ref, kc_ref, vc_ref, sink_ref, o_ref, kw_ref, vw_ref):
    T = DEC_SEQ
    q = q_ref[0]
    k_all = jnp.concatenate([kc_ref[0], kn_ref[0]], axis=0)
    v_all = jnp.concatenate([vc_ref[0], vn_ref[0]], axis=0)
    kw_ref[0] = k_all[T:]
    vw_ref[0] = v_all[T:]
    kb, vb = k_all.astype(BF16), v_all.astype(BF16)
    for g in range(SWA_KV_HEADS):
        gs = slice(g * SWA_HEAD_DIM, (g + 1) * SWA_HEAD_DIM)
        q4 = jnp.concatenate(
            [q[:, (g * SWA_GROUP + j) * SWA_HEAD_DIM:(g * SWA_GROUP + j + 1) * SWA_HEAD_DIM]
             for j in range(SWA_GROUP)], axis=0).astype(BF16)
        o = _sink_attention(q4, kb[:, gs], vb[:, gs], _sink_column(sink_ref, g, T), None)
        for j in range(SWA_GROUP):
            hd = (g * SWA_GROUP + j) * SWA_HEAD_DIM
            o_ref[0, :, hd:hd + SWA_HEAD_DIM] = o[j * T:(j + 1) * T].astype(BF16)


def _swa_sample(sq, sk, sv, k_cache, v_cache, sinks):
    B, T, _ = sq.shape
    b3 = lambda b: (b, 0, 0)
    return pl.pallas_call(
        _swa_sample_kernel,
        grid=(B,),
        in_specs=[pl.BlockSpec((1, T, SWA_WIDTH), b3),
                  pl.BlockSpec((1, T, SWA_KV_WIDTH), b3), pl.BlockSpec((1, T, SWA_KV_WIDTH), b3),
                  pl.BlockSpec((1, WINDOW, SWA_KV_WIDTH), b3), pl.BlockSpec((1, WINDOW, SWA_KV_WIDTH), b3),
                  pl.BlockSpec((1, SWA_HEADS), lambda b: (0, 0))],
        out_specs=[pl.BlockSpec((1, T, SWA_WIDTH), b3),
                   pl.BlockSpec((1, WINDOW, SWA_KV_WIDTH), b3), pl.BlockSpec((1, WINDOW, SWA_KV_WIDTH), b3)],
        out_shape=[jax.ShapeDtypeStruct((B, T, SWA_WIDTH), BF16),
                   jax.ShapeDtypeStruct((B, WINDOW, SWA_KV_WIDTH), F32),
                   jax.ShapeDtypeStruct((B, WINDOW, SWA_KV_WIDTH), F32)],
        compiler_params=pltpu.CompilerParams(dimension_semantics=("arbitrary",),
                                             vmem_limit_bytes=VMEM_LIMIT),
        name="swa_sample",
    )(sq, sk, sv, k_cache, v_cache, sinks)


def _outproj_kernel(hm_ref, hs_ref, x_ref, wo_ref, nw_ref, wq_ref, x1_ref, h2_ref, q_ref):
    mix = (jnp.dot(hm_ref[...], wo_ref[:ML_WIDTH, :], preferred_element_type=F32)
           + jnp.dot(hs_ref[...], wo_ref[ML_WIDTH:, :], preferred_element_type=F32))
    x1 = x_ref[...] + mix
    x1_ref[...] = x1
    h2 = _rms(x1, nw_ref[...]).astype(BF16)
    h2_ref[...] = pltpu.bitcast(h2, jnp.uint32)
    q = jnp.dot(h2, wq_ref[...], preferred_element_type=F32)
    q_ref[...] = pltpu.bitcast(q.astype(BF16), jnp.uint32)


def _outproj(hm, hs, x2d, wo, nw, wq):
    T = x2d.shape[0]
    TM = TM_OUTPROJ
    QW = PEER_HEADS * PEER_KEY_DIM
    row = lambda i: (i, 0)
    const = lambda i: (0, 0)
    return pl.pallas_call(
        _outproj_kernel,
        grid=(T // TM,),
        in_specs=[pl.BlockSpec((TM, ML_WIDTH), row), pl.BlockSpec((TM, SWA_WIDTH), row),
                  pl.BlockSpec((TM, D_MODEL), row),
                  pl.BlockSpec((D_MODEL, D_MODEL), const), pl.BlockSpec((1, D_MODEL), const),
                  pl.BlockSpec((D_MODEL, QW), const)],
        out_specs=[pl.BlockSpec((TM, D_MODEL), row), pl.BlockSpec((TM // 2, D_MODEL), row),
                   pl.BlockSpec((TM // 2, QW), row)],
        out_shape=[jax.ShapeDtypeStruct((T, D_MODEL), F32),
                   jax.ShapeDtypeStruct((T // 2, D_MODEL), jnp.uint32),
                   jax.ShapeDtypeStruct((T // 2, QW), jnp.uint32)],
        compiler_params=pltpu.CompilerParams(dimension_semantics=("arbitrary",),
                                             vmem_limit_bytes=VMEM_LIMIT),
        name="outproj",
    )(hm, hs, x2d, wo, nw, wq)


_CAND_NB = [PEER_TOPK // (a + 1) for a in range(PEER_TOPK)]
_CAND_ROWS = 16 + 8 * 7 + 8


def _extract_top16(S, exact):
    R = S.shape[0]
    iota = lax.broadcasted_iota(jnp.int32, S.shape, 0)
    rank = jnp.full(S.shape, float(PEER_TOPK), F32)
    vals = []
    for r in range(PEER_TOPK):
        mx = jnp.max(S, axis=0, keepdims=True)
        if exact:
            idx = jnp.min(jnp.where(S == mx, iota, R), axis=0, keepdims=True)
            hit = iota == idx
        else:
            hit = S == mx
        rank = jnp.where(hit, float(r), rank)
        S = jnp.where(hit, -jnp.inf, S)
        vals.append(mx)
    return vals, rank


def _count_excess(flags):
    return jnp.abs(jnp.sum(flags, axis=0, keepdims=True) - float(PEER_TOPK))


def _route_tables(q, sk1_ref, sk2_ref, r2_ref, e2_ref, lim_ref, e1_ref, exact):
    N = q.shape[0]
    row8 = lax.broadcasted_iota(jnp.int32, (8, N), 0)
    rowc = lax.broadcasted_iota(jnp.int32, (_CAND_ROWS, N), 0)
    mid = rowc - 16
    flat = jnp.where(rowc < 16, rowc,
                     jnp.where(rowc < _CAND_ROWS - 8,
                               PEER_TOPK * ((mid >> 3) + 1) + (mid & 7),
                               PEER_TOPK * (rowc - (_CAND_ROWS - 16))))
    excess = jnp.zeros((1, N), F32)
    for h in range(PEER_HEADS):
        c0 = h * PEER_KEY_DIM
        s1 = lax.dot_general(sk1_ref[...], q[:, c0:c0 + PEER_HALF], NT_DIMS, preferred_element_type=F32)
        s2 = lax.dot_general(sk2_ref[...], q[:, c0 + PEER_HALF:c0 + PEER_KEY_DIM], NT_DIMS,
                             preferred_element_type=F32)
        t1, r1 = _extract_top16(s1, exact)
        t2, r2 = _extract_top16(s2, exact)
        t2_16 = jnp.concatenate(t2, axis=0)
        t2_8 = t2_16[:8]
        blocks = [t1[0] + t2_16, t1[1] + t2_8]
        for a in range(2, 8):
            blocks.append(jnp.where(row8 < _CAND_NB[a], t1[a] + t2_8, -jnp.inf))
        blocks.append(jnp.concatenate(t1[8:], axis=0) + t2[0])
        cand = jnp.concatenate(blocks, axis=0)
        sel = jnp.zeros(cand.shape, F32)
        work = cand
        for _ in range(PEER_TOPK):
            mx = jnp.max(work, axis=0, keepdims=True)
            if exact:
                idx = jnp.min(jnp.where(work == mx, flat, PEER_TOPK * PEER_TOPK), axis=0, keepdims=True)
                hit = flat == idx
            else:
                hit = work == mx
            sel = jnp.where(hit, 1.0, sel)
            work = jnp.where(hit, -jnp.inf, work)
        if not exact:
            excess = jnp.maximum(excess, _count_excess(jnp.where(r1 < float(PEER_TOPK), 1.0, 0.0)))
            excess = jnp.maximum(excess, _count_excess(jnp.where(r2 < float(PEER_TOPK), 1.0, 0.0)))
            excess = jnp.maximum(excess, _count_excess(sel))
        z = jnp.sum(jnp.where(sel > 0.0, jnp.exp(cand - cand[0:1]), 0.0), axis=0, keepdims=True)
        counts = [jnp.sum(sel[0:16], axis=0, keepdims=True)]
        for a in range(1, 8):
            counts.append(jnp.sum(sel[8 + 8 * a:16 + 8 * a], axis=0, keepdims=True))
        for a in range(8, PEER_TOPK):
            counts.append(sel[_CAND_ROWS - 16 + a:_CAND_ROWS - 15 + a])
        lim = jnp.zeros(r1.shape, F32)
        for a in range(PEER_TOPK):
            lim = jnp.where(r1 == float(a), counts[a], lim)
        r2_ref[h] = pltpu.bitcast(r2.astype(BF16), jnp.uint32)
        e2_ref[h] = pltpu.bitcast(jnp.exp(s2 - t2[0]).astype(BF16), jnp.uint32)
        lim_ref[h] = lim
        e1_ref[h] = jnp.exp(s1 - t1[0]) / z
    return excess


def _route_kernel(q_ref, sk1_ref, sk2_ref, r2_ref, e2_ref, lim_ref, e1_ref):
    q = pltpu.bitcast(q_ref[...], BF16)
    outs = (r2_ref, e2_ref, lim_ref, e1_ref)
    excess = _route_tables(q, sk1_ref, sk2_ref, *outs, exact=False)

    @pl.when(jnp.max(excess) > 0.0)
    def _():
        _route_tables(q, sk1_ref, sk2_ref, *outs, exact=True)


def _route(q, sk1, sk2):
    T = 2 * q.shape[0]
    TM = TM_ROUTE
    const = lambda i: (0, 0)
    tab = pl.BlockSpec((None, PEER_HEADS, N_KEYS, TM), lambda i: (i, 0, 0, 0))
    tab_packed = pl.BlockSpec((None, PEER_HEADS, N_KEYS // 2, TM), lambda i: (i, 0, 0, 0))
    tab_shape = jax.ShapeDtypeStruct((T // TM, PEER_HEADS, N_KEYS, TM), F32)
    tab_packed_shape = jax.ShapeDtypeStruct((T // TM, PEER_HEADS, N_KEYS // 2, TM), jnp.uint32)
    return pl.pallas_call(
        _route_kernel,
        grid=(T // TM,),
        in_specs=[pl.BlockSpec((TM // 2, PEER_HEADS * PEER_KEY_DIM), lambda i: (i, 0)),
                  pl.BlockSpec((N_KEYS, PEER_HALF), const), pl.BlockSpec((N_KEYS, PEER_HALF), const)],
        out_specs=[tab_packed, tab_packed, tab, tab],
        out_shape=[tab_packed_shape, tab_packed_shape, tab_shape, tab_shape],
        compiler_params=pltpu.CompilerParams(dimension_semantics=("arbitrary",),
                                             vmem_limit_bytes=VMEM_LIMIT),
        name="route",
    )(q, sk1, sk2)


def _peer_kernel(h2_ref, x1_ref, u_ref, vt_ref, r2_ref, e2_ref, lim_ref, e1_ref, nfw_ref,
                 y_ref, acc_ref, act0_ref, act1_ref, p0_ref, p1_ref, *, n_e, n_work):
    g = pl.program_id(0)
    per_blk = PEER_SUB * PEER_NSUB // N_KEYS
    e_b = jnp.clip(g - 1, 0, n_work - 1) % n_e
    e_c = jnp.clip(g - 2, 0, n_work - 1) % n_e
    act_bufs = (act0_ref, act1_ref)
    p_bufs = (p0_ref, p1_ref)

    @pl.when(g == 0)
    def _():
        for buf in act_bufs + p_bufs:
            buf[...] = jnp.zeros(buf.shape, buf.dtype)

    @pl.when(e_c == 0)
    def _():
        acc_ref[...] = jnp.zeros(acc_ref.shape, F32)

    def stages(slot_a, slot_b):
        def stage_a(s):
            h2 = pltpu.bitcast(h2_ref[...], BF16)
            u = pltpu.bitcast(u_ref[s * PEER_SUB // 2:(s + 1) * PEER_SUB // 2, :], BF16)
            act_bufs[slot_a][s * PEER_SUB:(s + 1) * PEER_SUB, :] = lax.dot_general(
                u, h2, NT_DIMS, preferred_element_type=F32)

        def stage_b(jbs):
            n_rb = N_KEYS // PEER_RB
            zero = jnp.zeros((PEER_RB, LANES), BF16)
            for jb in jbs:
                j = e_b * per_blk + jb
                for lh in range(TM_PEER // LANES):
                    cols = slice(lh * LANES, (lh + 1) * LANES)
                    gw = [None] * n_rb
                    for h in range(PEER_HEADS):
                        lim = jnp.broadcast_to(lim_ref[lh, h, pl.ds(j, 1), :], (PEER_RB, LANES)).astype(BF16)
                        e1 = jnp.broadcast_to(e1_ref[lh, h, pl.ds(j, 1), :], (PEER_RB, LANES)).astype(BF16)
                        for rb in range(n_rb):
                            words = slice(rb * PEER_RB // 2, (rb + 1) * PEER_RB // 2)
                            r2 = pltpu.bitcast(r2_ref[lh, h, words, :], BF16)
                            e2 = pltpu.bitcast(e2_ref[lh, h, words, :], BF16)
                            t = jnp.where(r2 < lim, e2, zero) * e1
                            gw[rb] = t if gw[rb] is None else gw[rb] + t
                    for rb in range(n_rb):
                        arows = slice(jb * N_KEYS + rb * PEER_RB, jb * N_KEYS + (rb + 1) * PEER_RB)
                        a = act_bufs[slot_b][arows, cols]
                        ga = 0.5 * a * (1.0 + lax.erf(a * np.float32(np.sqrt(0.5))))
                        p_bufs[slot_b][arows, cols] = gw[rb] * ga.astype(BF16)

        def stage_c():
            acc_ref[...] += jnp.dot(pltpu.bitcast(vt_ref[...], BF16), p_bufs[slot_a][...],
                                    preferred_element_type=F32)

        half = per_blk // 2
        stage_b(range(0, half))
        stage_a(0)
        stage_b(range(half, per_blk))
        stage_a(1)
        stage_c()

    @pl.when(g % 2 == 0)
    def _():
        stages(0, 1)

    @pl.when(g % 2 == 1)
    def _():
        stages(1, 0)

    @pl.when(jnp.logical_and(g >= 2, e_c == n_e - 1))
    def _():
        x = x1_ref[...] + acc_ref[...].T
        y_ref[...] = _rms(x, nfw_ref[...])


def _peer(h2, x1, u, vt, r2, e2, lim, e1, nfw):
    T = x1.shape[0]
    TM = TM_PEER
    ET = PEER_SUB * PEER_NSUB
    n_e = N_EXPERTS // ET
    n_work = (T // TM) * n_e
    item = lambda g, lag: jnp.clip(g - lag, 0, n_work - 1)
    rt = TM // TM_ROUTE
    tab = pl.BlockSpec((rt, PEER_HEADS, N_KEYS, TM_ROUTE), lambda g: (item(g, 1) // n_e, 0, 0, 0))
    tab_packed = pl.BlockSpec((rt, PEER_HEADS, N_KEYS // 2, TM_ROUTE),
                              lambda g: (item(g, 1) // n_e, 0, 0, 0))
    tok_c = lambda g: (item(g, 2) // n_e, 0)
    return pl.pallas_call(
        functools.partial(_peer_kernel, n_e=n_e, n_work=n_work),
        grid=(n_work + 2,),
        in_specs=[pl.BlockSpec((TM // 2, D_MODEL), lambda g: (item(g, 0) // n_e, 0)),
                  pl.BlockSpec((TM, D_MODEL), tok_c),
                  pl.BlockSpec((ET // 2, D_MODEL), lambda g: (item(g, 0) % n_e, 0)),
                  pl.BlockSpec((D_MODEL // 2, ET), lambda g: (0, item(g, 2) % n_e)),
                  tab_packed, tab_packed, tab, tab,
                  pl.BlockSpec((1, D_MODEL), lambda g: (0, 0))],
        out_specs=pl.BlockSpec((TM, D_MODEL), tok_c),
        out_shape=jax.ShapeDtypeStruct((T, D_MODEL), F32),
        scratch_shapes=[pltpu.VMEM((D_MODEL, TM), F32),
                        pltpu.VMEM((ET, TM), F32), pltpu.VMEM((ET, TM), F32),
                        pltpu.VMEM((ET, TM), BF16), pltpu.VMEM((ET, TM), BF16)],
        compiler_params=pltpu.CompilerParams(dimension_semantics=("arbitrary",),
                                             vmem_limit_bytes=VMEM_LIMIT),
        name="peer",
    )(h2, x1, u, vt, r2, e2, lim, e1, nfw)


def _rope_tables(pos):
    inv = ROPE_THETA ** (-jnp.arange(ROPE_HALF, dtype=F32) * 2.0 / ROPE_DIM)
    ang = pos.astype(F32)[:, None] * inv[None, :]
    cos, sin = jnp.cos(ang), jnp.sin(ang)
    n = pos.shape[0]
    rest = SWA_HEAD_DIM - ROPE_DIM
    zh = jnp.zeros((n, ROPE_HALF), F32)
    cos_h = jnp.concatenate([cos, cos, jnp.ones((n, rest), F32)], axis=1)
    sina_h = jnp.concatenate([-sin, zh, jnp.zeros((n, rest), F32)], axis=1)
    sinb_h = jnp.concatenate([zh, sin, jnp.zeros((n, rest), F32)], axis=1)
    rep = LANES // SWA_HEAD_DIM
    return tuple(jnp.tile(t, (1, rep)) for t in (cos_h, sina_h, sinb_h))


def _pack_kernel(x_ref, o_ref, *, transpose):
    x = x_ref[...]
    if transpose:
        x = x.T
    o_ref[...] = pltpu.bitcast(x.astype(BF16), jnp.uint32)


def _pack_expert_table(w, *, transpose):
    n, d = w.shape
    rows = PACK_ROWS
    if transpose:
        out_spec = pl.BlockSpec((d // 2, rows), lambda i: (0, i))
        out_shape = jax.ShapeDtypeStruct((d // 2, n), jnp.uint32)
    else:
        out_spec = pl.BlockSpec((rows // 2, d), lambda i: (i, 0))
        out_shape = jax.ShapeDtypeStruct((n // 2, d), jnp.uint32)
    return pl.pallas_call(
        functools.partial(_pack_kernel, transpose=transpose),
        grid=(n // rows,),
        in_specs=[pl.BlockSpec((rows, d), lambda i: (i, 0))],
        out_specs=out_spec,
        out_shape=out_shape,
        compiler_params=pltpu.CompilerParams(dimension_semantics=("arbitrary",),
                                             vmem_limit_bytes=VMEM_LIMIT),
        name="pack_vt" if transpose else "pack_u",
    )(w)


def _layer_tokens(x2d, tables, tab_map, W):
    return _inproj(x2d, W["norm_mix"], W["w_in"], W["bias"], *tables, tab_map)


def _ffn(hm, hs, x2d, W):
    x1, h2, q = _outproj(hm, hs, x2d, W["w_out"], W["norm_ffn"], W["w_q"])
    r2, e2, lim, e1 = _route(q, W["sk1"], W["sk2"])
    return _peer(h2, x1, W["u"], W["vt"], r2, e2, lim, e1, W["norm_final"])


def kernel(x_prompt, x_sample, cache_swa_k, cache_swa_v, state_mlstm_c, state_mlstm_n, state_mlstm_m,
           norm_mix_w, w_in, mlstm_if_bias, mlstm_norm_w, swa_sinks, w_out, norm_ffn_w,
           peer_w_q, peer_sub_keys_1, peer_sub_keys_2, peer_u, peer_v, norm_final_w):
    B, S, _ = x_prompt.shape
    DB, DS, _ = x_sample.shape
    l = 0
    wi = w_in[l]
    s_q = 4 * ML_WIDTH + 2 * ML_HEADS
    w_perm = jnp.concatenate(
        [wi[:, :4 * ML_WIDTH], wi[:, s_q:], wi[:, 4 * ML_WIDTH:s_q],
         jnp.zeros((D_MODEL, LANES - 2 * ML_HEADS), F32)], axis=1).astype(BF16)
    bias_pad = jnp.concatenate([mlstm_if_bias[l], jnp.zeros((LANES - 2 * ML_HEADS,), F32)])[None, :]
    W = {
        "norm_mix": norm_mix_w[l][None, :],
        "w_in": w_perm,
        "bias": bias_pad,
        "w_out": w_out[l].astype(BF16),
        "norm_ffn": norm_ffn_w[l][None, :],
        "w_q": peer_w_q[l].astype(BF16),
        "sk1": peer_sub_keys_1[l].astype(BF16),
        "sk2": peer_sub_keys_2[l].astype(BF16),
        "u": _pack_expert_table(peer_u[l], transpose=False),
        "vt": _pack_expert_table(peer_v[l], transpose=True),
        "norm_final": norm_final_w[None, :],
    }
    ml_nw = mlstm_norm_w[l][None, :]
    sinks = swa_sinks[l][None, :]

    xp = x_prompt.reshape(B * S, D_MODEL)
    tiles_per_seq = S // TM_INPROJ
    tabs_p = _rope_tables(jnp.arange(S, dtype=jnp.int32))
    qm, km, vm, og, sq, sk, sv, gc = _layer_tokens(xp, tabs_p, lambda i: (i % tiles_per_seq, 0), W)
    r3 = lambda a: a.reshape(B, S, a.shape[-1])
    zc = jnp.zeros((B, ML_HEADS, ML_HEAD_DIM, ML_HEAD_DIM), F32)
    zn = jnp.zeros((B, ML_HEADS, ML_HEAD_DIM), F32)
    zm = jnp.zeros((B, 1, LANES), F32)
    hm_p, c_p, n_p, m_p = _mlstm(r3(qm), r3(km), r3(vm), r3(og), r3(gc), zc, zn, zm, ml_nw, L_in=ML_CHUNK)
    sk3, sv3 = r3(sk), r3(sv)
    hs_p = _swa_prompt(r3(sq), sk3, sv3, sinks)
    y_p = _ffn(hm_p.reshape(B * S, ML_WIDTH), hs_p.reshape(B * S, SWA_WIDTH), xp, W)
    kv_shape = (1, B, WINDOW, SWA_KV_HEADS, SWA_HEAD_DIM)
    k_win_p = sk3[:, S - WINDOW:].reshape(kv_shape)
    v_win_p = sv3[:, S - WINDOW:].reshape(kv_shape)

    xs = x_sample.reshape(DB * DS, D_MODEL)
    pos_s = PAST_LEN + jnp.arange(DS, dtype=jnp.int32)
    tabs_s = tuple(jnp.tile(t, (DB, 1)) for t in _rope_tables(pos_s))
    qm, km, vm, og, sq, sk, sv, gc = _layer_tokens(xs, tabs_s, lambda i: (i, 0), W)
    r3s = lambda a: a.reshape(DB, DS, a.shape[-1])
    m0 = jnp.concatenate([state_mlstm_m[l], jnp.zeros((DB, LANES - ML_HEADS), F32)], axis=1)[:, None, :]
    hm_s, c_s, n_s, m_s = _mlstm(r3s(qm), r3s(km), r3s(vm), r3s(og), r3s(gc),
                                 state_mlstm_c[l], state_mlstm_n[l], m0, ml_nw, L_in=DS)
    kc = cache_swa_k[l].reshape(DB, WINDOW, SWA_KV_WIDTH)
    vc = cache_swa_v[l].reshape(DB, WINDOW, SWA_KV_WIDTH)
    hs_s, k_win_s, v_win_s = _swa_sample(r3s(sq), r3s(sk), r3s(sv), kc, vc, sinks)
    y_s = _ffn(hm_s.reshape(DB * DS, ML_WIDTH), hs_s.reshape(DB * DS, SWA_WIDTH), xs, W)
    kv_shape_s = (1, DB, WINDOW, SWA_KV_HEADS, SWA_HEAD_DIM)

    return (y_p.reshape(B, S, D_MODEL), y_s.reshape(DB, DS, D_MODEL),
            k_win_p, v_win_p, c_p[None], n_p[None], m_p[None, :, 0, :ML_HEADS],
            k_win_s.reshape(kv_shape_s), v_win_s.reshape(kv_shape_s),
            c_s[None], n_s[None], m_s[None, :, 0, :ML_HEADS])
```

```python
import functools

import jax
import jax.numpy as jnp
import numpy as np
from jax import lax
from jax.experimental import pallas as pl
from jax.experimental.pallas import tpu as pltpu

F32 = jnp.float32
BF16 = jnp.bfloat16

D_MODEL = 1024
SEQ = 8192
DEC_SEQ = 32
PAST_LEN = 4096
NORM_EPS = 1e-6
ML_HEADS = 4
ML_HEAD_DIM = 128
ML_WIDTH = ML_HEADS * ML_HEAD_DIM
SWA_HEADS = 8
SWA_KV_HEADS = 2
SWA_GROUP = SWA_HEADS // SWA_KV_HEADS
SWA_HEAD_DIM = 64
SWA_WIDTH = SWA_HEADS * SWA_HEAD_DIM
SWA_KV_WIDTH = SWA_KV_HEADS * SWA_HEAD_DIM
WINDOW = 128
SWA_CHUNK = 64
ROPE_THETA = 500000.0
ROPE_DIM = SWA_HEAD_DIM // 4
ROPE_HALF = ROPE_DIM // 2
PEER_HEADS = 8
N_KEYS = 128
N_EXPERTS = N_KEYS * N_KEYS
PEER_TOPK = 16
PEER_KEY_DIM = 256
PEER_HALF = PEER_KEY_DIM // 2

LANES = 128
VMEM_LIMIT = 56 * 1024 * 1024

COL_MQ, COL_MK, COL_MV, COL_MO = 0, ML_WIDTH, 2 * ML_WIDTH, 3 * ML_WIDTH
COL_SQ = 4 * ML_WIDTH
COL_SK = COL_SQ + SWA_WIDTH
COL_SV = COL_SK + SWA_KV_WIDTH
COL_G = COL_SV + SWA_KV_WIDTH
IN_COLS_PAD = COL_G + LANES

TM_INPROJ = 512
ML_CHUNK = 128
SWA_TQ = 256
TM_OUTPROJ = 512
TM_ROUTE = 128
TM_PEER = 512
PEER_SUB = 512
PEER_NSUB = 4
PEER_RB = 16
PACK_ROWS = 512

NT_DIMS = (((1,), (1,)), ((), ()))
TN_DIMS = (((0,), (0,)), ((), ()))


def _rms(x, w):
    return x * lax.rsqrt(jnp.mean(x * x, axis=-1, keepdims=True) + NORM_EPS) * w


def _inproj_kernel(x_ref, nw_ref, w_ref, bias_ref, cos_ref, sina_ref, sinb_ref,
                   qm_ref, km_ref, vm_ref, og_ref, sq_ref, sk_ref, sv_ref, gc_ref):
    h = _rms(x_ref[...], nw_ref[...])
    proj = jnp.dot(h.astype(BF16), w_ref[...], preferred_element_type=F32)
    qm_ref[...] = proj[:, COL_MQ:COL_MQ + ML_WIDTH]
    km_ref[...] = proj[:, COL_MK:COL_MK + ML_WIDTH] * (ML_HEAD_DIM ** -0.5)
    vm_ref[...] = proj[:, COL_MV:COL_MV + ML_WIDTH]
    og_ref[...] = jax.nn.sigmoid(proj[:, COL_MO:COL_MO + ML_WIDTH])
    cosf, sina, sinb = cos_ref[...], sina_ref[...], sinb_ref[...]

    def rope(xc):
        return (xc * cosf + pltpu.roll(xc, LANES - ROPE_HALF, 1) * sina
                + pltpu.roll(xc, ROPE_HALF, 1) * sinb)

    for j in range(SWA_WIDTH // LANES):
        sq_ref[:, j * LANES:(j + 1) * LANES] = rope(proj[:, COL_SQ + j * LANES:COL_SQ + (j + 1) * LANES])
    sk_ref[...] = rope(proj[:, COL_SK:COL_SK + LANES])
    sv_ref[...] = proj[:, COL_SV:COL_SV + LANES]
    g = proj[:, COL_G:COL_G + LANES] + bias_ref[...]
    lane = lax.broadcasted_iota(jnp.int32, g.shape, 1)
    gc_ref[...] = jnp.where(lane < ML_HEADS, g, jax.nn.log_sigmoid(g))


def _inproj(x2d, nw, w_perm, bias_pad, cos_t, sina_t, sinb_t, tab_map):
    T = x2d.shape[0]
    TM = TM_INPROJ
    row = lambda i: (i, 0)
    const = lambda i: (0, 0)
    f = lambda n: jax.ShapeDtypeStruct((T, n), F32)
    return pl.pallas_call(
        _inproj_kernel,
        grid=(T // TM,),
        in_specs=[pl.BlockSpec((TM, D_MODEL), row),
                  pl.BlockSpec((1, D_MODEL), const),
                  pl.BlockSpec((D_MODEL, IN_COLS_PAD), const),
                  pl.BlockSpec((1, LANES), const),
                  pl.BlockSpec((TM, LANES), tab_map),
                  pl.BlockSpec((TM, LANES), tab_map),
                  pl.BlockSpec((TM, LANES), tab_map)],
        out_specs=[pl.BlockSpec((TM, ML_WIDTH), row)] * 4
                  + [pl.BlockSpec((TM, SWA_WIDTH), row),
                     pl.BlockSpec((TM, LANES), row), pl.BlockSpec((TM, LANES), row),
                     pl.BlockSpec((TM, LANES), row)],
        out_shape=[f(ML_WIDTH)] * 4 + [f(SWA_WIDTH), f(LANES), f(LANES), f(LANES)],
        compiler_params=pltpu.CompilerParams(dimension_semantics=("arbitrary",),
                                             vmem_limit_bytes=VMEM_LIMIT),
        name="inproj",
    )(x2d, nw, w_perm, bias_pad, cos_t, sina_t, sinb_t)


def _mlstm_kernel(q_ref, k_ref, v_ref, og_ref, gc_ref, c0_ref, n0_ref, m0_ref, nw_ref,
                  hm_ref, c_ref, n_ref, m_ref, *, BB, L_in):
    LP = ML_CHUNK

    @pl.when(pl.program_id(1) == 0)
    def _():
        c_ref[...] = c0_ref[...]
        n_ref[...] = n0_ref[...]
        m_ref[...] = m0_ref[...]

    src = lax.broadcasted_iota(jnp.int32, (LP, LP), 0)
    qry = lax.broadcasted_iota(jnp.int32, (LP, LP), 1)
    visible = src <= qry
    tri_t = visible.astype(F32)
    row8 = lax.broadcasted_iota(jnp.int32, (8, LP), 0)
    lane1 = lax.broadcasted_iota(jnp.int32, (1, LANES), 1)
    hi = lax.Precision.HIGHEST

    def pad_rows(a):
        if L_in == LP:
            return a
        return jnp.concatenate([a, jnp.zeros((LP - L_in, a.shape[1]), a.dtype)], axis=0)

    streams = [(b, h) for b in range(BB) for h in range(ML_HEADS)]
    gr, b_r, m_all, qkv = {}, {}, {}, {}
    for b in range(BB):
        gc = gc_ref[b]
        if L_in < LP:
            lane_pad = lax.broadcasted_iota(jnp.int32, (LP - L_in, LANES), 1)
            fill = jnp.where(lane_pad < ML_HEADS, -jnp.inf, 0.0).astype(F32)
            gc = jnp.concatenate([gc, fill], axis=0)
        gr[b] = gc.T[:8]
        lf_r = jnp.where(row8 >= ML_HEADS, gr[b], 0.0)
        b_r[b] = jnp.dot(lf_r, tri_t, precision=hi, preferred_element_type=F32)
        m_all[b] = m_ref[b]
        qkv[b] = (pad_rows(q_ref[b]), pad_rows(k_ref[b]), pad_rows(v_ref[b]))

    st = {}
    for b, h in streams:
        sl = slice(h * ML_HEAD_DIM, (h + 1) * ML_HEAD_DIM)
        q, k, v = (a[:, sl] for a in qkv[b])
        ig_r = gr[b][h:h + 1, :]
        b_rh = b_r[b][ML_HEADS + h:ML_HEADS + h + 1, :]
        m_prev = m_all[b][:, h:h + 1]
        src_term = jnp.broadcast_to(ig_r - b_rh, (LP, LP)).T
        logd = jnp.where(visible, b_rh + src_term, -jnp.inf)
        m_inter = b_rh + m_prev
        m_t = jnp.maximum(m_inter, jnp.max(logd, axis=0, keepdims=True))
        b_last = b_rh[:, LP - 1:LP]
        logw = b_last - b_rh + ig_r
        m_new = jnp.maximum(b_last + m_prev, jnp.max(logw, axis=1, keepdims=True))
        st[b, h] = dict(sl=sl, q=q, k=k, qb=q.astype(BF16), kb=k.astype(BF16), v_t=v.T, m_t=m_t,
                        dmat=jnp.exp(logd - m_t), w_int=jnp.exp(m_inter - m_t), m_new=m_new,
                        w_r=jnp.exp(logw - m_new), decay=jnp.exp(b_last + m_prev - m_new),
                        c_old=c_ref[b, h], n_old=n_ref[b, h:h + 1, :])

    for key in streams:
        d = st[key]
        d["s"] = lax.dot_general(d["kb"], d["qb"], NT_DIMS, preferred_element_type=F32)
        d["cq"] = lax.dot_general(d["c_old"].astype(BF16), d["qb"], NT_DIMS, preferred_element_type=F32)
        d["nq"] = lax.dot_general(jnp.broadcast_to(d["n_old"], (8, ML_HEAD_DIM)), d["q"], NT_DIMS,
                                  precision=hi, preferred_element_type=F32)[0:1]

    for b, h in streams:
        d = st[b, h]
        c_ref[b, h] = d["decay"] * d["c_old"] + jnp.dot(
            (d["v_t"] * d["w_r"]).astype(BF16), d["kb"], preferred_element_type=F32)
        n_ref[b, h:h + 1, :] = d["decay"] * d["n_old"] + jnp.dot(
            jnp.broadcast_to(d["w_r"], (8, LP)), d["k"], precision=hi, preferred_element_type=F32)[0:1]

    for key in streams:
        d = st[key]
        qk = d["s"] * d["dmat"]
        num = d["w_int"] * d["cq"] + jnp.dot(d["v_t"].astype(BF16), qk.astype(BF16),
                                             preferred_element_type=F32)
        den = d["w_int"] * d["nq"] + jnp.sum(qk, axis=0, keepdims=True)
        d["hh"] = num / jnp.maximum(jnp.abs(den), jnp.exp(-d["m_t"]))

    for b in range(BB):
        og_all = og_ref[b]
        m_out = m_all[b]
        for h in range(ML_HEADS):
            d = st[b, h]
            hh, sl = d["hh"], d["sl"]
            y_t = hh * lax.rsqrt(jnp.mean(hh * hh, axis=0, keepdims=True) + NORM_EPS)
            y = y_t.T * nw_ref[:, sl]
            hm_ref[b, :, sl] = (og_all[:, sl] * y[:L_in]).astype(BF16)
            m_out = jnp.where(lane1 == h, d["m_new"], m_out)
        m_ref[b] = m_out


def _mlstm(qm, km, vm, og, gc, c0, n0, m0, nw, *, L_in):
    B, T, _ = qm.shape
    BB = 2
    nchunks = T // L_in
    tok = lambda n: pl.BlockSpec((BB, L_in, n), lambda g, c: (g, c, 0))
    st_c = pl.BlockSpec((BB, ML_HEADS, ML_HEAD_DIM, ML_HEAD_DIM), lambda g, c: (g, 0, 0, 0))
    st_n = pl.BlockSpec((BB, ML_HEADS, ML_HEAD_DIM), lambda g, c: (g, 0, 0))
    st_m = pl.BlockSpec((BB, 1, LANES), lambda g, c: (g, 0, 0))
    return pl.pallas_call(
        functools.partial(_mlstm_kernel, BB=BB, L_in=L_in),
        grid=(B // BB, nchunks),
        in_specs=[tok(ML_WIDTH), tok(ML_WIDTH), tok(ML_WIDTH), tok(ML_WIDTH), tok(LANES),
                  st_c, st_n, st_m, pl.BlockSpec((1, ML_WIDTH), lambda g, c: (0, 0))],
        out_specs=[tok(ML_WIDTH), st_c, st_n, st_m],
        out_shape=[jax.ShapeDtypeStruct((B, T, ML_WIDTH), BF16),
                   jax.ShapeDtypeStruct(c0.shape, F32),
                   jax.ShapeDtypeStruct(n0.shape, F32),
                   jax.ShapeDtypeStruct(m0.shape, F32)],
        compiler_params=pltpu.CompilerParams(dimension_semantics=("arbitrary", "arbitrary"),
                                             vmem_limit_bytes=VMEM_LIMIT),
        name="mlstm",
    )(qm, km, vm, og, gc, c0, n0, m0, nw)


def _sink_attention(q4, kw, vw, sink_col, bias):
    s = lax.dot_general(q4, kw, NT_DIMS, preferred_element_type=F32) * (SWA_HEAD_DIM ** -0.5)
    if bias is not None:
        s = s + bias
    mx = jnp.maximum(jnp.max(s, axis=1, keepdims=True), sink_col)
    p = jnp.exp(s - mx)
    den = jnp.sum(p, axis=1, keepdims=True) + jnp.exp(sink_col - mx)
    return jnp.dot(p.astype(BF16), vw, preferred_element_type=F32) / den


def _sink_column(sink_ref, g, rows):
    return jnp.concatenate(
        [jnp.broadcast_to(sink_ref[:, g * SWA_GROUP + j:g * SWA_GROUP + j + 1], (rows, 1))
         for j in range(SWA_GROUP)], axis=0)


def _swa_prompt_kernel(q_ref, kc_ref, kp_ref, vc_ref, vp_ref, sink_ref, bias_ref, o_ref):
    TQ = SWA_TQ
    q = q_ref[0]
    kcat = jnp.concatenate([kp_ref[0, TQ - WINDOW:, :], kc_ref[0]], axis=0).astype(BF16)
    vcat_t = jnp.concatenate([vp_ref[0, TQ - WINDOW:, :], vc_ref[0]], axis=0).T.astype(BF16)
    bias_t = bias_ref[0]
    outs = []
    for g in range(SWA_KV_HEADS):
        gs = slice(g * SWA_HEAD_DIM, (g + 1) * SWA_HEAD_DIM)
        heads = range(g * SWA_GROUP, (g + 1) * SWA_GROUP)
        q4 = jnp.concatenate([q[:, h * SWA_HEAD_DIM:(h + 1) * SWA_HEAD_DIM] for h in heads],
                             axis=0).astype(BF16)
        sink_row = jnp.concatenate([jnp.broadcast_to(sink_ref[:, h:h + 1], (1, TQ)) for h in heads], axis=1)
        s_t = lax.dot_general(kcat[:, gs], q4, NT_DIMS, preferred_element_type=F32)
        s_t = s_t * (SWA_HEAD_DIM ** -0.5) + bias_t
        mx = jnp.maximum(jnp.max(s_t, axis=0, keepdims=True), sink_row)
        p = jnp.exp(s_t - mx)
        den = jnp.sum(p, axis=0, keepdims=True) + jnp.exp(sink_row - mx)
        o_t = jnp.dot(vcat_t[gs, :], p.astype(BF16), preferred_element_type=F32) / den
        outs += [o_t[:, j * TQ:(j + 1) * TQ] for j in range(SWA_GROUP)]
    o_ref[0] = jnp.concatenate(outs, axis=0).T.astype(BF16)


def _swa_prompt_bias():
    TQ = SWA_TQ
    t = np.arange(SWA_GROUP * TQ) % TQ
    k = np.arange(WINDOW + TQ)
    qc = (t // SWA_CHUNK)[None, :]
    kc = (k // SWA_CHUNK)[:, None]
    visible = (kc >= qc) & (kc <= qc + WINDOW // SWA_CHUNK)
    first = visible & (k[:, None] >= WINDOW)
    return np.where(np.stack([first, visible]), 0.0, -np.inf).astype(np.float32)


def _swa_prompt(sq, sk, sv, sinks):
    B, T, _ = sq.shape
    TQ = SWA_TQ
    cur = lambda b, i: (b, i, 0)
    prev = lambda b, i: (b, jnp.maximum(i - 1, 0), 0)
    bias = jnp.asarray(_swa_prompt_bias())
    return pl.pallas_call(
        _swa_prompt_kernel,
        grid=(B, T // TQ),
        in_specs=[pl.BlockSpec((1, TQ, SWA_WIDTH), cur),
                  pl.BlockSpec((1, TQ, SWA_KV_WIDTH), cur), pl.BlockSpec((1, TQ, SWA_KV_WIDTH), prev),
                  pl.BlockSpec((1, TQ, SWA_KV_WIDTH), cur), pl.BlockSpec((1, TQ, SWA_KV_WIDTH), prev),
                  pl.BlockSpec((1, SWA_HEADS), lambda b, i: (0, 0)),
                  pl.BlockSpec((1,) + bias.shape[1:], lambda b, i: (jnp.minimum(i, 1), 0, 0))],
        out_specs=pl.BlockSpec((1, TQ, SWA_WIDTH), cur),
        out_shape=jax.ShapeDtypeStruct((B, T, SWA_WIDTH), BF16),
        compiler_params=pltpu.CompilerParams(dimension_semantics=("arbitrary", "arbitrary"),
                                             vmem_limit_bytes=VMEM_LIMIT),
        name="swa_prompt",
    )(sq, sk, sk, sv, sv, sinks, bias)


def _swa_sample_kernel(q_ref, kn_ref, vn_ref, kc_ref, vc_ref, sink_ref, o_ref, kw_ref, vw_ref):
    T = DEC_SEQ
    q = q_ref[0]
    k_all = jnp.concatenate([kc_ref[0], kn_ref[0]], axis=0)
    v_all = jnp.concatenate([vc_ref[0], vn_ref[0]], axis=0)
    kw_ref[0] = k_all[T:]
    vw_ref[0] = v_all[T:]
    kb, vb = k_all.astype(BF16), v_all.astype(BF16)
    for g in range(SWA_KV_HEADS):
        gs = slice(g * SWA_HEAD_DIM, (g + 1) * SWA_HEAD_DIM)
        q4 = jnp.concatenate(
            [q[:, (g * SWA_GROUP + j) * SWA_HEAD_DIM:(g * SWA_GROUP + j + 1) * SWA_HEAD_DIM]
             for j in range(SWA_GROUP)], axis=0).astype(BF16)
        o = _sink_attention(q4, kb[:, gs], vb[:, gs], _sink_column(sink_ref, g, T), None)
        for j in range(SWA_GROUP):
            hd = (g * SWA_GROUP + j) * SWA_HEAD_DIM
            o_ref[0, :, hd:hd + SWA_HEAD_DIM] = o[j * T:(j + 1) * T].astype(BF16)


def _swa_sample(sq, sk, sv, k_cache, v_cache, sinks):
    B, T, _ = sq.shape
    b3 = lambda b: (b, 0, 0)
    return pl.pallas_call(
        _swa_sample_kernel,
        grid=(B,),
        in_specs=[pl.BlockSpec((1, T, SWA_WIDTH), b3),
                  pl.BlockSpec((1, T, SWA_KV_WIDTH), b3), pl.BlockSpec((1, T, SWA_KV_WIDTH), b3),
                  pl.BlockSpec((1, WINDOW, SWA_KV_WIDTH), b3), pl.BlockSpec((1, WINDOW, SWA_KV_WIDTH), b3),
                  pl.BlockSpec((1, SWA_HEADS), lambda b: (0, 0))],
        out_specs=[pl.BlockSpec((1, T, SWA_WIDTH), b3),
                   pl.BlockSpec((1, WINDOW, SWA_KV_WIDTH), b3), pl.BlockSpec((1, WINDOW, SWA_KV_WIDTH), b3)],
        out_shape=[jax.ShapeDtypeStruct((B, T, SWA_WIDTH), BF16),
                   jax.ShapeDtypeStruct((B, WINDOW, SWA_KV_WIDTH), F32),
                   jax.ShapeDtypeStruct((B, WINDOW, SWA_KV_WIDTH), F32)],
        compiler_params=pltpu.CompilerParams(dimension_semantics=("arbitrary",),
                                             vmem_limit_bytes=VMEM_LIMIT),
        name="swa_sample",
    )(sq, sk, sv, k_cache, v_cache, sinks)


def _outproj_kernel(hm_ref, hs_ref, x_ref, wo_ref, nw_ref, wq_ref, x1_ref, h2_ref, q_ref):
    mix = (jnp.dot(hm_ref[...], wo_ref[:ML_WIDTH, :], preferred_element_type=F32)
           + jnp.dot(hs_ref[...], wo_ref[ML_WIDTH:, :], preferred_element_type=F32))
    x1 = x_ref[...] + mix
    x1_ref[...] = x1
    h2 = _rms(x1, nw_ref[...]).astype(BF16)
    h2_ref[...] = pltpu.bitcast(h2, jnp.uint32)
    q = jnp.dot(h2, wq_ref[...], preferred_element_type=F32)
    q_ref[...] = pltpu.bitcast(q.astype(BF16), jnp.uint32)


def _outproj(hm, hs, x2d, wo, nw, wq):
    T = x2d.shape[0]
    TM = TM_OUTPROJ
    QW = PEER_HEADS * PEER_KEY_DIM
    row = lambda i: (i, 0)
    const = lambda i: (0, 0)
    return pl.pallas_call(
        _outproj_kernel,
        grid=(T // TM,),
        in_specs=[pl.BlockSpec((TM, ML_WIDTH), row), pl.BlockSpec((TM, SWA_WIDTH), row),
                  pl.BlockSpec((TM, D_MODEL), row),
                  pl.BlockSpec((D_MODEL, D_MODEL), const), pl.BlockSpec((1, D_MODEL), const),
                  pl.BlockSpec((D_MODEL, QW), const)],
        out_specs=[pl.BlockSpec((TM, D_MODEL), row), pl.BlockSpec((TM // 2, D_MODEL), row),
                   pl.BlockSpec((TM // 2, QW), row)],
        out_shape=[jax.ShapeDtypeStruct((T, D_MODEL), F32),
                   jax.ShapeDtypeStruct((T // 2, D_MODEL), jnp.uint32),
                   jax.ShapeDtypeStruct((T // 2, QW), jnp.uint32)],
        compiler_params=pltpu.CompilerParams(dimension_semantics=("arbitrary",),
                                             vmem_limit_bytes=VMEM_LIMIT),
        name="outproj",
    )(hm, hs, x2d, wo, nw, wq)


_CAND_NB = [PEER_TOPK // (a + 1) for a in range(PEER_TOPK)]
_CAND_ROWS = 16 + 8 * 7 + 8


def _extract_top16(S, exact):
    R = S.shape[0]
    iota = lax.broadcasted_iota(jnp.int32, S.shape, 0)
    rank = jnp.full(S.shape, float(PEER_TOPK), F32)
    vals = []
    for r in range(PEER_TOPK):
        mx = jnp.max(S, axis=0, keepdims=True)
        if exact:
            idx = jnp.min(jnp.where(S == mx, iota, R), axis=0, keepdims=True)
            hit = iota == idx
        else:
            hit = S == mx
        rank = jnp.where(hit, float(r), rank)
        S = jnp.where(hit, -jnp.inf, S)
        vals.append(mx)
    return vals, rank


def _count_excess(flags):
    return jnp.abs(jnp.sum(flags, axis=0, keepdims=True) - float(PEER_TOPK))


def _route_tables(q, sk1_ref, sk2_ref, r2_ref, e2_ref, lim_ref, e1_ref, exact):
    N = q.shape[0]
    row8 = lax.broadcasted_iota(jnp.int32, (8, N), 0)
    rowc = lax.broadcasted_iota(jnp.int32, (_CAND_ROWS, N), 0)
    mid = rowc - 16
    flat = jnp.where(rowc < 16, rowc,
                     jnp.where(rowc < _CAND_ROWS - 8,
                               PEER_TOPK * ((mid >> 3) + 1) + (mid & 7),
                               PEER_TOPK * (rowc - (_CAND_ROWS - 16))))
    excess = jnp.zeros((1, N), F32)
    for h in range(PEER_HEADS):
        c0 = h * PEER_KEY_DIM
        s1 = lax.dot_general(sk1_ref[...], q[:, c0:c0 + PEER_HALF], NT_DIMS, preferred_element_type=F32)
        s2 = lax.dot_general(sk2_ref[...], q[:, c0 + PEER_HALF:c0 + PEER_KEY_DIM], NT_DIMS,
                             preferred_element_type=F32)
        t1, r1 = _extract_top16(s1, exact)
        t2, r2 = _extract_top16(s2, exact)
        t2_16 = jnp.concatenate(t2, axis=0)
        t2_8 = t2_16[:8]
        blocks = [t1[0] + t2_16, t1[1] + t2_8]
        for a in range(2, 8):
            blocks.append(jnp.where(row8 < _CAND_NB[a], t1[a] + t2_8, -jnp.inf))
        blocks.append(jnp.concatenate(t1[8:], axis=0) + t2[0])
        cand = jnp.concatenate(blocks, axis=0)
        sel = jnp.zeros(cand.shape, F32)
        work = cand
        for _ in range(PEER_TOPK):
            mx = jnp.max(work, axis=0, keepdims=True)
            if exact:
                idx = jnp.min(jnp.where(work == mx, flat, PEER_TOPK * PEER_TOPK), axis=0, keepdims=True)
                hit = flat == idx
            else:
                hit = work == mx
            sel = jnp.where(hit, 1.0, sel)
            work = jnp.where(hit, -jnp.inf, work)
        if not exact:
            excess = jnp.maximum(excess, _count_excess(jnp.where(r1 < float(PEER_TOPK), 1.0, 0.0)))
            excess = jnp.maximum(excess, _count_excess(jnp.where(r2 < float(PEER_TOPK), 1.0, 0.0)))
            excess = jnp.maximum(excess, _count_excess(sel))
        z = jnp.sum(jnp.where(sel > 0.0, jnp.exp(cand - cand[0:1]), 0.0), axis=0, keepdims=True)
        counts = [jnp.sum(sel[0:16], axis=0, keepdims=True)]
        for a in range(1, 8):
            counts.append(jnp.sum(sel[8 + 8 * a:16 + 8 * a], axis=0, keepdims=True))
        for a in range(8, PEER_TOPK):
            counts.append(sel[_CAND_ROWS - 16 + a:_CAND_ROWS - 15 + a])
        lim = jnp.zeros(r1.shape, F32)
        for a in range(PEER_TOPK):
            lim = jnp.where(r1 == float(a), counts[a], lim)
        r2_ref[h] = pltpu.bitcast(r2.astype(BF16), jnp.uint32)
        e2_ref[h] = pltpu.bitcast(jnp.exp(s2 - t2[0]).astype(BF16), jnp.uint32)
        lim_ref[h] = lim
        e1_ref[h] = jnp.exp(s1 - t1[0]) / z
    return excess


def _route_kernel(q_ref, sk1_ref, sk2_ref, r2_ref, e2_ref, lim_ref, e1_ref):
    q = pltpu.bitcast(q_ref[...], BF16)
    outs = (r2_ref, e2_ref, lim_ref, e1_ref)
    excess = _route_tables(q, sk1_ref, sk2_ref, *outs, exact=False)

    @pl.when(jnp.max(excess) > 0.0)
    def _():
        _route_tables(q, sk1_ref, sk2_ref, *outs, exact=True)


def _route(q, sk1, sk2):
    T = 2 * q.shape[0]
    TM = TM_ROUTE
    const = lambda i: (0, 0)
    tab = pl.BlockSpec((None, PEER_HEADS, N_KEYS, TM), lambda i: (i, 0, 0, 0))
    tab_packed = pl.BlockSpec((None, PEER_HEADS, N_KEYS // 2, TM), lambda i: (i, 0, 0, 0))
    tab_shape = jax.ShapeDtypeStruct((T // TM, PEER_HEADS, N_KEYS, TM), F32)
    tab_packed_shape = jax.ShapeDtypeStruct((T // TM, PEER_HEADS, N_KEYS // 2, TM), jnp.uint32)
    return pl.pallas_call(
        _route_kernel,
        grid=(T // TM,),
        in_specs=[pl.BlockSpec((TM // 2, PEER_HEADS * PEER_KEY_DIM), lambda i: (i, 0)),
                  pl.BlockSpec((N_KEYS, PEER_HALF), const), pl.BlockSpec((N_KEYS, PEER_HALF), const)],
        out_specs=[tab_packed, tab_packed, tab, tab],
        out_shape=[tab_packed_shape, tab_packed_shape, tab_shape, tab_shape],
        compiler_params=pltpu.CompilerParams(dimension_semantics=("arbitrary",),
                                             vmem_limit_bytes=VMEM_LIMIT),
        name="route",
    )(q, sk1, sk2)


def _peer_kernel(h2_ref, x1_ref, u_ref, vt_ref, r2_ref, e2_ref, lim_ref, e1_ref, nfw_ref,
                 y_ref, acc_ref, act0_ref, act1_ref, p0_ref, p1_ref, *, n_e, n_work):
    g = pl.program_id(0)
    per_blk = PEER_SUB * PEER_NSUB // N_KEYS
    e_b = jnp.clip(g - 1, 0, n_work - 1) % n_e
    e_c = jnp.clip(g - 2, 0, n_work - 1) % n_e
    act_bufs = (act0_ref, act1_ref)
    p_bufs = (p0_ref, p1_ref)

    @pl.when(g == 0)
    def _():
        for buf in act_bufs + p_bufs:
            buf[...] = jnp.zeros(buf.shape, buf.dtype)

    @pl.when(e_c == 0)
    def _():
        acc_ref[...] = jnp.zeros(acc_ref.shape, F32)

    def stages(slot_a, slot_b):
        def stage_a(s):
            h2 = pltpu.bitcast(h2_ref[...], BF16)
            u = pltpu.bitcast(u_ref[s * PEER_SUB // 2:(s + 1) * PEER_SUB // 2, :], BF16)
            act_bufs[slot_a][s * PEER_SUB:(s + 1) * PEER_SUB, :] = lax.dot_general(
                u, h2, NT_DIMS, preferred_element_type=F32)

        def stage_b(jbs):
            n_rb = N_KEYS // PEER_RB
            zero = jnp.zeros((PEER_RB, LANES), BF16)
            for jb in jbs:
                j = e_b * per_blk + jb
                for lh in range(TM_PEER // LANES):
                    cols = slice(lh * LANES, (lh + 1) * LANES)
                    gw = [None] * n_rb
                    for h in range(PEER_HEADS):
                        lim = jnp.broadcast_to(lim_ref[lh, h, pl.ds(j, 1), :], (PEER_RB, LANES)).astype(BF16)
                        e1 = jnp.broadcast_to(e1_ref[lh, h, pl.ds(j, 1), :], (PEER_RB, LANES)).astype(BF16)
                        for rb in range(n_rb):
                            words = slice(rb * PEER_RB // 2, (rb + 1) * PEER_RB // 2)
                            r2 = pltpu.bitcast(r2_ref[lh, h, words, :], BF16)
                            e2 = pltpu.bitcast(e2_ref[lh, h, words, :], BF16)
                            t = jnp.where(r2 < lim, e2, zero) * e1
                            gw[rb] = t if gw[rb] is None else gw[rb] + t
                    for rb in range(n_rb):
                        arows = slice(jb * N_KEYS + rb * PEER_RB, jb * N_KEYS + (rb + 1) * PEER_RB)
                        a = act_bufs[slot_b][arows, cols]
                        ga = 0.5 * a * (1.0 + lax.erf(a * np.float32(np.sqrt(0.5))))
                        p_bufs[slot_b][arows, cols] = gw[rb] * ga.astype(BF16)

        def stage_c():
            acc_ref[...] += jnp.dot(pltpu.bitcast(vt_ref[...], BF16), p_bufs[slot_a][...],
                                    preferred_element_type=F32)

        part = per_blk // PEER_NSUB
        for s in range(PEER_NSUB):
            stage_b(range(s * part, (s + 1) * part))
            stage_a(s)
        stage_c()

    @pl.when(g % 2 == 0)
    def _():
        stages(0, 1)

    @pl.when(g % 2 == 1)
    def _():
        stages(1, 0)

    @pl.when(jnp.logical_and(g >= 2, e_c == n_e - 1))
    def _():
        x = x1_ref[...] + acc_ref[...].T
        y_ref[...] = _rms(x, nfw_ref[...])


def _peer(h2, x1, u, vt, r2, e2, lim, e1, nfw):
    T = x1.shape[0]
    TM = TM_PEER
    ET = PEER_SUB * PEER_NSUB
    n_e = N_EXPERTS // ET
    n_work = (T // TM) * n_e
    item = lambda g, lag: jnp.clip(g - lag, 0, n_work - 1)
    rt = TM // TM_ROUTE
    tab = pl.BlockSpec((rt, PEER_HEADS, N_KEYS, TM_ROUTE), lambda g: (item(g, 1) // n_e, 0, 0, 0))
    tab_packed = pl.BlockSpec((rt, PEER_HEADS, N_KEYS // 2, TM_ROUTE),
                              lambda g: (item(g, 1) // n_e, 0, 0, 0))
    tok_c = lambda g: (item(g, 2) // n_e, 0)
    return pl.pallas_call(
        functools.partial(_peer_kernel, n_e=n_e, n_work=n_work),
        grid=(n_work + 2,),
        in_specs=[pl.BlockSpec((TM // 2, D_MODEL), lambda g: (item(g, 0) // n_e, 0)),
                  pl.BlockSpec((TM, D_MODEL), tok_c),
                  pl.BlockSpec((ET // 2, D_MODEL), lambda g: (item(g, 0) % n_e, 0)),
                  pl.BlockSpec((D_MODEL // 2, ET), lambda g: (0, item(g, 2) % n_e)),
                  tab_packed, tab_packed, tab, tab,
                  pl.BlockSpec((1, D_MODEL), lambda g: (0, 0))],
        out_specs=pl.BlockSpec((TM, D_MODEL), tok_c),
        out_shape=jax.ShapeDtypeStruct((T, D_MODEL), F32),
        scratch_shapes=[pltpu.VMEM((D_MODEL, TM), F32),
                        pltpu.VMEM((ET, TM), F32), pltpu.VMEM((ET, TM), F32),
                        pltpu.VMEM((ET, TM), BF16), pltpu.VMEM((ET, TM), BF16)],
        compiler_params=pltpu.CompilerParams(dimension_semantics=("arbitrary",),
                                             vmem_limit_bytes=VMEM_LIMIT),
        name="peer",
    )(h2, x1, u, vt, r2, e2, lim, e1, nfw)


def _rope_tables(pos):
    inv = ROPE_THETA ** (-jnp.arange(ROPE_HALF, dtype=F32) * 2.0 / ROPE_DIM)
    ang = pos.astype(F32)[:, None] * inv[None, :]
    cos, sin = jnp.cos(ang), jnp.sin(ang)
    n = pos.shape[0]
    rest = SWA_HEAD_DIM - ROPE_DIM
    zh = jnp.zeros((n, ROPE_HALF), F32)
    cos_h = jnp.concatenate([cos, cos, jnp.ones((n, rest), F32)], axis=1)
    sina_h = jnp.concatenate([-sin, zh, jnp.zeros((n, rest), F32)], axis=1)
    sinb_h = jnp.concatenate([zh, sin, jnp.zeros((n, rest), F32)], axis=1)
    rep = LANES // SWA_HEAD_DIM
    return tuple(jnp.tile(t, (1, rep)) for t in (cos_h, sina_h, sinb_h))


def _pack_kernel(x_ref, o_ref, *, transpose):
    x = x_ref[...]
    if transpose:
        x = x.T
    o_ref[...] = pltpu.bitcast(x.astype(BF16), jnp.uint32)


def _pack_expert_table(w, *, transpose):
    n, d = w.shape
    rows = PACK_ROWS
    if transpose:
        out_spec = pl.BlockSpec((d // 2, rows), lambda i: (0, i))
        out_shape = jax.ShapeDtypeStruct((d // 2, n), jnp.uint32)
    else:
        out_spec = pl.BlockSpec((rows // 2, d), lambda i: (i, 0))
        out_shape = jax.ShapeDtypeStruct((n // 2, d), jnp.uint32)
    return pl.pallas_call(
        functools.partial(_pack_kernel, transpose=transpose),
        grid=(n // rows,),
        in_specs=[pl.BlockSpec((rows, d), lambda i: (i, 0))],
        out_specs=out_spec,
        out_shape=out_shape,
        compiler_params=pltpu.CompilerParams(dimension_semantics=("arbitrary",),
                                             vmem_limit_bytes=VMEM_LIMIT),
        name="pack_vt" if transpose else "pack_u",
    )(w)


def _layer_tokens(x2d, tables, tab_map, W):
    return _inproj(x2d, W["norm_mix"], W["w_in"], W["bias"], *tables, tab_map)


def _ffn(hm, hs, x2d, W):
    x1, h2, q = _outproj(hm, hs, x2d, W["w_out"], W["norm_ffn"], W["w_q"])
    r2, e2, lim, e1 = _route(q, W["sk1"], W["sk2"])
    return _peer(h2, x1, W["u"], W["vt"], r2, e2, lim, e1, W["norm_final"])


def kernel(x_prompt, x_sample, cache_swa_k, cache_swa_v, state_mlstm_c, state_mlstm_n, state_mlstm_m,
           norm_mix_w, w_in, mlstm_if_bias, mlstm_norm_w, swa_sinks, w_out, norm_ffn_w,
           peer_w_q, peer_sub_keys_1, peer_sub_keys_2, peer_u, peer_v, norm_final_w):
    B, S, _ = x_prompt.shape
    DB, DS, _ = x_sample.shape
    l = 0
    wi = w_in[l]
    s_q = 4 * ML_WIDTH + 2 * ML_HEADS
    w_perm = jnp.concatenate(
        [wi[:, :4 * ML_WIDTH], wi[:, s_q:], wi[:, 4 * ML_WIDTH:s_q],
         jnp.zeros((D_MODEL, LANES - 2 * ML_HEADS), F32)], axis=1).astype(BF16)
    bias_pad = jnp.concatenate([mlstm_if_bias[l], jnp.zeros((LANES - 2 * ML_HEADS,), F32)])[None, :]
    W = {
        "norm_mix": norm_mix_w[l][None, :],
        "w_in": w_perm,
        "bias": bias_pad,
        "w_out": w_out[l].astype(BF16),
        "norm_ffn": norm_ffn_w[l][None, :],
        "w_q": peer_w_q[l].astype(BF16),
        "sk1": peer_sub_keys_1[l].astype(BF16),
        "sk2": peer_sub_keys_2[l].astype(BF16),
        "u": _pack_expert_table(peer_u[l], transpose=False),
        "vt": _pack_expert_table(peer_v[l], transpose=True),
        "norm_final": norm_final_w[None, :],
    }
    ml_nw = mlstm_norm_w[l][None, :]
    sinks = swa_sinks[l][None, :]

    xp = x_prompt.reshape(B * S, D_MODEL)
    tiles_per_seq = S // TM_INPROJ
    tabs_p = _rope_tables(jnp.arange(S, dtype=jnp.int32))
    qm, km, vm, og, sq, sk, sv, gc = _layer_tokens(xp, tabs_p, lambda i: (i % tiles_per_seq, 0), W)
    r3 = lambda a: a.reshape(B, S, a.shape[-1])
    zc = jnp.zeros((B, ML_HEADS, ML_HEAD_DIM, ML_HEAD_DIM), F32)
    zn = jnp.zeros((B, ML_HEADS, ML_HEAD_DIM), F32)
    zm = jnp.zeros((B, 1, LANES), F32)
    hm_p, c_p, n_p, m_p = _mlstm(r3(qm), r3(km), r3(vm), r3(og), r3(gc), zc, zn, zm, ml_nw, L_in=ML_CHUNK)
    sk3, sv3 = r3(sk), r3(sv)
    hs_p = _swa_prompt(r3(sq), sk3, sv3, sinks)
    y_p = _ffn(hm_p.reshape(B * S, ML_WIDTH), hs_p.reshape(B * S, SWA_WIDTH), xp, W)
    kv_shape = (1, B, WINDOW, SWA_KV_HEADS, SWA_HEAD_DIM)
    k_win_p = sk3[:, S - WINDOW:].reshape(kv_shape)
    v_win_p = sv3[:, S - WINDOW:].reshape(kv_shape)

    xs = x_sample.reshape(DB * DS, D_MODEL)
    pos_s = PAST_LEN + jnp.arange(DS, dtype=jnp.int32)
    tabs_s = tuple(jnp.tile(t, (DB, 1)) for t in _rope_tables(pos_s))
    qm, km, vm, og, sq, sk, sv, gc = _layer_tokens(xs, tabs_s, lambda i: (i, 0), W)
    r3s = lambda a: a.reshape(DB, DS, a.shape[-1])
    m0 = jnp.concatenate([state_mlstm_m[l], jnp.zeros((DB, LANES - ML_HEADS), F32)], axis=1)[:, None, :]
    hm_s, c_s, n_s, m_s = _mlstm(r3s(qm), r3s(km), r3s(vm), r3s(og), r3s(gc),
                                 state_mlstm_c[l], state_mlstm_n[l], m0, ml_nw, L_in=DS)
    kc = cache_swa_k[l].reshape(DB, WINDOW, SWA_KV_WIDTH)
    vc = cache_swa_v[l].reshape(DB, WINDOW, SWA_KV_WIDTH)
    hs_s, k_win_s, v_win_s = _swa_sample(r3s(sq), r3s(sk), r3s(sv), kc, vc, sinks)
    y_s = _ffn(hm_s.reshape(DB * DS, ML_WIDTH), hs_s.reshape(DB * DS, SWA_WIDTH), xs, W)
    kv_shape_s = (1, DB, WINDOW, SWA_KV_HEADS, SWA_HEAD_DIM)

    return (y_p.reshape(B, S, D_MODEL), y_s.reshape(DB, DS, D_MODEL),
            k_win_p, v_win_p, c_p[None], n_p[None], m_p[None, :, 0, :ML_HEADS],
            k_win_s.reshape(kv_shape_s), v_win_s.reshape(kv_shape_s),
            c_s[None], n_s[None], m_s[None, :, 0, :ML_HEADS])
```

```python
import functools

import jax
import jax.numpy as jnp
import numpy as np
from jax import lax
from jax.experimental import pallas as pl
from jax.experimental.pallas import tpu as pltpu

F32 = jnp.float32
BF16 = jnp.bfloat16

D_MODEL = 1024
SEQ = 8192
DEC_SEQ = 32
PAST_LEN = 4096
NORM_EPS = 1e-6
ML_HEADS = 4
ML_HEAD_DIM = 128
ML_WIDTH = ML_HEADS * ML_HEAD_DIM
SWA_HEADS = 8
SWA_KV_HEADS = 2
SWA_GROUP = SWA_HEADS // SWA_KV_HEADS
SWA_HEAD_DIM = 64
SWA_WIDTH = SWA_HEADS * SWA_HEAD_DIM
SWA_KV_WIDTH = SWA_KV_HEADS * SWA_HEAD_DIM
WINDOW = 128
SWA_CHUNK = 64
ROPE_THETA = 500000.0
ROPE_DIM = SWA_HEAD_DIM // 4
ROPE_HALF = ROPE_DIM // 2
PEER_HEADS = 8
N_KEYS = 128
N_EXPERTS = N_KEYS * N_KEYS
PEER_TOPK = 16
PEER_KEY_DIM = 256
PEER_HALF = PEER_KEY_DIM // 2

LANES = 128
VMEM_LIMIT = 56 * 1024 * 1024

COL_MQ, COL_MK, COL_MV, COL_MO = 0, ML_WIDTH, 2 * ML_WIDTH, 3 * ML_WIDTH
COL_SQ = 4 * ML_WIDTH
COL_SK = COL_SQ + SWA_WIDTH
COL_SV = COL_SK + SWA_KV_WIDTH
COL_G = COL_SV + SWA_KV_WIDTH
IN_COLS_PAD = COL_G + LANES

TM_INPROJ = 512
ML_CHUNK = 128
SWA_TQ = 256
TM_OUTPROJ = 512
TM_ROUTE = 256
TM_PEER = 512
PEER_SUB = 512
PEER_NSUB = 4
PEER_RB = 16
PACK_ROWS = 512

NT_DIMS = (((1,), (1,)), ((), ()))
TN_DIMS = (((0,), (0,)), ((), ()))


def _rms(x, w):
    return x * lax.rsqrt(jnp.mean(x * x, axis=-1, keepdims=True) + NORM_EPS) * w


def _inproj_kernel(x_ref, nw_ref, w_ref, bias_ref, cos_ref, sina_ref, sinb_ref,
                   qm_ref, km_ref, vm_ref, og_ref, sq_ref, sk_ref, sv_ref, gc_ref):
    h = _rms(x_ref[...], nw_ref[...])
    proj = jnp.dot(h.astype(BF16), w_ref[...], preferred_element_type=F32)
    qm_ref[...] = proj[:, COL_MQ:COL_MQ + ML_WIDTH]
    km_ref[...] = proj[:, COL_MK:COL_MK + ML_WIDTH] * (ML_HEAD_DIM ** -0.5)
    vm_ref[...] = proj[:, COL_MV:COL_MV + ML_WIDTH]
    og_ref[...] = jax.nn.sigmoid(proj[:, COL_MO:COL_MO + ML_WIDTH])
    cosf, sina, sinb = cos_ref[...], sina_ref[...], sinb_ref[...]

    def rope(xc):
        return (xc * cosf + pltpu.roll(xc, LANES - ROPE_HALF, 1) * sina
                + pltpu.roll(xc, ROPE_HALF, 1) * sinb)

    for j in range(SWA_WIDTH // LANES):
        sq_ref[:, j * LANES:(j + 1) * LANES] = rope(proj[:, COL_SQ + j * LANES:COL_SQ + (j + 1) * LANES])
    sk_ref[...] = rope(proj[:, COL_SK:COL_SK + LANES])
    sv_ref[...] = proj[:, COL_SV:COL_SV + LANES]
    g = proj[:, COL_G:COL_G + LANES] + bias_ref[...]
    lane = lax.broadcasted_iota(jnp.int32, g.shape, 1)
    gc_ref[...] = jnp.where(lane < ML_HEADS, g, jax.nn.log_sigmoid(g))


def _inproj(x2d, nw, w_perm, bias_pad, cos_t, sina_t, sinb_t, tab_map):
    T = x2d.shape[0]
    TM = TM_INPROJ
    row = lambda i: (i, 0)
    const = lambda i: (0, 0)
    f = lambda n: jax.ShapeDtypeStruct((T, n), F32)
    return pl.pallas_call(
        _inproj_kernel,
        grid=(T // TM,),
        in_specs=[pl.BlockSpec((TM, D_MODEL), row),
                  pl.BlockSpec((1, D_MODEL), const),
                  pl.BlockSpec((D_MODEL, IN_COLS_PAD), const),
                  pl.BlockSpec((1, LANES), const),
                  pl.BlockSpec((TM, LANES), tab_map),
                  pl.BlockSpec((TM, LANES), tab_map),
                  pl.BlockSpec((TM, LANES), tab_map)],
        out_specs=[pl.BlockSpec((TM, ML_WIDTH), row)] * 4
                  + [pl.BlockSpec((TM, SWA_WIDTH), row),
                     pl.BlockSpec((TM, LANES), row), pl.BlockSpec((TM, LANES), row),
                     pl.BlockSpec((TM, LANES), row)],
        out_shape=[f(ML_WIDTH)] * 4 + [f(SWA_WIDTH), f(LANES), f(LANES), f(LANES)],
        compiler_params=pltpu.CompilerParams(dimension_semantics=("arbitrary",),
                                             vmem_limit_bytes=VMEM_LIMIT),
        name="inproj",
    )(x2d, nw, w_perm, bias_pad, cos_t, sina_t, sinb_t)


def _mlstm_kernel(q_ref, k_ref, v_ref, og_ref, gc_ref, c0_ref, n0_ref, m0_ref, nw_ref,
                  hm_ref, c_ref, n_ref, m_ref, *, BB, L_in):
    LP = ML_CHUNK

    @pl.when(pl.program_id(1) == 0)
    def _():
        c_ref[...] = c0_ref[...]
        n_ref[...] = n0_ref[...]
        m_ref[...] = m0_ref[...]

    src = lax.broadcasted_iota(jnp.int32, (LP, LP), 0)
    qry = lax.broadcasted_iota(jnp.int32, (LP, LP), 1)
    visible = src <= qry
    tri_t = visible.astype(F32)
    row8 = lax.broadcasted_iota(jnp.int32, (8, LP), 0)
    lane1 = lax.broadcasted_iota(jnp.int32, (1, LANES), 1)
    hi = lax.Precision.HIGHEST

    def pad_rows(a):
        if L_in == LP:
            return a
        return jnp.concatenate([a, jnp.zeros((LP - L_in, a.shape[1]), a.dtype)], axis=0)

    streams = [(b, h) for b in range(BB) for h in range(ML_HEADS)]
    gr, b_r, m_all, qkv = {}, {}, {}, {}
    for b in range(BB):
        gc = gc_ref[b]
        if L_in < LP:
            lane_pad = lax.broadcasted_iota(jnp.int32, (LP - L_in, LANES), 1)
            fill = jnp.where(lane_pad < ML_HEADS, -jnp.inf, 0.0).astype(F32)
            gc = jnp.concatenate([gc, fill], axis=0)
        gr[b] = gc.T[:8]
        lf_r = jnp.where(row8 >= ML_HEADS, gr[b], 0.0)
        b_r[b] = jnp.dot(lf_r, tri_t, precision=hi, preferred_element_type=F32)
        m_all[b] = m_ref[b]
        qkv[b] = (pad_rows(q_ref[b]), pad_rows(k_ref[b]), pad_rows(v_ref[b]))

    st = {}
    for b, h in streams:
        sl = slice(h * ML_HEAD_DIM, (h + 1) * ML_HEAD_DIM)
        q, k, v = (a[:, sl] for a in qkv[b])
        ig_r = gr[b][h:h + 1, :]
        b_rh = b_r[b][ML_HEADS + h:ML_HEADS + h + 1, :]
        m_prev = m_all[b][:, h:h + 1]
        src_term = jnp.broadcast_to(ig_r - b_rh, (LP, LP)).T
        logd = jnp.where(visible, b_rh + src_term, -jnp.inf)
        m_inter = b_rh + m_prev
        m_t = jnp.maximum(m_inter, jnp.max(logd, axis=0, keepdims=True))
        b_last = b_rh[:, LP - 1:LP]
        logw = b_last - b_rh + ig_r
        m_new = jnp.maximum(b_last + m_prev, jnp.max(logw, axis=1, keepdims=True))
        st[b, h] = dict(sl=sl, q=q, k=k, qb=q.astype(BF16), kb=k.astype(BF16), v_t=v.T, m_t=m_t,
                        dmat=jnp.exp(logd - m_t), w_int=jnp.exp(m_inter - m_t), m_new=m_new,
                        w_r=jnp.exp(logw - m_new), decay=jnp.exp(b_last + m_prev - m_new),
                        c_old=c_ref[b, h], n_old=n_ref[b, h:h + 1, :])

    for key in streams:
        d = st[key]
        d["s"] = lax.dot_general(d["kb"], d["qb"], NT_DIMS, preferred_element_type=F32)
        d["cq"] = lax.dot_general(d["c_old"].astype(BF16), d["qb"], NT_DIMS, preferred_element_type=F32)
        d["nq"] = lax.dot_general(jnp.broadcast_to(d["n_old"], (8, ML_HEAD_DIM)), d["q"], NT_DIMS,
                                  precision=hi, preferred_element_type=F32)[0:1]

    for b, h in streams:
        d = st[b, h]
        c_ref[b, h] = d["decay"] * d["c_old"] + jnp.dot(
            (d["v_t"] * d["w_r"]).astype(BF16), d["kb"], preferred_element_type=F32)
        n_ref[b, h:h + 1, :] = d["decay"] * d["n_old"] + jnp.dot(
            jnp.broadcast_to(d["w_r"], (8, LP)), d["k"], precision=hi, preferred_element_type=F32)[0:1]

    for key in streams:
        d = st[key]
        qk = d["s"] * d["dmat"]
        num = d["w_int"] * d["cq"] + jnp.dot(d["v_t"].astype(BF16), qk.astype(BF16),
                                             preferred_element_type=F32)
        den = d["w_int"] * d["nq"] + jnp.sum(qk, axis=0, keepdims=True)
        d["hh"] = num / jnp.maximum(jnp.abs(den), jnp.exp(-d["m_t"]))

    for b in range(BB):
        og_all = og_ref[b]
        m_out = m_all[b]
        for h in range(ML_HEADS):
            d = st[b, h]
            hh, sl = d["hh"], d["sl"]
            y_t = hh * lax.rsqrt(jnp.mean(hh * hh, axis=0, keepdims=True) + NORM_EPS)
            y = y_t.T * nw_ref[:, sl]
            hm_ref[b, :, sl] = (og_all[:, sl] * y[:L_in]).astype(BF16)
            m_out = jnp.where(lane1 == h, d["m_new"], m_out)
        m_ref[b] = m_out


def _mlstm(qm, km, vm, og, gc, c0, n0, m0, nw, *, L_in):
    B, T, _ = qm.shape
    BB = 2
    nchunks = T // L_in
    tok = lambda n: pl.BlockSpec((BB, L_in, n), lambda g, c: (g, c, 0))
    st_c = pl.BlockSpec((BB, ML_HEADS, ML_HEAD_DIM, ML_HEAD_DIM), lambda g, c: (g, 0, 0, 0))
    st_n = pl.BlockSpec((BB, ML_HEADS, ML_HEAD_DIM), lambda g, c: (g, 0, 0))
    st_m = pl.BlockSpec((BB, 1, LANES), lambda g, c: (g, 0, 0))
    return pl.pallas_call(
        functools.partial(_mlstm_kernel, BB=BB, L_in=L_in),
        grid=(B // BB, nchunks),
        in_specs=[tok(ML_WIDTH), tok(ML_WIDTH), tok(ML_WIDTH), tok(ML_WIDTH), tok(LANES),
                  st_c, st_n, st_m, pl.BlockSpec((1, ML_WIDTH), lambda g, c: (0, 0))],
        out_specs=[tok(ML_WIDTH), st_c, st_n, st_m],
        out_shape=[jax.ShapeDtypeStruct((B, T, ML_WIDTH), BF16),
                   jax.ShapeDtypeStruct(c0.shape, F32),
                   jax.ShapeDtypeStruct(n0.shape, F32),
                   jax.ShapeDtypeStruct(m0.shape, F32)],
        compiler_params=pltpu.CompilerParams(dimension_semantics=("arbitrary", "arbitrary"),
                                             vmem_limit_bytes=VMEM_LIMIT),
        name="mlstm",
    )(qm, km, vm, og, gc, c0, n0, m0, nw)


def _sink_attention(q4, kw, vw, sink_col, bias):
    s = lax.dot_general(q4, kw, NT_DIMS, preferred_element_type=F32) * (SWA_HEAD_DIM ** -0.5)
    if bias is not None:
        s = s + bias
    mx = jnp.maximum(jnp.max(s, axis=1, keepdims=True), sink_col)
    p = jnp.exp(s - mx)
    den = jnp.sum(p, axis=1, keepdims=True) + jnp.exp(sink_col - mx)
    return jnp.dot(p.astype(BF16), vw, preferred_element_type=F32) / den


def _sink_column(sink_ref, g, rows):
    return jnp.concatenate(
        [jnp.broadcast_to(sink_ref[:, g * SWA_GROUP + j:g * SWA_GROUP + j + 1], (rows, 1))
         for j in range(SWA_GROUP)], axis=0)


def _swa_prompt_kernel(q_ref, kc_ref, kp_ref, vc_ref, vp_ref, sink_ref, bias_ref, o_ref):
    TQ = SWA_TQ
    q = q_ref[0]
    kcat = jnp.concatenate([kp_ref[0, TQ - WINDOW:, :], kc_ref[0]], axis=0).astype(BF16)
    vcat_t = jnp.concatenate([vp_ref[0, TQ - WINDOW:, :], vc_ref[0]], axis=0).T.astype(BF16)
    bias_t = bias_ref[0]
    outs = []
    for g in range(SWA_KV_HEADS):
        gs = slice(g * SWA_HEAD_DIM, (g + 1) * SWA_HEAD_DIM)
        heads = range(g * SWA_GROUP, (g + 1) * SWA_GROUP)
        q4 = jnp.concatenate([q[:, h * SWA_HEAD_DIM:(h + 1) * SWA_HEAD_DIM] for h in heads],
                             axis=0).astype(BF16)
        sink_row = jnp.concatenate([jnp.broadcast_to(sink_ref[:, h:h + 1], (1, TQ)) for h in heads], axis=1)
        s_t = lax.dot_general(kcat[:, gs], q4, NT_DIMS, preferred_element_type=F32)
        s_t = s_t * (SWA_HEAD_DIM ** -0.5) + bias_t
        mx = jnp.maximum(jnp.max(s_t, axis=0, keepdims=True), sink_row)
        p = jnp.exp(s_t - mx)
        den = jnp.sum(p, axis=0, keepdims=True) + jnp.exp(sink_row - mx)
        o_t = jnp.dot(vcat_t[gs, :], p.astype(BF16), preferred_element_type=F32) / den
        outs += [o_t[:, j * TQ:(j + 1) * TQ] for j in range(SWA_GROUP)]
    o_ref[0] = jnp.concatenate(outs, axis=0).T.astype(BF16)


def _swa_prompt_bias():
    TQ = SWA_TQ
    t = np.arange(SWA_GROUP * TQ) % TQ
    k = np.arange(WINDOW + TQ)
    qc = (t // SWA_CHUNK)[None, :]
    kc = (k // SWA_CHUNK)[:, None]
    visible = (kc >= qc) & (kc <= qc + WINDOW // SWA_CHUNK)
    first = visible & (k[:, None] >= WINDOW)
    return np.where(np.stack([first, visible]), 0.0, -np.inf).astype(np.float32)


def _swa_prompt(sq, sk, sv, sinks):
    B, T, _ = sq.shape
    TQ = SWA_TQ
    cur = lambda b, i: (b, i, 0)
    prev = lambda b, i: (b, jnp.maximum(i - 1, 0), 0)
    bias = jnp.asarray(_swa_prompt_bias())
    return pl.pallas_call(
        _swa_prompt_kernel,
        grid=(B, T // TQ),
        in_specs=[pl.BlockSpec((1, TQ, SWA_WIDTH), cur),
                  pl.BlockSpec((1, TQ, SWA_KV_WIDTH), cur), pl.BlockSpec((1, TQ, SWA_KV_WIDTH), prev),
                  pl.BlockSpec((1, TQ, SWA_KV_WIDTH), cur), pl.BlockSpec((1, TQ, SWA_KV_WIDTH), prev),
                  pl.BlockSpec((1, SWA_HEADS), lambda b, i: (0, 0)),
                  pl.BlockSpec((1,) + bias.shape[1:], lambda b, i: (jnp.minimum(i, 1), 0, 0))],
        out_specs=pl.BlockSpec((1, TQ, SWA_WIDTH), cur),
        out_shape=jax.ShapeDtypeStruct((B, T, SWA_WIDTH), BF16),
        compiler_params=pltpu.CompilerParams(dimension_semantics=("arbitrary", "arbitrary"),
                                             vmem_limit_bytes=VMEM_LIMIT),
        name="swa_prompt",
    )(sq, sk, sk, sv, sv, sinks, bias)


def _swa_sample_kernel(q_ref, kn_ref, vn_ref, kc_ref, vc_ref, sink_ref, o_ref, kw_ref, vw_ref):
    T = DEC_SEQ
    q = q_ref[0]
    k_all = jnp.concatenate([kc_ref[0], kn_ref[0]], axis=0)
    v_all = jnp.concatenate([vc_ref[0], vn_ref[0]], axis=0)
    kw_ref[0] = k_all[T:]
    vw_ref[0] = v_all[T:]
    kb, vb = k_all.astype(BF16), v_all.astype(BF16)
    for g in range(SWA_KV_HEADS):
        gs = slice(g * SWA_HEAD_DIM, (g + 1) * SWA_HEAD_DIM)
        q4 = jnp.concatenate(
            [q[:, (g * SWA_GROUP + j) * SWA_HEAD_DIM:(g * SWA_GROUP + j + 1) * SWA_HEAD_DIM]
             for j in range(SWA_GROUP)], axis=0).astype(BF16)
        o = _sink_attention(q4, kb[:, gs], vb[:, gs], _sink_column(sink_ref, g, T), None)
        for j in range(SWA_GROUP):
            hd = (g * SWA_GROUP + j) * SWA_HEAD_DIM
            o_ref[0, :, hd:hd + SWA_HEAD_DIM] = o[j * T:(j + 1) * T].astype(BF16)


def _swa_sample(sq, sk, sv, k_cache, v_cache, sinks):
    B, T, _ = sq.shape
    b3 = lambda b: (b, 0, 0)
    return pl.pallas_call(
        _swa_sample_kernel,
        grid=(B,),
        in_specs=[pl.BlockSpec((1, T, SWA_WIDTH), b3),
                  pl.BlockSpec((1, T, SWA_KV_WIDTH), b3), pl.BlockSpec((1, T, SWA_KV_WIDTH), b3),
                  pl.BlockSpec((1, WINDOW, SWA_KV_WIDTH), b3), pl.BlockSpec((1, WINDOW, SWA_KV_WIDTH), b3),
                  pl.BlockSpec((1, SWA_HEADS), lambda b: (0, 0))],
        out_specs=[pl.BlockSpec((1, T, SWA_WIDTH), b3),
                   pl.BlockSpec((1, WINDOW, SWA_KV_WIDTH), b3), pl.BlockSpec((1, WINDOW, SWA_KV_WIDTH), b3)],
        out_shape=[jax.ShapeDtypeStruct((B, T, SWA_WIDTH), BF16),
                   jax.ShapeDtypeStruct((B, WINDOW, SWA_KV_WIDTH), F32),
                   jax.ShapeDtypeStruct((B, WINDOW, SWA_KV_WIDTH), F32)],
        compiler_params=pltpu.CompilerParams(dimension_semantics=("arbitrary",),
                                             vmem_limit_bytes=VMEM_LIMIT),
        name="swa_sample",
    )(sq, sk, sv, k_cache, v_cache, sinks)


def _outproj_kernel(hm_ref, hs_ref, x_ref, wo_ref, nw_ref, wq_ref, x1_ref, h2_ref, q_ref):
    mix = (jnp.dot(hm_ref[...], wo_ref[:ML_WIDTH, :], preferred_element_type=F32)
           + jnp.dot(hs_ref[...], wo_ref[ML_WIDTH:, :], preferred_element_type=F32))
    x1 = x_ref[...] + mix
    x1_ref[...] = x1
    h2 = _rms(x1, nw_ref[...]).astype(BF16)
    h2_ref[...] = pltpu.bitcast(h2, jnp.uint32)
    q = jnp.dot(h2, wq_ref[...], preferred_element_type=F32)
    q_ref[...] = pltpu.bitcast(q.astype(BF16), jnp.uint32)


def _outproj(hm, hs, x2d, wo, nw, wq):
    T = x2d.shape[0]
    TM = TM_OUTPROJ
    QW = PEER_HEADS * PEER_KEY_DIM
    row = lambda i: (i, 0)
    const = lambda i: (0, 0)
    return pl.pallas_call(
        _outproj_kernel,
        grid=(T // TM,),
        in_specs=[pl.BlockSpec((TM, ML_WIDTH), row), pl.BlockSpec((TM, SWA_WIDTH), row),
                  pl.BlockSpec((TM, D_MODEL), row),
                  pl.BlockSpec((D_MODEL, D_MODEL), const), pl.BlockSpec((1, D_MODEL), const),
                  pl.BlockSpec((D_MODEL, QW), const)],
        out_specs=[pl.BlockSpec((TM, D_MODEL), row), pl.BlockSpec((TM // 2, D_MODEL), row),
                   pl.BlockSpec((TM // 2, QW), row)],
        out_shape=[jax.ShapeDtypeStruct((T, D_MODEL), F32),
                   jax.ShapeDtypeStruct((T // 2, D_MODEL), jnp.uint32),
                   jax.ShapeDtypeStruct((T // 2, QW), jnp.uint32)],
        compiler_params=pltpu.CompilerParams(dimension_semantics=("arbitrary",),
                                             vmem_limit_bytes=VMEM_LIMIT),
        name="outproj",
    )(hm, hs, x2d, wo, nw, wq)


_CAND_NB = [PEER_TOPK // (a + 1) for a in range(PEER_TOPK)]
_CAND_ROWS = 16 + 8 * 7 + 8


def _extract_top16(S, exact):
    R = S.shape[0]
    iota = lax.broadcasted_iota(jnp.int32, S.shape, 0)
    rank = jnp.full(S.shape, float(PEER_TOPK), F32)
    vals = []
    for r in range(PEER_TOPK):
        mx = jnp.max(S, axis=0, keepdims=True)
        if exact:
            idx = jnp.min(jnp.where(S == mx, iota, R), axis=0, keepdims=True)
            hit = iota == idx
        else:
            hit = S == mx
        rank = jnp.where(hit, float(r), rank)
        S = jnp.where(hit, -jnp.inf, S)
        vals.append(mx)
    return vals, rank


def _count_excess(flags):
    return jnp.abs(jnp.sum(flags, axis=0, keepdims=True) - float(PEER_TOPK))


def _route_tables(q, sk1_ref, sk2_ref, r2_ref, e2_ref, lim_ref, e1_ref, exact):
    N = q.shape[0]
    row8 = lax.broadcasted_iota(jnp.int32, (8, N), 0)
    rowc = lax.broadcasted_iota(jnp.int32, (_CAND_ROWS, N), 0)
    mid = rowc - 16
    flat = jnp.where(rowc < 16, rowc,
                     jnp.where(rowc < _CAND_ROWS - 8,
                               PEER_TOPK * ((mid >> 3) + 1) + (mid & 7),
                               PEER_TOPK * (rowc - (_CAND_ROWS - 16))))
    excess = jnp.zeros((1, N), F32)
    for h in range(PEER_HEADS):
        c0 = h * PEER_KEY_DIM
        s1 = lax.dot_general(sk1_ref[...], q[:, c0:c0 + PEER_HALF], NT_DIMS, preferred_element_type=F32)
        s2 = lax.dot_general(sk2_ref[...], q[:, c0 + PEER_HALF:c0 + PEER_KEY_DIM], NT_DIMS,
                             preferred_element_type=F32)
        t1, r1 = _extract_top16(s1, exact)
        t2, r2 = _extract_top16(s2, exact)
        t2_16 = jnp.concatenate(t2, axis=0)
        t2_8 = t2_16[:8]
        blocks = [t1[0] + t2_16, t1[1] + t2_8]
        for a in range(2, 8):
            blocks.append(jnp.where(row8 < _CAND_NB[a], t1[a] + t2_8, -jnp.inf))
        blocks.append(jnp.concatenate(t1[8:], axis=0) + t2[0])
        cand = jnp.concatenate(blocks, axis=0)
        sel = jnp.zeros(cand.shape, F32)
        work = cand
        for _ in range(PEER_TOPK):
            mx = jnp.max(work, axis=0, keepdims=True)
            if exact:
                idx = jnp.min(jnp.where(work == mx, flat, PEER_TOPK * PEER_TOPK), axis=0, keepdims=True)
                hit = flat == idx
            else:
                hit = work == mx
            sel = jnp.where(hit, 1.0, sel)
            work = jnp.where(hit, -jnp.inf, work)
        if not exact:
            excess = jnp.maximum(excess, _count_excess(jnp.where(r1 < float(PEER_TOPK), 1.0, 0.0)))
            excess = jnp.maximum(excess, _count_excess(jnp.where(r2 < float(PEER_TOPK), 1.0, 0.0)))
            excess = jnp.maximum(excess, _count_excess(sel))
        z = jnp.sum(jnp.where(sel > 0.0, jnp.exp(cand - cand[0:1]), 0.0), axis=0, keepdims=True)
        counts = [jnp.sum(sel[0:16], axis=0, keepdims=True)]
        for a in range(1, 8):
            counts.append(jnp.sum(sel[8 + 8 * a:16 + 8 * a], axis=0, keepdims=True))
        for a in range(8, PEER_TOPK):
            counts.append(sel[_CAND_ROWS - 16 + a:_CAND_ROWS - 15 + a])
        lim = jnp.zeros(r1.shape, F32)
        for a in range(PEER_TOPK):
            lim = jnp.where(r1 == float(a), counts[a], lim)
        r2_ref[h] = pltpu.bitcast(r2.astype(BF16), jnp.uint32)
        e2_ref[h] = pltpu.bitcast(jnp.exp(s2 - t2[0]).astype(BF16), jnp.uint32)
        lim_ref[h] = lim
        e1_ref[h] = jnp.exp(s1 - t1[0]) / z
    return excess


def _route_kernel(q_ref, sk1_ref, sk2_ref, r2_ref, e2_ref, lim_ref, e1_ref):
    q = pltpu.bitcast(q_ref[...], BF16)
    groups = range(q.shape[0] // LANES)
    outs = lambda gi: tuple(ref.at[gi] for ref in (r2_ref, e2_ref, lim_ref, e1_ref))
    excess = None
    for gi in groups:
        ex = _route_tables(q[gi * LANES:(gi + 1) * LANES], sk1_ref, sk2_ref, *outs(gi), exact=False)
        excess = ex if excess is None else jnp.maximum(excess, ex)

    @pl.when(jnp.max(excess) > 0.0)
    def _():
        for gi in groups:
            _route_tables(q[gi * LANES:(gi + 1) * LANES], sk1_ref, sk2_ref, *outs(gi), exact=True)


def _route(q, sk1, sk2):
    T = 2 * q.shape[0]
    TM = TM_ROUTE
    G = TM // LANES
    const = lambda i: (0, 0)
    tab = pl.BlockSpec((G, PEER_HEADS, N_KEYS, LANES), lambda i: (i, 0, 0, 0))
    tab_packed = pl.BlockSpec((G, PEER_HEADS, N_KEYS // 2, LANES), lambda i: (i, 0, 0, 0))
    tab_shape = jax.ShapeDtypeStruct((T // LANES, PEER_HEADS, N_KEYS, LANES), F32)
    tab_packed_shape = jax.ShapeDtypeStruct((T // LANES, PEER_HEADS, N_KEYS // 2, LANES), jnp.uint32)
    return pl.pallas_call(
        _route_kernel,
        grid=(T // TM,),
        in_specs=[pl.BlockSpec((TM // 2, PEER_HEADS * PEER_KEY_DIM), lambda i: (i, 0)),
                  pl.BlockSpec((N_KEYS, PEER_HALF), const), pl.BlockSpec((N_KEYS, PEER_HALF), const)],
        out_specs=[tab_packed, tab_packed, tab, tab],
        out_shape=[tab_packed_shape, tab_packed_shape, tab_shape, tab_shape],
        compiler_params=pltpu.CompilerParams(dimension_semantics=("arbitrary",),
                                             vmem_limit_bytes=VMEM_LIMIT),
        name="route",
    )(q, sk1, sk2)


def _peer_kernel(h2_ref, x1_ref, u_ref, vt_ref, r2_ref, e2_ref, lim_ref, e1_ref, nfw_ref,
                 y_ref, acc_ref, act0_ref, act1_ref, p0_ref, p1_ref, *, n_e, n_work):
    g = pl.program_id(0)
    per_blk = PEER_SUB * PEER_NSUB // N_KEYS
    e_b = jnp.clip(g - 1, 0, n_work - 1) % n_e
    e_c = jnp.clip(g - 2, 0, n_work - 1) % n_e
    act_bufs = (act0_ref, act1_ref)
    p_bufs = (p0_ref, p1_ref)

    @pl.when(g == 0)
    def _():
        for buf in act_bufs + p_bufs:
            buf[...] = jnp.zeros(buf.shape, buf.dtype)

    @pl.when(e_c == 0)
    def _():
        acc_ref[...] = jnp.zeros(acc_ref.shape, F32)

    def stages(slot_a, slot_b):
        def stage_a(s):
            h2 = pltpu.bitcast(h2_ref[...], BF16)
            u = pltpu.bitcast(u_ref[s * PEER_SUB // 2:(s + 1) * PEER_SUB // 2, :], BF16)
            act_bufs[slot_a][s * PEER_SUB:(s + 1) * PEER_SUB, :] = lax.dot_general(
                u, h2, NT_DIMS, preferred_element_type=F32)

        def stage_b(jbs):
            n_rb = N_KEYS // PEER_RB
            zero = jnp.zeros((PEER_RB, LANES), BF16)
            for jb in jbs:
                j = e_b * per_blk + jb
                for lh in range(TM_PEER // LANES):
                    cols = slice(lh * LANES, (lh + 1) * LANES)
                    gw = [None] * n_rb
                    for h in range(PEER_HEADS):
                        lim = jnp.broadcast_to(lim_ref[lh, h, pl.ds(j, 1), :], (PEER_RB, LANES)).astype(BF16)
                        e1 = jnp.broadcast_to(e1_ref[lh, h, pl.ds(j, 1), :], (PEER_RB, LANES)).astype(BF16)
                        for rb in range(n_rb):
                            words = slice(rb * PEER_RB // 2, (rb + 1) * PEER_RB // 2)
                            r2 = pltpu.bitcast(r2_ref[lh, h, words, :], BF16)
                            e2 = pltpu.bitcast(e2_ref[lh, h, words, :], BF16)
                            t = jnp.where(r2 < lim, e2, zero) * e1
                            gw[rb] = t if gw[rb] is None else gw[rb] + t
                    for rb in range(n_rb):
                        arows = slice(jb * N_KEYS + rb * PEER_RB, jb * N_KEYS + (rb + 1) * PEER_RB)
                        a = act_bufs[slot_b][arows, cols]
                        ga = 0.5 * a * (1.0 + lax.erf(a * np.float32(np.sqrt(0.5))))
                        p_bufs[slot_b][arows, cols] = gw[rb] * ga.astype(BF16)

        def stage_c():
            acc_ref[...] += jnp.dot(pltpu.bitcast(vt_ref[...], BF16), p_bufs[slot_a][...],
                                    preferred_element_type=F32)

        part = per_blk // PEER_NSUB
        for s in range(PEER_NSUB):
            stage_b(range(s * part, (s + 1) * part))
            stage_a(s)
        stage_c()

    @pl.when(g % 2 == 0)
    def _():
        stages(0, 1)

    @pl.when(g % 2 == 1)
    def _():
        stages(1, 0)

    @pl.when(jnp.logical_and(g >= 2, e_c == n_e - 1))
    def _():
        x = x1_ref[...] + acc_ref[...].T
        y_ref[...] = _rms(x, nfw_ref[...])


def _peer(h2, x1, u, vt, r2, e2, lim, e1, nfw):
    T = x1.shape[0]
    TM = TM_PEER
    ET = PEER_SUB * PEER_NSUB
    n_e = N_EXPERTS // ET
    n_work = (T // TM) * n_e
    item = lambda g, lag: jnp.clip(g - lag, 0, n_work - 1)
    rt = TM // LANES
    tab = pl.BlockSpec((rt, PEER_HEADS, N_KEYS, LANES), lambda g: (item(g, 1) // n_e, 0, 0, 0))
    tab_packed = pl.BlockSpec((rt, PEER_HEADS, N_KEYS // 2, LANES),
                              lambda g: (item(g, 1) // n_e, 0, 0, 0))
    tok_c = lambda g: (item(g, 2) // n_e, 0)
    return pl.pallas_call(
        functools.partial(_peer_kernel, n_e=n_e, n_work=n_work),
        grid=(n_work + 2,),
        in_specs=[pl.BlockSpec((TM // 2, D_MODEL), lambda g: (item(g, 0) // n_e, 0)),
                  pl.BlockSpec((TM, D_MODEL), tok_c),
                  pl.BlockSpec((ET // 2, D_MODEL), lambda g: (item(g, 0) % n_e, 0)),
                  pl.BlockSpec((D_MODEL // 2, ET), lambda g: (0, item(g, 2) % n_e)),
                  tab_packed, tab_packed, tab, tab,
                  pl.BlockSpec((1, D_MODEL), lambda g: (0, 0))],
        out_specs=pl.BlockSpec((TM, D_MODEL), tok_c),
        out_shape=jax.ShapeDtypeStruct((T, D_MODEL), F32),
        scratch_shapes=[pltpu.VMEM((D_MODEL, TM), F32),
                        pltpu.VMEM((ET, TM), F32), pltpu.VMEM((ET, TM), F32),
                        pltpu.VMEM((ET, TM), BF16), pltpu.VMEM((ET, TM), BF16)],
        compiler_params=pltpu.CompilerParams(dimension_semantics=("arbitrary",),
                                             vmem_limit_bytes=VMEM_LIMIT),
        name="peer",
    )(h2, x1, u, vt, r2, e2, lim, e1, nfw)


def _rope_tables(pos):
    inv = ROPE_THETA ** (-jnp.arange(ROPE_HALF, dtype=F32) * 2.0 / ROPE_DIM)
    ang = pos.astype(F32)[:, None] * inv[None, :]
    cos, sin = jnp.cos(ang), jnp.sin(ang)
    n = pos.shape[0]
    rest = SWA_HEAD_DIM - ROPE_DIM
    zh = jnp.zeros((n, ROPE_HALF), F32)
    cos_h = jnp.concatenate([cos, cos, jnp.ones((n, rest), F32)], axis=1)
    sina_h = jnp.concatenate([-sin, zh, jnp.zeros((n, rest), F32)], axis=1)
    sinb_h = jnp.concatenate([zh, sin, jnp.zeros((n, rest), F32)], axis=1)
    rep = LANES // SWA_HEAD_DIM
    return tuple(jnp.tile(t, (1, rep)) for t in (cos_h, sina_h, sinb_h))


def _pack_kernel(x_ref, o_ref, *, transpose):
    x = x_ref[...]
    if transpose:
        x = x.T
    o_ref[...] = pltpu.bitcast(x.astype(BF16), jnp.uint32)


def _pack_expert_table(w, *, transpose):
    n, d = w.shape
    rows = PACK_ROWS
    if transpose:
        out_spec = pl.BlockSpec((d // 2, rows), lambda i: (0, i))
        out_shape = jax.ShapeDtypeStruct((d // 2, n), jnp.uint32)
    else:
        out_spec = pl.BlockSpec((rows // 2, d), lambda i: (i, 0))
        out_shape = jax.ShapeDtypeStruct((n // 2, d), jnp.uint32)
    return pl.pallas_call(
        functools.partial(_pack_kernel, transpose=transpose),
        grid=(n // rows,),
        in_specs=[pl.BlockSpec((rows, d), lambda i: (i, 0))],
        out_specs=out_spec,
        out_shape=out_shape,
        compiler_params=pltpu.CompilerParams(dimension_semantics=("arbitrary",),
                                             vmem_limit_bytes=VMEM_LIMIT),
        name="pack_vt" if transpose else "pack_u",
    )(w)


def _layer_tokens(x2d, tables, tab_map, W):
    return _inproj(x2d, W["norm_mix"], W["w_in"], W["bias"], *tables, tab_map)


def _ffn(hm, hs, x2d, W):
    x1, h2, q = _outproj(hm, hs, x2d, W["w_out"], W["norm_ffn"], W["w_q"])
    r2, e2, lim, e1 = _route(q, W["sk1"], W["sk2"])
    return _peer(h2, x1, W["u"], W["vt"], r2, e2, lim, e1, W["norm_final"])


def kernel(x_prompt, x_sample, cache_swa_k, cache_swa_v, state_mlstm_c, state_mlstm_n, state_mlstm_m,
           norm_mix_w, w_in, mlstm_if_bias, mlstm_norm_w, swa_sinks, w_out, norm_ffn_w,
           peer_w_q, peer_sub_keys_1, peer_sub_keys_2, peer_u, peer_v, norm_final_w):
    B, S, _ = x_prompt.shape
    DB, DS, _ = x_sample.shape
    l = 0
    wi = w_in[l]
    s_q = 4 * ML_WIDTH + 2 * ML_HEADS
    w_perm = jnp.concatenate(
        [wi[:, :4 * ML_WIDTH], wi[:, s_q:], wi[:, 4 * ML_WIDTH:s_q],
         jnp.zeros((D_MODEL, LANES - 2 * ML_HEADS), F32)], axis=1).astype(BF16)
    bias_pad = jnp.concatenate([mlstm_if_bias[l], jnp.zeros((LANES - 2 * ML_HEADS,), F32)])[None, :]
    W = {
        "norm_mix": norm_mix_w[l][None, :],
        "w_in": w_perm,
        "bias": bias_pad,
        "w_out": w_out[l].astype(BF16),
        "norm_ffn": norm_ffn_w[l][None, :],
        "w_q": peer_w_q[l].astype(BF16),
        "sk1": peer_sub_keys_1[l].astype(BF16),
        "sk2": peer_sub_keys_2[l].astype(BF16),
        "u": _pack_expert_table(peer_u[l], transpose=False),
        "vt": _pack_expert_table(peer_v[l], transpose=True),
        "norm_final": norm_final_w[None, :],
    }
    ml_nw = mlstm_norm_w[l][None, :]
    sinks = swa_sinks[l][None, :]

    xp = x_prompt.reshape(B * S, D_MODEL)
    tiles_per_seq = S // TM_INPROJ
    tabs_p = _rope_tables(jnp.arange(S, dtype=jnp.int32))
    qm, km, vm, og, sq, sk, sv, gc = _layer_tokens(xp, tabs_p, lambda i: (i % tiles_per_seq, 0), W)
    r3 = lambda a: a.reshape(B, S, a.shape[-1])
    zc = jnp.zeros((B, ML_HEADS, ML_HEAD_DIM, ML_HEAD_DIM), F32)
    zn = jnp.zeros((B, ML_HEADS, ML_HEAD_DIM), F32)
    zm = jnp.zeros((B, 1, LANES), F32)
    hm_p, c_p, n_p, m_p = _mlstm(r3(qm), r3(km), r3(vm), r3(og), r3(gc), zc, zn, zm, ml_nw, L_in=ML_CHUNK)
    sk3, sv3 = r3(sk), r3(sv)
    hs_p = _swa_prompt(r3(sq), sk3, sv3, sinks)
    y_p = _ffn(hm_p.reshape(B * S, ML_WIDTH), hs_p.reshape(B * S, SWA_WIDTH), xp, W)
    kv_shape = (1, B, WINDOW, SWA_KV_HEADS, SWA_HEAD_DIM)
    k_win_p = sk3[:, S - WINDOW:].reshape(kv_shape)
    v_win_p = sv3[:, S - WINDOW:].reshape(kv_shape)

    xs = x_sample.reshape(DB * DS, D_MODEL)
    pos_s = PAST_LEN + jnp.arange(DS, dtype=jnp.int32)
    tabs_s = tuple(jnp.tile(t, (DB, 1)) for t in _rope_tables(pos_s))
    qm, km, vm, og, sq, sk, sv, gc = _layer_tokens(xs, tabs_s, lambda i: (i, 0), W)
    r3s = lambda a: a.reshape(DB, DS, a.shape[-1])
    m0 = jnp.concatenate([state_mlstm_m[l], jnp.zeros((DB, LANES - ML_HEADS), F32)], axis=1)[:, None, :]
    hm_s, c_s, n_s, m_s = _mlstm(r3s(qm), r3s(km), r3s(vm), r3s(og), r3s(gc),
                                 state_mlstm_c[l], state_mlstm_n[l], m0, ml_nw, L_in=DS)
    kc = cache_swa_k[l].reshape(DB, WINDOW, SWA_KV_WIDTH)
    vc = cache_swa_v[l].reshape(DB, WINDOW, SWA_KV_WIDTH)
    hs_s, k_win_s, v_win_s = _swa_sample(r3s(sq), r3s(sk), r3s(sv), kc, vc, sinks)
    y_s = _ffn(hm_s.reshape(DB * DS, ML_WIDTH), hs_s.reshape(DB * DS, SWA_WIDTH), xs, W)
    kv_shape_s = (1, DB, WINDOW, SWA_KV_HEADS, SWA_HEAD_DIM)

    return (y_p.reshape(B, S, D_MODEL), y_s.reshape(DB, DS, D_MODEL),
            k_win_p, v_win_p, c_p[None], n_p[None], m_p[None, :, 0, :ML_HEADS],
            k_win_s.reshape(kv_shape_s), v_win_s.reshape(kv_shape_s),
            c_s[None], n_s[None], m_s[None, :, 0, :ML_HEADS])
```

```python
import functools

import jax
import jax.numpy as jnp
import numpy as np
from jax import lax
from jax.experimental import pallas as pl
from jax.experimental.pallas import tpu as pltpu

F32 = jnp.float32
BF16 = jnp.bfloat16

D_MODEL = 1024
SEQ = 8192
DEC_SEQ = 32
PAST_LEN = 4096
NORM_EPS = 1e-6
ML_HEADS = 4
ML_HEAD_DIM = 128
ML_WIDTH = ML_HEADS * ML_HEAD_DIM
SWA_HEADS = 8
SWA_KV_HEADS = 2
SWA_GROUP = SWA_HEADS // SWA_KV_HEADS
SWA_HEAD_DIM = 64
SWA_WIDTH = SWA_HEADS * SWA_HEAD_DIM
SWA_KV_WIDTH = SWA_KV_HEADS * SWA_HEAD_DIM
WINDOW = 128
SWA_CHUNK = 64
ROPE_THETA = 500000.0
ROPE_DIM = SWA_HEAD_DIM // 4
ROPE_HALF = ROPE_DIM // 2
PEER_HEADS = 8
N_KEYS = 128
N_EXPERTS = N_KEYS * N_KEYS
PEER_TOPK = 16
PEER_KEY_DIM = 256
PEER_HALF = PEER_KEY_DIM // 2

LANES = 128
VMEM_LIMIT = 56 * 1024 * 1024

COL_MQ, COL_MK, COL_MV, COL_MO = 0, ML_WIDTH, 2 * ML_WIDTH, 3 * ML_WIDTH
COL_SQ = 4 * ML_WIDTH
COL_SK = COL_SQ + SWA_WIDTH
COL_SV = COL_SK + SWA_KV_WIDTH
COL_G = COL_SV + SWA_KV_WIDTH
IN_COLS_PAD = COL_G + LANES

TM_INPROJ = 512
ML_CHUNK = 256
SWA_TQ = 256
TM_OUTPROJ = 512
TM_ROUTE = 128
TM_PEER = 512
PEER_SUB = 512
PEER_NSUB = 4
PEER_RB = 16
PACK_ROWS = 512

NT_DIMS = (((1,), (1,)), ((), ()))
TN_DIMS = (((0,), (0,)), ((), ()))


def _rms(x, w):
    return x * lax.rsqrt(jnp.mean(x * x, axis=-1, keepdims=True) + NORM_EPS) * w


def _inproj_kernel(x_ref, nw_ref, w_ref, bias_ref, cos_ref, sina_ref, sinb_ref,
                   qm_ref, km_ref, vm_ref, og_ref, sq_ref, sk_ref, sv_ref, gc_ref):
    h = _rms(x_ref[...], nw_ref[...])
    proj = jnp.dot(h.astype(BF16), w_ref[...], preferred_element_type=F32)
    qm_ref[...] = proj[:, COL_MQ:COL_MQ + ML_WIDTH]
    km_ref[...] = proj[:, COL_MK:COL_MK + ML_WIDTH] * (ML_HEAD_DIM ** -0.5)
    vm_ref[...] = proj[:, COL_MV:COL_MV + ML_WIDTH]
    og_ref[...] = jax.nn.sigmoid(proj[:, COL_MO:COL_MO + ML_WIDTH])
    cosf, sina, sinb = cos_ref[...], sina_ref[...], sinb_ref[...]

    def rope(xc):
        return (xc * cosf + pltpu.roll(xc, LANES - ROPE_HALF, 1) * sina
                + pltpu.roll(xc, ROPE_HALF, 1) * sinb)

    for j in range(SWA_WIDTH // LANES):
        sq_ref[:, j * LANES:(j + 1) * LANES] = rope(proj[:, COL_SQ + j * LANES:COL_SQ + (j + 1) * LANES])
    sk_ref[...] = rope(proj[:, COL_SK:COL_SK + LANES])
    sv_ref[...] = proj[:, COL_SV:COL_SV + LANES]
    g = proj[:, COL_G:COL_G + LANES] + bias_ref[...]
    lane = lax.broadcasted_iota(jnp.int32, g.shape, 1)
    gc_ref[...] = jnp.where(lane < ML_HEADS, g, jax.nn.log_sigmoid(g))


def _inproj(x2d, nw, w_perm, bias_pad, cos_t, sina_t, sinb_t, tab_map):
    T = x2d.shape[0]
    TM = TM_INPROJ
    row = lambda i: (i, 0)
    const = lambda i: (0, 0)
    f = lambda n: jax.ShapeDtypeStruct((T, n), F32)
    return pl.pallas_call(
        _inproj_kernel,
        grid=(T // TM,),
        in_specs=[pl.BlockSpec((TM, D_MODEL), row),
                  pl.BlockSpec((1, D_MODEL), const),
                  pl.BlockSpec((D_MODEL, IN_COLS_PAD), const),
                  pl.BlockSpec((1, LANES), const),
                  pl.BlockSpec((TM, LANES), tab_map),
                  pl.BlockSpec((TM, LANES), tab_map),
                  pl.BlockSpec((TM, LANES), tab_map)],
        out_specs=[pl.BlockSpec((TM, ML_WIDTH), row)] * 4
                  + [pl.BlockSpec((TM, SWA_WIDTH), row),
                     pl.BlockSpec((TM, LANES), row), pl.BlockSpec((TM, LANES), row),
                     pl.BlockSpec((TM, LANES), row)],
        out_shape=[f(ML_WIDTH)] * 4 + [f(SWA_WIDTH), f(LANES), f(LANES), f(LANES)],
        compiler_params=pltpu.CompilerParams(dimension_semantics=("arbitrary",),
                                             vmem_limit_bytes=VMEM_LIMIT),
        name="inproj",
    )(x2d, nw, w_perm, bias_pad, cos_t, sina_t, sinb_t)


def _mlstm_kernel(q_ref, k_ref, v_ref, og_ref, gc_ref, c0_ref, n0_ref, m0_ref, nw_ref,
                  hm_ref, c_ref, n_ref, m_ref, *, BB, L_in):
    LP = max(L_in, LANES)

    @pl.when(pl.program_id(1) == 0)
    def _():
        c_ref[...] = c0_ref[...]
        n_ref[...] = n0_ref[...]
        m_ref[...] = m0_ref[...]

    src = lax.broadcasted_iota(jnp.int32, (LP, LP), 0)
    qry = lax.broadcasted_iota(jnp.int32, (LP, LP), 1)
    visible = src <= qry
    tri_t = visible.astype(F32)
    row8 = lax.broadcasted_iota(jnp.int32, (8, LP), 0)
    lane1 = lax.broadcasted_iota(jnp.int32, (1, LANES), 1)
    hi = lax.Precision.HIGHEST

    def pad_rows(a):
        if L_in == LP:
            return a
        return jnp.concatenate([a, jnp.zeros((LP - L_in, a.shape[1]), a.dtype)], axis=0)

    streams = [(b, h) for b in range(BB) for h in range(ML_HEADS)]
    gr, b_r, m_all, qkv = {}, {}, {}, {}
    for b in range(BB):
        gc = gc_ref[b]
        if L_in < LP:
            lane_pad = lax.broadcasted_iota(jnp.int32, (LP - L_in, LANES), 1)
            fill = jnp.where(lane_pad < ML_HEADS, -jnp.inf, 0.0).astype(F32)
            gc = jnp.concatenate([gc, fill], axis=0)
        gr[b] = gc.T[:8]
        lf_r = jnp.where(row8 >= ML_HEADS, gr[b], 0.0)
        b_r[b] = jnp.dot(lf_r, tri_t, precision=hi, preferred_element_type=F32)
        m_all[b] = m_ref[b]
        qkv[b] = (pad_rows(q_ref[b]), pad_rows(k_ref[b]), pad_rows(v_ref[b]))

    st = {}
    for b, h in streams:
        sl = slice(h * ML_HEAD_DIM, (h + 1) * ML_HEAD_DIM)
        q, k, v = (a[:, sl] for a in qkv[b])
        ig_r = gr[b][h:h + 1, :]
        b_rh = b_r[b][ML_HEADS + h:ML_HEADS + h + 1, :]
        m_prev = m_all[b][:, h:h + 1]
        src_term = jnp.broadcast_to(ig_r - b_rh, (LP, LP)).T
        logd = jnp.where(visible, b_rh + src_term, -jnp.inf)
        m_inter = b_rh + m_prev
        m_t = jnp.maximum(m_inter, jnp.max(logd, axis=0, keepdims=True))
        b_last = b_rh[:, LP - 1:LP]
        logw = b_last - b_rh + ig_r
        m_new = jnp.maximum(b_last + m_prev, jnp.max(logw, axis=1, keepdims=True))
        st[b, h] = dict(sl=sl, q=q, k=k, qb=q.astype(BF16), kb=k.astype(BF16), v_t=v.T, m_t=m_t,
                        dmat=jnp.exp(logd - m_t), w_int=jnp.exp(m_inter - m_t), m_new=m_new,
                        w_r=jnp.exp(logw - m_new), decay=jnp.exp(b_last + m_prev - m_new),
                        c_old=c_ref[b, h], n_old=n_ref[b, h:h + 1, :])

    for key in streams:
        d = st[key]
        d["s"] = lax.dot_general(d["kb"], d["qb"], NT_DIMS, preferred_element_type=F32)
        d["cq"] = lax.dot_general(d["c_old"].astype(BF16), d["qb"], NT_DIMS, preferred_element_type=F32)
        d["nq"] = lax.dot_general(jnp.broadcast_to(d["n_old"], (8, ML_HEAD_DIM)), d["q"], NT_DIMS,
                                  precision=hi, preferred_element_type=F32)[0:1]

    for b, h in streams:
        d = st[b, h]
        c_ref[b, h] = d["decay"] * d["c_old"] + jnp.dot(
            (d["v_t"] * d["w_r"]).astype(BF16), d["kb"], preferred_element_type=F32)
        n_ref[b, h:h + 1, :] = d["decay"] * d["n_old"] + jnp.dot(
            jnp.broadcast_to(d["w_r"], (8, LP)), d["k"], precision=hi, preferred_element_type=F32)[0:1]

    for key in streams:
        d = st[key]
        qk = d["s"] * d["dmat"]
        num = d["w_int"] * d["cq"] + jnp.dot(d["v_t"].astype(BF16), qk.astype(BF16),
                                             preferred_element_type=F32)
        den = d["w_int"] * d["nq"] + jnp.sum(qk, axis=0, keepdims=True)
        d["hh"] = num / jnp.maximum(jnp.abs(den), jnp.exp(-d["m_t"]))

    for b in range(BB):
        og_all = og_ref[b]
        m_out = m_all[b]
        for h in range(ML_HEADS):
            d = st[b, h]
            hh, sl = d["hh"], d["sl"]
            y_t = hh * lax.rsqrt(jnp.mean(hh * hh, axis=0, keepdims=True) + NORM_EPS)
            y = y_t.T * nw_ref[:, sl]
            hm_ref[b, :, sl] = (og_all[:, sl] * y[:L_in]).astype(BF16)
            m_out = jnp.where(lane1 == h, d["m_new"], m_out)
        m_ref[b] = m_out


def _mlstm(qm, km, vm, og, gc, c0, n0, m0, nw, *, L_in):
    B, T, _ = qm.shape
    BB = 2
    nchunks = T // L_in
    tok = lambda n: pl.BlockSpec((BB, L_in, n), lambda g, c: (g, c, 0))
    st_c = pl.BlockSpec((BB, ML_HEADS, ML_HEAD_DIM, ML_HEAD_DIM), lambda g, c: (g, 0, 0, 0))
    st_n = pl.BlockSpec((BB, ML_HEADS, ML_HEAD_DIM), lambda g, c: (g, 0, 0))
    st_m = pl.BlockSpec((BB, 1, LANES), lambda g, c: (g, 0, 0))
    return pl.pallas_call(
        functools.partial(_mlstm_kernel, BB=BB, L_in=L_in),
        grid=(B // BB, nchunks),
        in_specs=[tok(ML_WIDTH), tok(ML_WIDTH), tok(ML_WIDTH), tok(ML_WIDTH), tok(LANES),
                  st_c, st_n, st_m, pl.BlockSpec((1, ML_WIDTH), lambda g, c: (0, 0))],
        out_specs=[tok(ML_WIDTH), st_c, st_n, st_m],
        out_shape=[jax.ShapeDtypeStruct((B, T, ML_WIDTH), BF16),
                   jax.ShapeDtypeStruct(c0.shape, F32),
                   jax.ShapeDtypeStruct(n0.shape, F32),
                   jax.ShapeDtypeStruct(m0.shape, F32)],
        compiler_params=pltpu.CompilerParams(dimension_semantics=("arbitrary", "arbitrary"),
                                             vmem_limit_bytes=VMEM_LIMIT),
        name="mlstm",
    )(qm, km, vm, og, gc, c0, n0, m0, nw)


def _sink_attention(q4, kw, vw, sink_col, bias):
    s = lax.dot_general(q4, kw, NT_DIMS, preferred_element_type=F32) * (SWA_HEAD_DIM ** -0.5)
    if bias is not None:
        s = s + bias
    mx = jnp.maximum(jnp.max(s, axis=1, keepdims=True), sink_col)
    p = jnp.exp(s - mx)
    den = jnp.sum(p, axis=1, keepdims=True) + jnp.exp(sink_col - mx)
    return jnp.dot(p.astype(BF16), vw, preferred_element_type=F32) / den


def _sink_column(sink_ref, g, rows):
    return jnp.concatenate(
        [jnp.broadcast_to(sink_ref[:, g * SWA_GROUP + j:g * SWA_GROUP + j + 1], (rows, 1))
         for j in range(SWA_GROUP)], axis=0)


def _swa_prompt_kernel(q_ref, kc_ref, kp_ref, vc_ref, vp_ref, sink_ref, bias_ref, o_ref):
    TQ = SWA_TQ
    q = q_ref[0]
    kcat = jnp.concatenate([kp_ref[0, TQ - WINDOW:, :], kc_ref[0]], axis=0).astype(BF16)
    vcat_t = jnp.concatenate([vp_ref[0, TQ - WINDOW:, :], vc_ref[0]], axis=0).T.astype(BF16)
    bias_t = bias_ref[0]
    outs = []
    for g in range(SWA_KV_HEADS):
        gs = slice(g * SWA_HEAD_DIM, (g + 1) * SWA_HEAD_DIM)
        heads = range(g * SWA_GROUP, (g + 1) * SWA_GROUP)
        q4 = jnp.concatenate([q[:, h * SWA_HEAD_DIM:(h + 1) * SWA_HEAD_DIM] for h in heads],
                             axis=0).astype(BF16)
        sink_row = jnp.concatenate([jnp.broadcast_to(sink_ref[:, h:h + 1], (1, TQ)) for h in heads], axis=1)
        s_t = lax.dot_general(kcat[:, gs], q4, NT_DIMS, preferred_element_type=F32)
        s_t = s_t * (SWA_HEAD_DIM ** -0.5) + bias_t
        mx = jnp.maximum(jnp.max(s_t, axis=0, keepdims=True), sink_row)
        p = jnp.exp(s_t - mx)
        den = jnp.sum(p, axis=0, keepdims=True) + jnp.exp(sink_row - mx)
        o_t = jnp.dot(vcat_t[gs, :], p.astype(BF16), preferred_element_type=F32) / den
        outs += [o_t[:, j * TQ:(j + 1) * TQ] for j in range(SWA_GROUP)]
    o_ref[0] = jnp.concatenate(outs, axis=0).T.astype(BF16)


def _swa_prompt_bias():
    TQ = SWA_TQ
    t = np.arange(SWA_GROUP * TQ) % TQ
    k = np.arange(WINDOW + TQ)
    qc = (t // SWA_CHUNK)[None, :]
    kc = (k // SWA_CHUNK)[:, None]
    visible = (kc >= qc) & (kc <= qc + WINDOW // SWA_CHUNK)
    first = visible & (k[:, None] >= WINDOW)
    return np.where(np.stack([first, visible]), 0.0, -np.inf).astype(np.float32)


def _swa_prompt(sq, sk, sv, sinks):
    B, T, _ = sq.shape
    TQ = SWA_TQ
    cur = lambda b, i: (b, i, 0)
    prev = lambda b, i: (b, jnp.maximum(i - 1, 0), 0)
    bias = jnp.asarray(_swa_prompt_bias())
    return pl.pallas_call(
        _swa_prompt_kernel,
        grid=(B, T // TQ),
        in_specs=[pl.BlockSpec((1, TQ, SWA_WIDTH), cur),
                  pl.BlockSpec((1, TQ, SWA_KV_WIDTH), cur), pl.BlockSpec((1, TQ, SWA_KV_WIDTH), prev),
                  pl.BlockSpec((1, TQ, SWA_KV_WIDTH), cur), pl.BlockSpec((1, TQ, SWA_KV_WIDTH), prev),
                  pl.BlockSpec((1, SWA_HEADS), lambda b, i: (0, 0)),
                  pl.BlockSpec((1,) + bias.shape[1:], lambda b, i: (jnp.minimum(i, 1), 0, 0))],
        out_specs=pl.BlockSpec((1, TQ, SWA_WIDTH), cur),
        out_shape=jax.ShapeDtypeStruct((B, T, SWA_WIDTH), BF16),
        compiler_params=pltpu.CompilerParams(dimension_semantics=("arbitrary", "arbitrary"),
                                             vmem_limit_bytes=VMEM_LIMIT),
        name="swa_prompt",
    )(sq, sk, sk, sv, sv, sinks, bias)


def _swa_sample_kernel(q_ref, kn_ref, vn_ref, kc_ref, vc_ref, sink_ref, o_ref, kw_ref, vw_ref):
    T = DEC_SEQ
    q = q_ref[0]
    k_all = jnp.concatenate([kc_ref[0], kn_ref[0]], axis=0)
    v_all = jnp.concatenate([vc_ref[0], vn_ref[0]], axis=0)
    kw_ref[0] = k_all[T:]
    vw_ref[0] = v_all[T:]
    kb, vb = k_all.astype(BF16), v_all.astype(BF16)
    for g in range(SWA_KV_HEADS):
        gs = slice(g * SWA_HEAD_DIM, (g + 1) * SWA_HEAD_DIM)
        q4 = jnp.concatenate(
            [q[:, (g * SWA_GROUP + j) * SWA_HEAD_DIM:(g * SWA_GROUP + j + 1) * SWA_HEAD_DIM]
             for j in range(SWA_GROUP)], axis=0).astype(BF16)
        o = _sink_attention(q4, kb[:, gs], vb[:, gs], _sink_column(sink_ref, g, T), None)
        for j in range(SWA_GROUP):
            hd = (g * SWA_GROUP + j) * SWA_HEAD_DIM
            o_ref[0, :, hd:hd + SWA_HEAD_DIM] = o[j * T:(j + 1) * T].astype(BF16)


def _swa_sample(sq, sk, sv, k_cache, v_cache, sinks):
    B, T, _ = sq.shape
    b3 = lambda b: (b, 0, 0)
    return pl.pallas_call(
        _swa_sample_kernel,
        grid=(B,),
        in_specs=[pl.BlockSpec((1, T, SWA_WIDTH), b3),
                  pl.BlockSpec((1, T, SWA_KV_WIDTH), b3), pl.BlockSpec((1, T, SWA_KV_WIDTH), b3),
                  pl.BlockSpec((1, WINDOW, SWA_KV_WIDTH), b3), pl.BlockSpec((1, WINDOW, SWA_KV_WIDTH), b3),
                  pl.BlockSpec((1, SWA_HEADS), lambda b: (0, 0))],
        out_specs=[pl.BlockSpec((1, T, SWA_WIDTH), b3),
                   pl.BlockSpec((1, WINDOW, SWA_KV_WIDTH), b3), pl.BlockSpec((1, WINDOW, SWA_KV_WIDTH), b3)],
        out_shape=[jax.ShapeDtypeStruct((B, T, SWA_WIDTH), BF16),
                   jax.ShapeDtypeStruct((B, WINDOW, SWA_KV_WIDTH), F32),
                   jax.ShapeDtypeStruct((B, WINDOW, SWA_KV_WIDTH), F32)],
        compiler_params=pltpu.CompilerParams(dimension_semantics=("arbitrary",),
                                             vmem_limit_bytes=VMEM_LIMIT),
        name="swa_sample",
    )(sq, sk, sv, k_cache, v_cache, sinks)


def _outproj_kernel(hm_ref, hs_ref, x_ref, wo_ref, nw_ref, wq_ref, x1_ref, h2_ref, q_ref):
    mix = (jnp.dot(hm_ref[...], wo_ref[:ML_WIDTH, :], preferred_element_type=F32)
           + jnp.dot(hs_ref[...], wo_ref[ML_WIDTH:, :], preferred_element_type=F32))
    x1 = x_ref[...] + mix
    x1_ref[...] = x1
    h2 = _rms(x1, nw_ref[...]).astype(BF16)
    h2_ref[...] = pltpu.bitcast(h2, jnp.uint32)
    q = jnp.dot(h2, wq_ref[...], preferred_element_type=F32)
    q_ref[...] = pltpu.bitcast(q.astype(BF16), jnp.uint32)


def _outproj(hm, hs, x2d, wo, nw, wq):
    T = x2d.shape[0]
    TM = TM_OUTPROJ
    QW = PEER_HEADS * PEER_KEY_DIM
    row = lambda i: (i, 0)
    const = lambda i: (0, 0)
    return pl.pallas_call(
        _outproj_kernel,
        grid=(T // TM,),
        in_specs=[pl.BlockSpec((TM, ML_WIDTH), row), pl.BlockSpec((TM, SWA_WIDTH), row),
                  pl.BlockSpec((TM, D_MODEL), row),
                  pl.BlockSpec((D_MODEL, D_MODEL), const), pl.BlockSpec((1, D_MODEL), const),
                  pl.BlockSpec((D_MODEL, QW), const)],
        out_specs=[pl.BlockSpec((TM, D_MODEL), row), pl.BlockSpec((TM // 2, D_MODEL), row),
                   pl.BlockSpec((TM // 2, QW), row)],
        out_shape=[jax.ShapeDtypeStruct((T, D_MODEL), F32),
                   jax.ShapeDtypeStruct((T // 2, D_MODEL), jnp.uint32),
                   jax.ShapeDtypeStruct((T // 2, QW), jnp.uint32)],
        compiler_params=pltpu.CompilerParams(dimension_semantics=("arbitrary",),
                                             vmem_limit_bytes=VMEM_LIMIT),
        name="outproj",
    )(hm, hs, x2d, wo, nw, wq)


_CAND_NB = [PEER_TOPK // (a + 1) for a in range(PEER_TOPK)]
_CAND_ROWS = 16 + 8 * 7 + 8


def _extract_top16(S, exact):
    R = S.shape[0]
    iota = lax.broadcasted_iota(jnp.int32, S.shape, 0)
    rank = jnp.full(S.shape, float(PEER_TOPK), F32)
    vals = []
    for r in range(PEER_TOPK):
        mx = jnp.max(S, axis=0, keepdims=True)
        if exact:
            idx = jnp.min(jnp.where(S == mx, iota, R), axis=0, keepdims=True)
            hit = iota == idx
        else:
            hit = S == mx
        rank = jnp.where(hit, float(r), rank)
        S = jnp.where(hit, -jnp.inf, S)
        vals.append(mx)
    return vals, rank


def _count_excess(flags):
    return jnp.abs(jnp.sum(flags, axis=0, keepdims=True) - float(PEER_TOPK))


def _route_tables(q, sk1_ref, sk2_ref, r2_ref, e2_ref, lim_ref, e1_ref, exact):
    N = q.shape[0]
    row8 = lax.broadcasted_iota(jnp.int32, (8, N), 0)
    rowc = lax.broadcasted_iota(jnp.int32, (_CAND_ROWS, N), 0)
    mid = rowc - 16
    flat = jnp.where(rowc < 16, rowc,
                     jnp.where(rowc < _CAND_ROWS - 8,
                               PEER_TOPK * ((mid >> 3) + 1) + (mid & 7),
                               PEER_TOPK * (rowc - (_CAND_ROWS - 16))))
    excess = jnp.zeros((1, N), F32)
    for h in range(PEER_HEADS):
        c0 = h * PEER_KEY_DIM
        s1 = lax.dot_general(sk1_ref[...], q[:, c0:c0 + PEER_HALF], NT_DIMS, preferred_element_type=F32)
        s2 = lax.dot_general(sk2_ref[...], q[:, c0 + PEER_HALF:c0 + PEER_KEY_DIM], NT_DIMS,
                             preferred_element_type=F32)
        t1, r1 = _extract_top16(s1, exact)
        t2, r2 = _extract_top16(s2, exact)
        t2_16 = jnp.concatenate(t2, axis=0)
        t2_8 = t2_16[:8]
        blocks = [t1[0] + t2_16, t1[1] + t2_8]
        for a in range(2, 8):
            blocks.append(jnp.where(row8 < _CAND_NB[a], t1[a] + t2_8, -jnp.inf))
        blocks.append(jnp.concatenate(t1[8:], axis=0) + t2[0])
        cand = jnp.concatenate(blocks, axis=0)
        sel = jnp.zeros(cand.shape, F32)
        work = cand
        for _ in range(PEER_TOPK):
            mx = jnp.max(work, axis=0, keepdims=True)
            if exact:
                idx = jnp.min(jnp.where(work == mx, flat, PEER_TOPK * PEER_TOPK), axis=0, keepdims=True)
                hit = flat == idx
            else:
                hit = work == mx
            sel = jnp.where(hit, 1.0, sel)
            work = jnp.where(hit, -jnp.inf, work)
        if not exact:
            excess = jnp.maximum(excess, _count_excess(jnp.where(r1 < float(PEER_TOPK), 1.0, 0.0)))
            excess = jnp.maximum(excess, _count_excess(jnp.where(r2 < float(PEER_TOPK), 1.0, 0.0)))
            excess = jnp.maximum(excess, _count_excess(sel))
        z = jnp.sum(jnp.where(sel > 0.0, jnp.exp(cand - cand[0:1]), 0.0), axis=0, keepdims=True)
        counts = [jnp.sum(sel[0:16], axis=0, keepdims=True)]
        for a in range(1, 8):
            counts.append(jnp.sum(sel[8 + 8 * a:16 + 8 * a], axis=0, keepdims=True))
        for a in range(8, PEER_TOPK):
            counts.append(sel[_CAND_ROWS - 16 + a:_CAND_ROWS - 15 + a])
        lim = jnp.zeros(r1.shape, F32)
        for a in range(PEER_TOPK):
            lim = jnp.where(r1 == float(a), counts[a], lim)
        r2_ref[h] = pltpu.bitcast(r2.astype(BF16), jnp.uint32)
        e2_ref[h] = pltpu.bitcast(jnp.exp(s2 - t2[0]).astype(BF16), jnp.uint32)
        lim_ref[h] = lim
        e1_ref[h] = jnp.exp(s1 - t1[0]) / z
    return excess


def _route_kernel(q_ref, sk1_ref, sk2_ref, r2_ref, e2_ref, lim_ref, e1_ref):
    q = pltpu.bitcast(q_ref[...], BF16)
    outs = (r2_ref, e2_ref, lim_ref, e1_ref)
    excess = _route_tables(q, sk1_ref, sk2_ref, *outs, exact=False)

    @pl.when(jnp.max(excess) > 0.0)
    def _():
        _route_tables(q, sk1_ref, sk2_ref, *outs, exact=True)


def _route(q, sk1, sk2):
    T = 2 * q.shape[0]
    TM = TM_ROUTE
    const = lambda i: (0, 0)
    tab = pl.BlockSpec((None, PEER_HEADS, N_KEYS, TM), lambda i: (i, 0, 0, 0))
    tab_packed = pl.BlockSpec((None, PEER_HEADS, N_KEYS // 2, TM), lambda i: (i, 0, 0, 0))
    tab_shape = jax.ShapeDtypeStruct((T // TM, PEER_HEADS, N_KEYS, TM), F32)
    tab_packed_shape = jax.ShapeDtypeStruct((T // TM, PEER_HEADS, N_KEYS // 2, TM), jnp.uint32)
    return pl.pallas_call(
        _route_kernel,
        grid=(T // TM,),
        in_specs=[pl.BlockSpec((TM // 2, PEER_HEADS * PEER_KEY_DIM), lambda i: (i, 0)),
                  pl.BlockSpec((N_KEYS, PEER_HALF), const), pl.BlockSpec((N_KEYS, PEER_HALF), const)],
        out_specs=[tab_packed, tab_packed, tab, tab],
        out_shape=[tab_packed_shape, tab_packed_shape, tab_shape, tab_shape],
        compiler_params=pltpu.CompilerParams(dimension_semantics=("arbitrary",),
                                             vmem_limit_bytes=VMEM_LIMIT),
        name="route",
    )(q, sk1, sk2)


def _peer_kernel(h2_ref, x1_ref, u_ref, vt_ref, r2_ref, e2_ref, lim_ref, e1_ref, nfw_ref,
                 y_ref, acc_ref, act0_ref, act1_ref, p0_ref, p1_ref, *, n_e, n_work):
    g = pl.program_id(0)
    per_blk = PEER_SUB * PEER_NSUB // N_KEYS
    e_b = jnp.clip(g - 1, 0, n_work - 1) % n_e
    e_c = jnp.clip(g - 2, 0, n_work - 1) % n_e
    act_bufs = (act0_ref, act1_ref)
    p_bufs = (p0_ref, p1_ref)

    @pl.when(g == 0)
    def _():
        for buf in act_bufs + p_bufs:
            buf[...] = jnp.zeros(buf.shape, buf.dtype)

    @pl.when(e_c == 0)
    def _():
        acc_ref[...] = jnp.zeros(acc_ref.shape, F32)

    def stages(slot_a, slot_b):
        def stage_a(s):
            h2 = pltpu.bitcast(h2_ref[...], BF16)
            u = pltpu.bitcast(u_ref[s * PEER_SUB // 2:(s + 1) * PEER_SUB // 2, :], BF16)
            act_bufs[slot_a][s * PEER_SUB:(s + 1) * PEER_SUB, :] = lax.dot_general(
                u, h2, NT_DIMS, preferred_element_type=F32)

        def stage_b(jbs):
            n_rb = N_KEYS // PEER_RB
            zero = jnp.zeros((PEER_RB, LANES), BF16)
            for jb in jbs:
                j = e_b * per_blk + jb
                for lh in range(TM_PEER // LANES):
                    cols = slice(lh * LANES, (lh + 1) * LANES)
                    gw = [None] * n_rb
                    for h in range(PEER_HEADS):
                        lim = jnp.broadcast_to(lim_ref[lh, h, pl.ds(j, 1), :], (PEER_RB, LANES)).astype(BF16)
                        e1 = jnp.broadcast_to(e1_ref[lh, h, pl.ds(j, 1), :], (PEER_RB, LANES)).astype(BF16)
                        for rb in range(n_rb):
                            words = slice(rb * PEER_RB // 2, (rb + 1) * PEER_RB // 2)
                            r2 = pltpu.bitcast(r2_ref[lh, h, words, :], BF16)
                            e2 = pltpu.bitcast(e2_ref[lh, h, words, :], BF16)
                            t = jnp.where(r2 < lim, e2, zero) * e1
                            gw[rb] = t if gw[rb] is None else gw[rb] + t
                    for rb in range(n_rb):
                        arows = slice(jb * N_KEYS + rb * PEER_RB, jb * N_KEYS + (rb + 1) * PEER_RB)
                        a = act_bufs[slot_b][arows, cols]
                        ga = 0.5 * a * (1.0 + lax.erf(a * np.float32(np.sqrt(0.5))))
                        p_bufs[slot_b][arows, cols] = gw[rb] * ga.astype(BF16)

        def stage_c():
            acc_ref[...] += jnp.dot(pltpu.bitcast(vt_ref[...], BF16), p_bufs[slot_a][...],
                                    preferred_element_type=F32)

        part = per_blk // PEER_NSUB
        for s in range(PEER_NSUB):
            stage_b(range(s * part, (s + 1) * part))
            stage_a(s)
        stage_c()

    @pl.when(g % 2 == 0)
    def _():
        stages(0, 1)

    @pl.when(g % 2 == 1)
    def _():
        stages(1, 0)

    @pl.when(jnp.logical_and(g >= 2, e_c == n_e - 1))
    def _():
        x = x1_ref[...] + acc_ref[...].T
        y_ref[...] = _rms(x, nfw_ref[...])


def _peer(h2, x1, u, vt, r2, e2, lim, e1, nfw):
    T = x1.shape[0]
    TM = TM_PEER
    ET = PEER_SUB * PEER_NSUB
    n_e = N_EXPERTS // ET
    n_work = (T // TM) * n_e
    item = lambda g, lag: jnp.clip(g - lag, 0, n_work - 1)
    rt = TM // TM_ROUTE
    tab = pl.BlockSpec((rt, PEER_HEADS, N_KEYS, TM_ROUTE), lambda g: (item(g, 1) // n_e, 0, 0, 0))
    tab_packed = pl.BlockSpec((rt, PEER_HEADS, N_KEYS // 2, TM_ROUTE),
                              lambda g: (item(g, 1) // n_e, 0, 0, 0))
    tok_c = lambda g: (item(g, 2) // n_e, 0)
    return pl.pallas_call(
        functools.partial(_peer_kernel, n_e=n_e, n_work=n_work),
        grid=(n_work + 2,),
        in_specs=[pl.BlockSpec((TM // 2, D_MODEL), lambda g: (item(g, 0) // n_e, 0)),
                  pl.BlockSpec((TM, D_MODEL), tok_c),
                  pl.BlockSpec((ET // 2, D_MODEL), lambda g: (item(g, 0) % n_e, 0)),
                  pl.BlockSpec((D_MODEL // 2, ET), lambda g: (0, item(g, 2) % n_e)),
                  tab_packed, tab_packed, tab, tab,
                  pl.BlockSpec((1, D_MODEL), lambda g: (0, 0))],
        out_specs=pl.BlockSpec((TM, D_MODEL), tok_c),
        out_shape=jax.ShapeDtypeStruct((T, D_MODEL), F32),
        scratch_shapes=[pltpu.VMEM((D_MODEL, TM), F32),
                        pltpu.VMEM((ET, TM), F32), pltpu.VMEM((ET, TM), F32),
                        pltpu.VMEM((ET, TM), BF16), pltpu.VMEM((ET, TM), BF16)],
        compiler_params=pltpu.CompilerParams(dimension_semantics=("arbitrary",),
                                             vmem_limit_bytes=VMEM_LIMIT),
        name="peer",
    )(h2, x1, u, vt, r2, e2, lim, e1, nfw)


def _rope_tables(pos):
    inv = ROPE_THETA ** (-jnp.arange(ROPE_HALF, dtype=F32) * 2.0 / ROPE_DIM)
    ang = pos.astype(F32)[:, None] * inv[None, :]
    cos, sin = jnp.cos(ang), jnp.sin(ang)
    n = pos.shape[0]
    rest = SWA_HEAD_DIM - ROPE_DIM
    zh = jnp.zeros((n, ROPE_HALF), F32)
    cos_h = jnp.concatenate([cos, cos, jnp.ones((n, rest), F32)], axis=1)
    sina_h = jnp.concatenate([-sin, zh, jnp.zeros((n, rest), F32)], axis=1)
    sinb_h = jnp.concatenate([zh, sin, jnp.zeros((n, rest), F32)], axis=1)
    rep = LANES // SWA_HEAD_DIM
    return tuple(jnp.tile(t, (1, rep)) for t in (cos_h, sina_h, sinb_h))


def _pack_kernel(x_ref, o_ref, *, transpose):
    x = x_ref[...]
    if transpose:
        x = x.T
    o_ref[...] = pltpu.bitcast(x.astype(BF16), jnp.uint32)


def _pack_expert_table(w, *, transpose):
    n, d = w.shape
    rows = PACK_ROWS
    if transpose:
        out_spec = pl.BlockSpec((d // 2, rows), lambda i: (0, i))
        out_shape = jax.ShapeDtypeStruct((d // 2, n), jnp.uint32)
    else:
        out_spec = pl.BlockSpec((rows // 2, d), lambda i: (i, 0))
        out_shape = jax.ShapeDtypeStruct((n // 2, d), jnp.uint32)
    return pl.pallas_call(
        functools.partial(_pack_kernel, transpose=transpose),
        grid=(n // rows,),
        in_specs=[pl.BlockSpec((rows, d), lambda i: (i, 0))],
        out_specs=out_spec,
        out_shape=out_shape,
        compiler_params=pltpu.CompilerParams(dimension_semantics=("arbitrary",),
                                             vmem_limit_bytes=VMEM_LIMIT),
        name="pack_vt" if transpose else "pack_u",
    )(w)


def _layer_tokens(x2d, tables, tab_map, W):
    return _inproj(x2d, W["norm_mix"], W["w_in"], W["bias"], *tables, tab_map)


def _ffn(hm, hs, x2d, W):
    x1, h2, q = _outproj(hm, hs, x2d, W["w_out"], W["norm_ffn"], W["w_q"])
    r2, e2, lim, e1 = _route(q, W["sk1"], W["sk2"])
    return _peer(h2, x1, W["u"], W["vt"], r2, e2, lim, e1, W["norm_final"])


def kernel(x_prompt, x_sample, cache_swa_k, cache_swa_v, state_mlstm_c, state_mlstm_n, state_mlstm_m,
           norm_mix_w, w_in, mlstm_if_bias, mlstm_norm_w, swa_sinks, w_out, norm_ffn_w,
           peer_w_q, peer_sub_keys_1, peer_sub_keys_2, peer_u, peer_v, norm_final_w):
    B, S, _ = x_prompt.shape
    DB, DS, _ = x_sample.shape
    l = 0
    wi = w_in[l]
    s_q = 4 * ML_WIDTH + 2 * ML_HEADS
    w_perm = jnp.concatenate(
        [wi[:, :4 * ML_WIDTH], wi[:, s_q:], wi[:, 4 * ML_WIDTH:s_q],
         jnp.zeros((D_MODEL, LANES - 2 * ML_HEADS), F32)], axis=1).astype(BF16)
    bias_pad = jnp.concatenate([mlstm_if_bias[l], jnp.zeros((LANES - 2 * ML_HEADS,), F32)])[None, :]
    W = {
        "norm_mix": norm_mix_w[l][None, :],
        "w_in": w_perm,
        "bias": bias_pad,
        "w_out": w_out[l].astype(BF16),
        "norm_ffn": norm_ffn_w[l][None, :],
        "w_q": peer_w_q[l].astype(BF16),
        "sk1": peer_sub_keys_1[l].astype(BF16),
        "sk2": peer_sub_keys_2[l].astype(BF16),
        "u": _pack_expert_table(peer_u[l], transpose=False),
        "vt": _pack_expert_table(peer_v[l], transpose=True),
        "norm_final": norm_final_w[None, :],
    }
    ml_nw = mlstm_norm_w[l][None, :]
    sinks = swa_sinks[l][None, :]

    xp = x_prompt.reshape(B * S, D_MODEL)
    tiles_per_seq = S // TM_INPROJ
    tabs_p = _rope_tables(jnp.arange(S, dtype=jnp.int32))
    qm, km, vm, og, sq, sk, sv, gc = _layer_tokens(xp, tabs_p, lambda i: (i % tiles_per_seq, 0), W)
    r3 = lambda a: a.reshape(B, S, a.shape[-1])
    zc = jnp.zeros((B, ML_HEADS, ML_HEAD_DIM, ML_HEAD_DIM), F32)
    zn = jnp.zeros((B, ML_HEADS, ML_HEAD_DIM), F32)
    zm = jnp.zeros((B, 1, LANES), F32)
    hm_p, c_p, n_p, m_p = _mlstm(r3(qm), r3(km), r3(vm), r3(og), r3(gc), zc, zn, zm, ml_nw, L_in=ML_CHUNK)
    sk3, sv3 = r3(sk), r3(sv)
    hs_p = _swa_prompt(r3(sq), sk3, sv3, sinks)
    y_p = _ffn(hm_p.reshape(B * S, ML_WIDTH), hs_p.reshape(B * S, SWA_WIDTH), xp, W)
    kv_shape = (1, B, WINDOW, SWA_KV_HEADS, SWA_HEAD_DIM)
    k_win_p = sk3[:, S - WINDOW:].reshape(kv_shape)
    v_win_p = sv3[:, S - WINDOW:].reshape(kv_shape)

    xs = x_sample.reshape(DB * DS, D_MODEL)
    pos_s = PAST_LEN + jnp.arange(DS, dtype=jnp.int32)
    tabs_s = tuple(jnp.tile(t, (DB, 1)) for t in _rope_tables(pos_s))
    qm, km, vm, og, sq, sk, sv, gc = _layer_tokens(xs, tabs_s, lambda i: (i, 0), W)
    r3s = lambda a: a.reshape(DB, DS, a.shape[-1])
    m0 = jnp.concatenate([state_mlstm_m[l], jnp.zeros((DB, LANES - ML_HEADS), F32)], axis=1)[:, None, :]
    hm_s, c_s, n_s, m_s = _mlstm(r3s(qm), r3s(km), r3s(vm), r3s(og), r3s(gc),
                                 state_mlstm_c[l], state_mlstm_n[l], m0, ml_nw, L_in=DS)
    kc = cache_swa_k[l].reshape(DB, WINDOW, SWA_KV_WIDTH)
    vc = cache_swa_v[l].reshape(DB, WINDOW, SWA_KV_WIDTH)
    hs_s, k_win_s, v_win_s = _swa_sample(r3s(sq), r3s(sk), r3s(sv), kc, vc, sinks)
    y_s = _ffn(hm_s.reshape(DB * DS, ML_WIDTH), hs_s.reshape(DB * DS, SWA_WIDTH), xs, W)
    kv_shape_s = (1, DB, WINDOW, SWA_KV_HEADS, SWA_HEAD_DIM)

    return (y_p.reshape(B, S, D_MODEL), y_s.reshape(DB, DS, D_MODEL),
            k_win_p, v_win_p, c_p[None], n_p[None], m_p[None, :, 0, :ML_HEADS],
            k_win_s.reshape(kv_shape_s), v_win_s.reshape(kv_shape_s),
            c_s[None], n_s[None], m_s[None, :, 0, :ML_HEADS])
```
